```python
import math
import jax, jax.numpy as jnp
from jax import lax
import numpy as np

D_MODEL = 1024
BATCH = 8
SEQ = 8192
DEPTH = 1
DEC_BATCH = 16
DEC_SEQ = 32
PAST_LEN = 1024

CHUNK = 64
MIX_WIDTH = D_MODEL
GDN_HEADS = 4
GDN_DK = MIX_WIDTH // 2 // GDN_HEADS
GDN_DV = MIX_WIDTH // 2 // GDN_HEADS
GDN_KEY_DIM = GDN_HEADS * GDN_DK
GDN_VAL_DIM = GDN_HEADS * GDN_DV
GDN_CONV_DIM = 2 * GDN_KEY_DIM + GDN_VAL_DIM
CONV_W = 4
DIFF_HEADS = 4
DIFF_DV = MIX_WIDTH // 2 // DIFF_HEADS
DIFF_DQK = DIFF_DV // 2
ROT_DIM = DIFF_DQK // 4
ROPE_THETA = 500000.0
Q_BLOCK = 128
N_MEM = 256
MEM_HEADS = 4
MEM_HD = 128
D_FF = 4 * D_MODEL
NORM_EPS = 1e-6
_IN_SIZES = (GDN_CONV_DIM, GDN_VAL_DIM, GDN_HEADS, GDN_HEADS,
             DIFF_HEADS * 2 * DIFF_DQK, DIFF_HEADS * 2 * DIFF_DQK, DIFF_HEADS * DIFF_DV)
D_IN = sum(_IN_SIZES)
IN_SPLITS = tuple(int(s) for s in np.cumsum(_IN_SIZES)[:-1])

kernel_name = 'hybrid_gdn_diffattn_stream_step'


def rms_norm(x, g):
    xf = x.astype(jnp.float32)
    y = xf * lax.rsqrt(jnp.mean(xf * xf, axis=-1, keepdims=True) + NORM_EPS)
    return (y * g.astype(jnp.float32)).astype(x.dtype)


def l2_normalize(x):
    xf = x.astype(jnp.float32)
    return xf * lax.rsqrt(jnp.sum(xf * xf, axis=-1, keepdims=True) + NORM_EPS)


def apply_partial_rope(x, pos):
    inv = ROPE_THETA ** (-jnp.arange(0, ROT_DIM, 2, dtype=jnp.float32) / ROT_DIM)
    ang = pos.astype(jnp.float32)[:, None] * inv[None, :]
    cos = jnp.cos(ang)[None, :, None, None, :]
    sin = jnp.sin(ang)[None, :, None, None, :]
    xr = x[..., :ROT_DIM].astype(jnp.float32)
    x1, x2 = xr[..., :ROT_DIM // 2], xr[..., ROT_DIM // 2:]
    rot = jnp.concatenate([x1 * cos - x2 * sin, x2 * cos + x1 * sin], axis=-1)
    return jnp.concatenate([rot.astype(x.dtype), x[..., ROT_DIM:]], axis=-1)


def causal_conv(x, prev, w):
    t = x.shape[1]
    xp = jnp.concatenate([prev.astype(x.dtype), x], axis=1)
    y = xp[:, 0:t] * w[0]
    for i in range(1, CONV_W):
        y = y + xp[:, i:i + t] * w[i]
    return y, xp[:, -(CONV_W - 1):]


def gated_delta_rule(q, k, v, g, beta, s0):
    b, t, h, _ = q.shape
    dv = v.shape[-1]
    c = min(t, CHUNK)
    n = t // c
    f32 = jnp.float32

    def blocks(a):
        return a.astype(f32).reshape((b, n, c) + a.shape[2:]).swapaxes(2, 3)

    qb, kb, vb, gb, bb = (blocks(a) for a in (q, k, v, g, beta))
    gc = jnp.cumsum(gb, axis=-1)
    idx = jnp.arange(c)
    tril = idx[:, None] >= idx[None, :]
    strict = idx[:, None] > idx[None, :]
    decay = jnp.exp(jnp.where(tril, gc[..., :, None] - gc[..., None, :], -jnp.inf))
    kbeta = kb * bb[..., None]
    m = jnp.where(strict, jnp.einsum('bnhid,bnhjd->bnhij', kbeta, kb) * decay, 0.0)
    eye = jnp.eye(c, dtype=f32)
    tinv = lax.linalg.triangular_solve(eye + m, jnp.broadcast_to(eye, m.shape), left_side=True, lower=True)
    u = tinv @ (vb * bb[..., None])
    w = tinv @ (kbeta * jnp.exp(gc)[..., None])
    aqk = jnp.einsum('bnhid,bnhjd->bnhij', qb, kb) * decay
    qg = qb * jnp.exp(gc)[..., None]
    kd = kb * jnp.exp(gc[..., -1:] - gc)[..., None]
    glast = jnp.exp(gc[..., -1])

    def step(s, xs):
        u_n, w_n, a_n, qg_n, kd_n, gl_n = xs
        v_new = u_n - jnp.einsum('bhck,bhkv->bhcv', w_n, s)
        o_n = jnp.einsum('bhck,bhkv->bhcv', qg_n, s) + jnp.einsum('bhij,bhjv->bhiv', a_n, v_new)
        s = s * gl_n[..., None, None] + jnp.einsum('bhck,bhcv->bhkv', kd_n, v_new)
        return s, o_n

    xs = tuple(jnp.moveaxis(a, 1, 0) for a in (u, w, aqk, qg, kd, glast))
    s_fin, o = lax.scan(step, s0.astype(f32), xs)
    o = jnp.moveaxis(o, 0, 1).swapaxes(2, 3).reshape(b, t, h, dv)
    return o, s_fin


def gdn_mixer(qkv, z, a, bgate, conv_prev, s0, conv_w, a_log, dt_bias, norm_g):
    b, t, _ = qkv.shape
    y, conv_new = causal_conv(qkv, conv_prev, conv_w)
    y = jax.nn.silu(y)
    q, k, v = jnp.split(y, [GDN_KEY_DIM, 2 * GDN_KEY_DIM], axis=-1)
    q = l2_normalize(q.reshape(b, t, GDN_HEADS, GDN_DK)) * (GDN_DK ** -0.5)
    k = l2_normalize(k.reshape(b, t, GDN_HEADS, GDN_DK))
    v = v.reshape(b, t, GDN_HEADS, GDN_DV)
    beta = jax.nn.sigmoid(bgate.astype(jnp.float32))
    g = -jnp.exp(a_log.astype(jnp.float32)) * jax.nn.softplus(a.astype(jnp.float32) + dt_bias.astype(jnp.float32))
    o, s_new = gated_delta_rule(q, k, v, g, beta, s0)
    o = rms_norm(o.astype(qkv.dtype), norm_g) * jax.nn.silu(z.reshape(b, t, GDN_HEADS, GDN_DV))
    return o.reshape(b, t, GDN_VAL_DIM), s_new, conv_new


def diff_attend(q, k, v, q_pos, k_pos, lam):
    s = jnp.einsum('bqhsd,bkhsd->bhsqk', q, k).astype(jnp.float32) * (DIFF_DQK ** -0.5)
    mask = (k_pos[None, :] // CHUNK) <= (q_pos[:, None] // CHUNK)
    s = jnp.where(mask, s, -jnp.inf)
    p = jax.nn.softmax(s, axis=-1)
    att = p[:, :, 0] - lam * p[:, :, 1]
    return jnp.einsum('bhqk,bkhd->bqhd', att.astype(v.dtype), v)


def diff_attention_blocked(q, k, v, pos, lam):
    b, t = q.shape[:2]
    nb = t // Q_BLOCK
    qb = q.reshape((b, nb, Q_BLOCK) + q.shape[2:]).swapaxes(0, 1)
    pb = pos.reshape(nb, Q_BLOCK)

    def one(args):
        q_blk, p_blk = args
        return diff_attend(q_blk, k, v, p_blk, pos, lam)

    o = lax.map(one, (qb, pb))
    return o.swapaxes(0, 1).reshape(b, t, DIFF_HEADS, DIFF_DV)


def cross_attend(h, mem_k, mem_v, w_mq, w_mo):
    b, t, _ = h.shape
    q = (h @ w_mq).reshape(b, t, MEM_HEADS, MEM_HD)
    s = jnp.einsum('bqhd,bkhd->bhqk', q, mem_k.astype(q.dtype)).astype(jnp.float32) * (MEM_HD ** -0.5)
    p = jax.nn.softmax(s, axis=-1)
    o = jnp.einsum('bhqk,bkhd->bqhd', p.astype(h.dtype), mem_v.astype(h.dtype))
    return o.reshape(b, t, MEM_HEADS * MEM_HD) @ w_mo


def encoder_layer(x, pos, conv_prev, s0, kv_prev, mem_k, mem_v, lw, layer_idx):
    (norm_mix_g, w_in, gdn_conv_w, gdn_a_log, gdn_dt_bias, gdn_norm_g, diff_lambda, diff_norm_g,
     w_out, norm_mem_g, w_mq, w_mo, norm_ffn_g, w_up, w_down) = lw
    b, t, _ = x.shape
    h = rms_norm(x, norm_mix_g)
    proj = h @ w_in
    qkv_g, z_g, a_g, b_g, q_d, k_d, v_d = jnp.split(proj, IN_SPLITS, axis=-1)
    if conv_prev is None:
        conv_prev = jnp.zeros((b, CONV_W - 1, GDN_CONV_DIM), proj.dtype)
    if s0 is None:
        s0 = jnp.zeros((b, GDN_HEADS, GDN_DK, GDN_DV), jnp.float32)
    o_gdn, s_new, conv_new = gdn_mixer(qkv_g, z_g, a_g, b_g, conv_prev, s0,
                                       gdn_conv_w, gdn_a_log, gdn_dt_bias, gdn_norm_g)
    q_d = apply_partial_rope(q_d.reshape(b, t, DIFF_HEADS, 2, DIFF_DQK), pos)
    k_d = apply_partial_rope(k_d.reshape(b, t, DIFF_HEADS, 2, DIFF_DQK), pos)
    v_d = v_d.reshape(b, t, DIFF_HEADS, DIFF_DV)
    lam_init = 0.8 - 0.6 * math.exp(-0.3 * layer_idx)
    lf = diff_lambda.astype(jnp.float32)
    lam = jnp.exp(jnp.sum(lf[0] * lf[1])) - jnp.exp(jnp.sum(lf[2] * lf[3])) + lam_init
    if kv_prev is None:
        o_diff = diff_attention_blocked(q_d, k_d, v_d, pos, lam)
    else:
        k_prev, v_prev = kv_prev
        p_len = k_prev.shape[1]
        k_all = jnp.concatenate([k_prev.reshape(b, p_len, DIFF_HEADS, 2, DIFF_DQK).astype(k_d.dtype), k_d], axis=1)
        v_all = jnp.concatenate([v_prev.astype(v_d.dtype), v_d], axis=1)
        o_diff = diff_attend(q_d, k_all, v_all, pos, jnp.arange(p_len + t), lam)
    o_diff = rms_norm(o_diff, diff_norm_g) * (1.0 - lam_init)
    mix = jnp.concatenate([o_gdn, o_diff.reshape(b, t, DIFF_HEADS * DIFF_DV)], axis=-1)
    x = x + mix @ w_out
    x = x + cross_attend(rms_norm(x, norm_mem_g), mem_k, mem_v, w_mq, w_mo)
    hf = rms_norm(x, norm_ffn_g)
    x = x + jnp.square(jax.nn.relu(hf @ w_up)) @ w_down
    return x, s_new, conv_new, k_d.reshape(b, t, DIFF_HEADS, 2 * DIFF_DQK), v_d


def setup_inputs(seed: int = 0) -> dict:
    key = jax.random.key(seed)
    ks = jax.random.split(key, 32)
    f32 = jnp.float32
    nrm = lambda k, shape, s: jax.random.normal(k, shape, f32) * s
    gain = lambda k, shape: 1.0 + 0.05 * jax.random.normal(k, shape, f32)
    dt = jnp.exp(jax.random.uniform(ks[16], (DEPTH, GDN_HEADS), f32, math.log(1e-3), math.log(1e-1)))
    return {
        'x_prompt': nrm(ks[0], (BATCH, SEQ, D_MODEL), 1.0),
        'x_sample': nrm(ks[1], (DEC_BATCH, DEC_SEQ, D_MODEL), 1.0),
        'mem_prompt': nrm(ks[2], (BATCH, N_MEM, D_MODEL), 1.0),
        'cache_diff_k': nrm(ks[3], (DEPTH, DEC_BATCH, PAST_LEN, DIFF_HEADS, 2 * DIFF_DQK), 1.0),
        'cache_diff_v': nrm(ks[4], (DEPTH, DEC_BATCH, PAST_LEN, DIFF_HEADS, DIFF_DV), 1.0),
        'cache_mem_k': nrm(ks[5], (DEPTH, DEC_BATCH, N_MEM, MEM_HEADS, MEM_HD), 1.0),
        'cache_mem_v': nrm(ks[6], (DEPTH, DEC_BATCH, N_MEM, MEM_HEADS, MEM_HD), 1.0),
        'state_gdn': nrm(ks[7], (DEPTH, DEC_BATCH, GDN_HEADS, GDN_DK, GDN_DV), 0.1),
        'state_gdn_conv': nrm(ks[8], (DEPTH, DEC_BATCH, CONV_W - 1, GDN_CONV_DIM), 1.0),
        'norm_mix_g': gain(ks[9], (DEPTH, D_MODEL)),
        'w_in': nrm(ks[10], (DEPTH, D_MODEL, D_IN), D_MODEL ** -0.5),
        'gdn_conv_w': nrm(ks[11], (DEPTH, CONV_W, GDN_CONV_DIM), CONV_W ** -0.5),
        'gdn_a_log': jnp.log(jax.random.uniform(ks[12], (DEPTH, GDN_HEADS), f32, 1.0, 16.0)),
        'gdn_dt_bias': jnp.log(jnp.expm1(dt)),
        'gdn_norm_g': gain(ks[13], (DEPTH, GDN_DV)),
        'diff_lambda': nrm(ks[14], (DEPTH, 4, DIFF_DQK), 0.1),
        'diff_norm_g': gain(ks[15], (DEPTH, DIFF_DV)),
        'w_out': nrm(ks[17], (DEPTH, MIX_WIDTH, D_MODEL), MIX_WIDTH ** -0.5),
        'norm_mem_g': gain(ks[18], (DEPTH, D_MODEL)),
        'mem_norm_g': gain(ks[19], (DEPTH, D_MODEL)),
        'w_mq': nrm(ks[20], (DEPTH, D_MODEL, MEM_HEADS * MEM_HD), D_MODEL ** -0.5),
        'w_mkv': nrm(ks[21], (DEPTH, D_MODEL, 2 * MEM_HEADS * MEM_HD), D_MODEL ** -0.5),
        'w_mo': nrm(ks[22], (DEPTH, MEM_HEADS * MEM_HD, D_MODEL), (MEM_HEADS * MEM_HD) ** -0.5),
        'norm_ffn_g': gain(ks[23], (DEPTH, D_MODEL)),
        'w_up': nrm(ks[24], (DEPTH, D_MODEL, D_FF), D_MODEL ** -0.5),
        'w_down': nrm(ks[25], (DEPTH, D_FF, D_MODEL), 0.5 * D_FF ** -0.5),
        'final_norm_g': gain(ks[26], (D_MODEL,)),
    }


def reference(x_prompt, x_sample, mem_prompt, cache_diff_k, cache_diff_v, cache_mem_k, cache_mem_v,
              state_gdn, state_gdn_conv, norm_mix_g, w_in, gdn_conv_w, gdn_a_log, gdn_dt_bias, gdn_norm_g,
              diff_lambda, diff_norm_g, w_out, norm_mem_g, mem_norm_g, w_mq, w_mkv, w_mo,
              norm_ffn_g, w_up, w_down, final_norm_g):
    bp, tp, _ = x_prompt.shape
    ts = x_sample.shape[1]
    p_len = cache_diff_k.shape[2]
    pos_p = jnp.arange(tp)
    pos_s = p_len + jnp.arange(ts)
    hp, hs = x_prompt, x_sample
    p_s, p_c, p_k, p_v, p_mk, p_mv = [], [], [], [], [], []
    s_s, s_c, s_k, s_v = [], [], [], []
    for l in range(DEPTH):
        lw = (norm_mix_g[l], w_in[l], gdn_conv_w[l], gdn_a_log[l], gdn_dt_bias[l], gdn_norm_g[l],
              diff_lambda[l], diff_norm_g[l], w_out[l], norm_mem_g[l], w_mq[l], w_mo[l],
              norm_ffn_g[l], w_up[l], w_down[l])
        mh = rms_norm(mem_prompt, mem_norm_g[l])
        mk, mv = jnp.split(mh @ w_mkv[l], 2, axis=-1)
        mk = mk.reshape(bp, -1, MEM_HEADS, MEM_HD)
        mv = mv.reshape(bp, -1, MEM_HEADS, MEM_HD)
        hp, sp, cp, kp, vp = encoder_layer(hp, pos_p, None, None, None, mk, mv, lw, l)
        hs, ss, cs, ks_, vs = encoder_layer(hs, pos_s, state_gdn_conv[l], state_gdn[l],
                                            (cache_diff_k[l], cache_diff_v[l]),
                                            cache_mem_k[l], cache_mem_v[l], lw, l)
        p_s.append(sp); p_c.append(cp); p_k.append(kp); p_v.append(vp); p_mk.append(mk); p_mv.append(mv)
        s_s.append(ss); s_c.append(cs); s_k.append(ks_); s_v.append(vs)
    y_prompt = rms_norm(hp, final_norm_g)
    y_sample = rms_norm(hs, final_norm_g)
    return (y_prompt, y_sample,
            jnp.stack(p_s), jnp.stack(p_c), jnp.stack(p_k), jnp.stack(p_v), jnp.stack(p_mk), jnp.stack(p_mv),
            jnp.stack(s_s), jnp.stack(s_c), jnp.stack(s_k), jnp.stack(s_v))
```

```python
import functools
import math

import jax
import jax.numpy as jnp
from jax import lax
from jax.experimental import pallas as pl
from jax.experimental.pallas import tpu as pltpu

D_MODEL = 1024
CHUNK = 64
HEADS = 4
HEAD_DIM = 128
KEY_DIM = HEADS * HEAD_DIM
CONV_DIM = 3 * KEY_DIM
CONV_W = 4
DQK = 64
ROT_DIM = 16
ROPE_THETA = 500000.0
N_MEM = 256
D_FF = 4 * D_MODEL
NORM_EPS = 1e-6
MAIN_COLS = CONV_DIM + 4 * KEY_DIM
LANES = 128
CONV_PAD = 8
VMEM_LIMIT = 56 * 1024 * 1024

F32 = jnp.float32
BF16 = jnp.bfloat16
HIGHEST = lax.Precision.HIGHEST


def _dot(a, b, precision=None):
    return jnp.dot(a, b, preferred_element_type=F32, precision=precision)


def _dot_nt(a, b):
    return lax.dot_general(a, b, (((1,), (1,)), ((), ())), preferred_element_type=F32)


def _dot_tn(a, b):
    return lax.dot_general(a, b, (((0,), (0,)), ((), ())), preferred_element_type=F32)


def _rms(x, g):
    return x * lax.rsqrt(jnp.mean(x * x, axis=-1, keepdims=True) + NORM_EPS) * g


def _sigmoid(x):
    return 1.0 / (1.0 + jnp.exp(-x))


def _params(*sem):
    return pltpu.CompilerParams(dimension_semantics=sem, vmem_limit_bytes=VMEM_LIMIT)


def _proj_in_kernel(x_ref, g_ref, wm_ref, wab_ref, alog_ref, dtb_ref, cos_ref, sna_ref, snb_ref,
                    qkv_ref, z_ref, gb_ref, q_ref, kf_ref, kb_ref, vf_ref, vb_ref):
    x = x_ref[...]
    h = _rms(x, g_ref[...]).astype(BF16)
    main = _dot(h, wm_ref[...])
    ab = _dot(h, wab_ref[...])
    qkv_ref[...] = main[:, :CONV_DIM]
    z_ref[...] = main[:, CONV_DIM:CONV_DIM + KEY_DIM]
    xa = ab + dtb_ref[...]
    softplus = jnp.maximum(xa, 0.0) + jnp.log1p(jnp.exp(-jnp.abs(xa)))
    lane = lax.broadcasted_iota(jnp.int32, ab.shape, 1)
    gb_ref[...] = jnp.where(lane < HEADS, -jnp.exp(alog_ref[...]) * softplus, _sigmoid(ab))
    cos, sna, snb = cos_ref[...], sna_ref[...], snb_ref[...]
    q0 = CONV_DIM + KEY_DIM
    for hd in range(HEADS):
        sl = slice(hd * HEAD_DIM, (hd + 1) * HEAD_DIM)
        qh = main[:, q0 + hd * HEAD_DIM:q0 + (hd + 1) * HEAD_DIM]
        kh = main[:, q0 + KEY_DIM + hd * HEAD_DIM:q0 + KEY_DIM + (hd + 1) * HEAD_DIM]
        vh = main[:, q0 + 2 * KEY_DIM + hd * HEAD_DIM:q0 + 2 * KEY_DIM + (hd + 1) * HEAD_DIM]
        qr = qh * cos + pltpu.roll(qh, LANES - ROT_DIM // 2, 1) * sna + pltpu.roll(qh, ROT_DIM // 2, 1) * snb
        kr = kh * cos + pltpu.roll(kh, LANES - ROT_DIM // 2, 1) * sna + pltpu.roll(kh, ROT_DIM // 2, 1) * snb
        q_ref[:, sl] = (qr * (DQK ** -0.5)).astype(BF16)
        kf_ref[:, sl] = kr
        kb_ref[:, sl] = kr.astype(BF16)
        vf_ref[:, sl] = vh
        vb_ref[:, sl] = vh.astype(BF16)


def _rope_tables(pos):
    half = ROT_DIM // 2
    inv = ROPE_THETA ** (-jnp.arange(0, ROT_DIM, 2, dtype=F32) / ROT_DIM)
    ang = pos.astype(F32)[:, None] * inv[None, :]
    cos, sin = jnp.cos(ang), jnp.sin(ang)
    t = pos.shape[0]
    pad = jnp.zeros((t, DQK - ROT_DIM), F32)
    cos64 = jnp.concatenate([cos, cos, pad + 1.0], axis=1)
    sna64 = jnp.concatenate([-sin, jnp.zeros_like(sin), pad], axis=1)
    snb64 = jnp.concatenate([jnp.zeros_like(sin), sin, pad], axis=1)
    return tuple(jnp.concatenate([a, a], axis=1) for a in (cos64, sna64, snb64))


def _proj_in(x2d, pos_rows, tab_rows, w, tm):
    n = x2d.shape[0]
    nt = tab_rows // tm
    cos, sna, snb = _rope_tables(pos_rows)
    row = lambda i: (i, 0)
    fix = lambda i: (0, 0)
    tab = lambda i: (i % nt, 0)
    out_shape = (
        jax.ShapeDtypeStruct((n, CONV_DIM), F32),
        jax.ShapeDtypeStruct((n, KEY_DIM), F32),
        jax.ShapeDtypeStruct((n, LANES), F32),
        jax.ShapeDtypeStruct((n, KEY_DIM), BF16),
        jax.ShapeDtypeStruct((n, KEY_DIM), F32),
        jax.ShapeDtypeStruct((n, KEY_DIM), BF16),
        jax.ShapeDtypeStruct((n, KEY_DIM), F32),
        jax.ShapeDtypeStruct((n, KEY_DIM), BF16),
    )
    return pl.pallas_call(
        _proj_in_kernel,
        grid=(n // tm,),
        in_specs=[
            pl.BlockSpec((tm, D_MODEL), row),
            pl.BlockSpec((1, D_MODEL), fix),
            pl.BlockSpec((D_MODEL, MAIN_COLS), fix),
            pl.BlockSpec((D_MODEL, LANES), fix),
            pl.BlockSpec((1, LANES), fix),
            pl.BlockSpec((1, LANES), fix),
            pl.BlockSpec((tm, LANES), tab),
            pl.BlockSpec((tm, LANES), tab),
            pl.BlockSpec((tm, LANES), tab),
        ],
        out_specs=tuple(pl.BlockSpec((tm, s.shape[1]), row) for s in out_shape),
        out_shape=out_shape,
        compiler_params=_params("parallel"),
        name="proj_in",
    )(x2d, w["norm_mix_g"], w["w_main"], w["w_ab"], w["a_log"], w["dt_bias"], cos, sna, snb)


def _gdn_kernel(qkv_ref, z_ref, gb_ref, gbt_ref, cprev_ref, s0_ref, cw_ref, ng_ref,
                o_ref, sout_ref, xp_ref, y_ref, s_ref, *, c, nchunks):
    j = pl.program_id(1)
    rows = c * nchunks
    hc = HEADS * c

    @pl.when(j == 0)
    def _():
        xp_ref[0:CONV_PAD, :] = cprev_ref[0]
        s_ref[...] = s0_ref[0]

    xp_ref[CONV_PAD:CONV_PAD + rows, :] = qkv_ref[0]
    cw = cw_ref[...]
    off = CONV_PAD - (CONV_W - 1)
    y = xp_ref[off:off + rows, :] * cw[0:1, :]
    for i in range(1, CONV_W):
        y = y + xp_ref[off + i:off + i + rows, :] * cw[i:i + 1, :]
    tail = xp_ref[rows:rows + CONV_PAD, :]
    xp_ref[0:CONV_PAD, :] = tail
    y_ref[...] = y * _sigmoid(y)

    ri = lax.broadcasted_iota(jnp.int32, (hc, hc), 0)
    cj = lax.broadcasted_iota(jnp.int32, (hc, hc), 1)
    same = (ri // c) == (cj // c)
    tril_bd = same & (ri >= cj)
    strict_bd = same & (ri > cj)
    eye_bd = (ri == cj).astype(F32)
    r1 = lax.broadcasted_iota(jnp.int32, (c, c), 0)
    c1 = lax.broadcasted_iota(jnp.int32, (c, c), 1)
    tril_c = (r1 >= c1).astype(F32)
    triu_c = (r1 <= c1).astype(F32)
    ng = ng_ref[...]

    def chunk(ci, carry):
        r0 = pl.multiple_of(ci * c, c)
        rs = pl.ds(r0, c)
        gcol = gb_ref[0, rs, :]
        gc_col = _dot(tril_c, gcol, HIGHEST)
        gc_row = _dot(gbt_ref[0, ci], triu_c, HIGHEST)
        ks, kbs, qs, vbs, kbgs, qgs, kds, dcol, grow, glast = [], [], [], [], [], [], [], [], [], []
        for hd in range(HEADS):
            sl = slice(hd * HEAD_DIM, (hd + 1) * HEAD_DIM)
            qh = y_ref[rs, hd * HEAD_DIM:(hd + 1) * HEAD_DIM]
            kh = y_ref[rs, KEY_DIM + hd * HEAD_DIM:KEY_DIM + (hd + 1) * HEAD_DIM]
            vh = y_ref[rs, 2 * KEY_DIM + hd * HEAD_DIM:2 * KEY_DIM + (hd + 1) * HEAD_DIM]
            qh = qh * lax.rsqrt(jnp.sum(qh * qh, axis=-1, keepdims=True) + NORM_EPS) * (HEAD_DIM ** -0.5)
            kh = kh * lax.rsqrt(jnp.sum(kh * kh, axis=-1, keepdims=True) + NORM_EPS)
            g_h = gc_col[:, hd:hd + 1]
            beta_h = gcol[:, HEADS + hd:HEADS + hd + 1]
            g_last = gc_col[c - 1:c, hd:hd + 1]
            eg = jnp.exp(g_h)
            kb = kh * beta_h
            ks.append(kh)
            kbs.append(kb)
            qs.append(qh)
            vbs.append(vh * beta_h)
            kbgs.append(kb * eg)
            qgs.append(qh * eg)
            kds.append(kh * jnp.exp(g_last - g_h))
            glast.append(jnp.exp(g_last))
            dcol.append(jnp.broadcast_to(g_h, (c, hc)))
            grow.append(gc_row[hd:hd + 1, :])
        k_s = jnp.concatenate(ks, axis=0).astype(BF16)
        q_s = jnp.concatenate(qs, axis=0).astype(BF16)
        dmat = jnp.concatenate(dcol, axis=0) - jnp.concatenate(grow, axis=1)
        decay = jnp.exp(jnp.where(tril_bd, dmat, -jnp.inf))
        a_kk = _dot_nt(jnp.concatenate(kbs, axis=0).astype(BF16), k_s)
        x = jnp.where(strict_bd, -(a_kk * decay), 0.0)
        p = eye_bd + x
        xp = x
        for _ in range(int(math.log2(c)) - 1):
            xb = xp.astype(BF16)
            xp = _dot(xb, xb)
            p = p + _dot(p.astype(BF16), xp.astype(BF16))
        rhs = jnp.concatenate([jnp.concatenate(vbs, axis=0), jnp.concatenate(kbgs, axis=0)], axis=1)
        uw = _dot(p.astype(BF16), rhs.astype(BF16))
        a_qk = (_dot_nt(q_s, k_s) * decay).astype(BF16)
        vnews = []
        qss = []
        for hd in range(HEADS):
            hs = slice(hd * c, (hd + 1) * c)
            s_h = s_ref[hd].astype(BF16)
            wq = jnp.concatenate([uw[hs, HEAD_DIM:], qgs[hd]], axis=0).astype(BF16)
            ws = _dot(wq, s_h)
            v_new = uw[hs, :HEAD_DIM] - ws[:c]
            vnews.append(v_new)
            qss.append(ws[c:])
            s_ref[hd] = s_ref[hd] * glast[hd] + _dot_tn(kds[hd].astype(BF16), v_new.astype(BF16))
        o_s = jnp.concatenate(qss, axis=0) + _dot(a_qk, jnp.concatenate(vnews, axis=0).astype(BF16))
        for hd in range(HEADS):
            sl = slice(hd * HEAD_DIM, (hd + 1) * HEAD_DIM)
            zh = z_ref[0, rs, sl]
            o_h = _rms(o_s[hd * c:(hd + 1) * c], ng) * (zh * _sigmoid(zh))
            o_ref[0, rs, sl] = o_h.astype(BF16)
        return carry

    lax.fori_loop(0, nchunks, chunk, 0)

    @pl.when(j == pl.num_programs(1) - 1)
    def _():
        sout_ref[0] = s_ref[...]


def _gdn(qkv, z, gb, conv_prev, s0, w, nchunks):
    b, t, _ = qkv.shape
    c = min(t, CHUNK)
    rows = c * nchunks
    cprev = jnp.concatenate([jnp.zeros((b, CONV_PAD - (CONV_W - 1), CONV_DIM), F32), conv_prev], axis=1)
    gbt = gb[:, :, :2 * HEADS].reshape(b, t // c, c, 2 * HEADS).swapaxes(2, 3)
    blk = lambda i, j: (i, j, 0)
    kern = functools.partial(_gdn_kernel, c=c, nchunks=nchunks)
    return pl.pallas_call(
        kern,
        grid=(b, t // rows),
        in_specs=[
            pl.BlockSpec((1, rows, CONV_DIM), blk),
            pl.BlockSpec((1, rows, KEY_DIM), blk),
            pl.BlockSpec((1, rows, LANES), blk),
            pl.BlockSpec((1, nchunks, 2 * HEADS, c), lambda i, j: (i, j, 0, 0)),
            pl.BlockSpec((1, CONV_PAD, CONV_DIM), lambda i, j: (i, 0, 0)),
            pl.BlockSpec((1, HEADS, HEAD_DIM, HEAD_DIM), lambda i, j: (i, 0, 0, 0)),
            pl.BlockSpec((CONV_W, CONV_DIM), lambda i, j: (0, 0)),
            pl.BlockSpec((1, HEAD_DIM), lambda i, j: (0, 0)),
        ],
        out_specs=(
            pl.BlockSpec((1, rows, KEY_DIM), blk),
            pl.BlockSpec((1, HEADS, HEAD_DIM, HEAD_DIM), lambda i, j: (i, 0, 0, 0)),
        ),
        out_shape=(
            jax.ShapeDtypeStruct((b, t, KEY_DIM), BF16),
            jax.ShapeDtypeStruct((b, HEADS, HEAD_DIM, HEAD_DIM), F32),
        ),
        scratch_shapes=[
            pltpu.VMEM((rows + CONV_PAD, CONV_DIM), F32),
            pltpu.VMEM((rows, CONV_DIM), F32),
            pltpu.VMEM((HEADS, HEAD_DIM, HEAD_DIM), F32),
        ],
        compiler_params=_params("parallel", "arbitrary"),
        name="gdn",
    )(qkv, z, gb, gbt, cprev, s0, w["conv_w"], w["gdn_norm_g"])


def _diff_attn_kernel(q_ref, k_ref, v_ref, lam_ref, ng_ref, o_ref, *, tq, tk, n_keys, q_pos0, lam_init):
    qi = pl.program_id(2)
    q = q_ref[0]
    lane = lax.broadcasted_iota(jnp.int32, q.shape, 1)
    zero = jnp.zeros_like(q)
    qs = jnp.concatenate([jnp.where(lane < DQK, q, zero), jnp.where(lane >= DQK, q, zero)], axis=0)
    pos_lo = q_pos0 + qi * tq
    k_lo = jnp.minimum((pos_lo // CHUNK + 1) * CHUNK, n_keys)
    k_hi = jnp.minimum(((pos_lo + tq - 1) // CHUNK + 1) * CHUNK, n_keys)
    n_full = k_lo // tk
    n_all = (k_hi + tk - 1) // tk

    def step(jb, carry, masked):
        m, l, acc = carry
        ks = pl.ds(pl.multiple_of(jb * tk, tk), tk)
        s = _dot_nt(qs, k_ref[0, ks, :])
        if masked:
            row = lax.broadcasted_iota(jnp.int32, s.shape, 0)
            qpos = pos_lo + jnp.where(row >= tq, row - tq, row)
            kpos = jb * tk + lax.broadcasted_iota(jnp.int32, s.shape, 1)
            vis = kpos < jnp.minimum((qpos // CHUNK + 1) * CHUNK, n_keys)
            s = jnp.where(vis, s, -jnp.inf)
        m_new = jnp.maximum(m, jnp.max(s, axis=-1, keepdims=True))
        alpha = jnp.exp(m - m_new)
        p = jnp.exp(s - m_new)
        l = alpha * l + jnp.sum(p, axis=-1, keepdims=True)
        acc = alpha * acc + _dot(p.astype(BF16), v_ref[0, ks, :])
        return m_new, l, acc

    init = (jnp.full((2 * tq, 1), -jnp.inf, F32), jnp.zeros((2 * tq, 1), F32),
            jnp.zeros((2 * tq, HEAD_DIM), F32))
    carry = lax.fori_loop(0, n_full, functools.partial(step, masked=False), init)
    m, l, acc = lax.fori_loop(n_full, n_all, functools.partial(step, masked=True), carry)
    o = acc / l
    lf = lam_ref[...]
    lam = (jnp.exp(jnp.sum(lf[0:1] * lf[1:2], axis=-1, keepdims=True))
           - jnp.exp(jnp.sum(lf[2:3] * lf[3:4], axis=-1, keepdims=True)) + lam_init)
    od = o[:tq] - lam * o[tq:]
    o_ref[0] = (_rms(od, ng_ref[...]) * (1.0 - lam_init)).astype(BF16)


def _diff_attn(q, k, v, w, n_keys, q_pos0, tq, tk):
    b, t, _ = q.shape
    tkp = k.shape[1]
    lam_init = 0.8 - 0.6 * math.exp(-0.3 * 0)
    kern = functools.partial(_diff_attn_kernel, tq=tq, tk=tk, n_keys=n_keys, q_pos0=q_pos0,
                             lam_init=lam_init)
    return pl.pallas_call(
        kern,
        grid=(b, HEADS, t // tq),
        in_specs=[
            pl.BlockSpec((1, tq, HEAD_DIM), lambda i, h, j: (i, j, h)),
            pl.BlockSpec((1, tkp, HEAD_DIM), lambda i, h, j: (i, 0, h)),
            pl.BlockSpec((1, tkp, HEAD_DIM), lambda i, h, j: (i, 0, h)),
            pl.BlockSpec((4, DQK), lambda i, h, j: (0, 0)),
            pl.BlockSpec((1, HEAD_DIM), lambda i, h, j: (0, 0)),
        ],
        out_specs=pl.BlockSpec((1, tq, HEAD_DIM), lambda i, h, j: (i, j, h)),
        out_shape=jax.ShapeDtypeStruct((b, t, KEY_DIM), BF16),
        compiler_params=_params("parallel", "parallel", "arbitrary"),
        name="diff_attn",
    )(q, k, v, w["diff_lambda"], w["diff_norm_g"])


def _post_mix_kernel(x_ref, og_ref, od_ref, mk_ref, mv_ref, wo_ref, g_ref, wq_ref, wmo_ref, o_ref):
    x1 = x_ref[0] + _dot(og_ref[0], wo_ref[0:KEY_DIM, :]) + _dot(od_ref[0], wo_ref[KEY_DIM:, :])
    h2 = _rms(x1, g_ref[...]).astype(BF16)
    qm = _dot(h2, wq_ref[...]).astype(BF16)
    outs = []
    for hd in range(HEADS):
        sl = slice(hd * HEAD_DIM, (hd + 1) * HEAD_DIM)
        mk = mk_ref[0, :, sl].astype(BF16)
        mv = mv_ref[0, :, sl].astype(BF16)
        s = _dot_nt(qm[:, sl], mk) * (HEAD_DIM ** -0.5)
        p = jnp.exp(s - jnp.max(s, axis=-1, keepdims=True))
        l = jnp.sum(p, axis=-1, keepdims=True)
        outs.append(_dot((p / l).astype(BF16), mv))
    om = jnp.concatenate(outs, axis=1).astype(BF16)
    o_ref[0] = x1 + _dot(om, wmo_ref[...])


def _post_mix(x, og, od, mk, mv, w, tm):
    b, t, _ = x.shape
    blk = lambda i, j: (i, j, 0)
    fix = lambda i, j: (0, 0)
    mem = lambda i, j: (i, 0, 0)
    return pl.pallas_call(
        _post_mix_kernel,
        grid=(b, t // tm),
        in_specs=[
            pl.BlockSpec((1, tm, D_MODEL), blk),
            pl.BlockSpec((1, tm, KEY_DIM), blk),
            pl.BlockSpec((1, tm, KEY_DIM), blk),
            pl.BlockSpec((1, N_MEM, KEY_DIM), mem),
            pl.BlockSpec((1, N_MEM, KEY_DIM), mem),
            pl.BlockSpec((D_MODEL, D_MODEL), fix),
            pl.BlockSpec((1, D_MODEL), fix),
            pl.BlockSpec((D_MODEL, KEY_DIM), fix),
            pl.BlockSpec((KEY_DIM, D_MODEL), fix),
        ],
        out_specs=pl.BlockSpec((1, tm, D_MODEL), blk),
        out_shape=jax.ShapeDtypeStruct((b, t, D_MODEL), F32),
        compiler_params=_params("parallel", "parallel"),
        name="post_mix",
    )(x, og, od, mk, mv, w["w_out"], w["norm_mem_g"], w["w_mq"], w["w_mo"])


def _mlp_kernel(x_ref, g_ref, wu_ref, wd_ref, fg_ref, o_ref, *, ff_blk):
    x = x_ref[...]
    hf = _rms(x, g_ref[...]).astype(BF16)
    acc = x
    for c0 in range(0, D_FF, ff_blk):
        u = jnp.maximum(_dot(hf, wu_ref[:, c0:c0 + ff_blk]), 0.0)
        acc = acc + _dot((u * u).astype(BF16), wd_ref[c0:c0 + ff_blk, :])
    o_ref[...] = _rms(acc, fg_ref[...])


def _mlp(x2d, w, tm):
    n = x2d.shape[0]
    row = lambda i: (i, 0)
    fix = lambda i: (0, 0)
    return pl.pallas_call(
        functools.partial(_mlp_kernel, ff_blk=D_MODEL),
        grid=(n // tm,),
        in_specs=[
            pl.BlockSpec((tm, D_MODEL), row),
            pl.BlockSpec((1, D_MODEL), fix),
            pl.BlockSpec((D_MODEL, D_FF), fix),
            pl.BlockSpec((D_FF, D_MODEL), fix),
            pl.BlockSpec((1, D_MODEL), fix),
        ],
        out_specs=pl.BlockSpec((tm, D_MODEL), row),
        out_shape=jax.ShapeDtypeStruct((n, D_MODEL), F32),
        compiler_params=_params("parallel"),
        name="mlp",
    )(x2d, w["norm_ffn_g"], w["w_up"], w["w_down"], w["final_norm_g"])


def _mem_kv_kernel(m_ref, g_ref, w_ref, k_ref, v_ref):
    mh = _rms(m_ref[...], g_ref[...]).astype(BF16)
    kv = _dot(mh, w_ref[...])
    k_ref[...] = kv[:, :KEY_DIM]
    v_ref[...] = kv[:, KEY_DIM:]


def _mem_kv(mem2d, w, tm):
    n = mem2d.shape[0]
    row = lambda i: (i, 0)
    fix = lambda i: (0, 0)
    return pl.pallas_call(
        _mem_kv_kernel,
        grid=(n // tm,),
        in_specs=[
            pl.BlockSpec((tm, D_MODEL), row),
            pl.BlockSpec((1, D_MODEL), fix),
            pl.BlockSpec((D_MODEL, 2 * KEY_DIM), fix),
        ],
        out_specs=(pl.BlockSpec((tm, KEY_DIM), row), pl.BlockSpec((tm, KEY_DIM), row)),
        out_shape=(jax.ShapeDtypeStruct((n, KEY_DIM), F32), jax.ShapeDtypeStruct((n, KEY_DIM), F32)),
        compiler_params=_params("parallel"),
        name="mem_kv",
    )(mem2d, w["mem_norm_g"], w["w_mkv"])


def _tile(n, pref):
    t = min(n, pref)
    assert n % t == 0, (n, t)
    return t


def _layer(x, pos0, conv_prev, s0, kv_prev, mem_k, mem_v, w):
    b, t, _ = x.shape
    n = b * t
    c = min(t, CHUNK)
    x2d = x.reshape(n, D_MODEL)
    tm = _tile(n, 512)
    tab_rows = t if t % tm == 0 else n
    pos_rows = pos0 + (jnp.arange(tab_rows) % t)
    qkv, z, gb, qd, kf, kb, vf, vb = _proj_in(x2d, pos_rows, tab_rows, w, tm)
    r3 = lambda a: a.reshape(b, t, a.shape[-1])
    qkv3 = r3(qkv)
    og, s_new = _gdn(qkv3, r3(z), r3(gb), conv_prev, s0, w, nchunks=_tile(t // c, 8))
    conv_new = qkv3[:, t - (CONV_W - 1):, :]
    kb3, vb3 = r3(kb), r3(vb)
    if kv_prev is not None:
        kb3 = jnp.concatenate([kv_prev[0].astype(BF16), kb3], axis=1)
        vb3 = jnp.concatenate([kv_prev[1].astype(BF16), vb3], axis=1)
    n_keys = kb3.shape[1]
    tk = _tile(n_keys, 512) if n_keys % LANES == 0 else LANES
    pad = (-n_keys) % tk
    if pad:
        kb3 = jnp.pad(kb3, ((0, 0), (0, pad), (0, 0)))
        vb3 = jnp.pad(vb3, ((0, 0), (0, pad), (0, 0)))
    od = _diff_attn(r3(qd), kb3, vb3, w, n_keys=n_keys, q_pos0=pos0, tq=_tile(t, 256), tk=tk)
    x2 = _post_mix(x, og, od, mem_k, mem_v, w, tm=_tile(t, 512))
    y = _mlp(x2.reshape(n, D_MODEL), w, tm).reshape(b, t, D_MODEL)
    return y, s_new, conv_new, r3(kf), r3(vf)


def kernel(x_prompt, x_sample, mem_prompt, cache_diff_k, cache_diff_v, cache_mem_k, cache_mem_v, state_gdn, state_gdn_conv, norm_mix_g, w_in, gdn_conv_w, gdn_a_log, gdn_dt_bias, gdn_norm_g, diff_lambda, diff_norm_g, w_out, norm_mem_g, mem_norm_g, w_mq, w_mkv, w_mo, norm_ffn_g, w_up, w_down, final_norm_g):
    bp, tp, _ = x_prompt.shape
    bs, ts, _ = x_sample.shape
    p_len = cache_diff_k.shape[2]
    depth = w_in.shape[0]
    assert depth == 1
    l = 0
    wi = w_in[l]
    sp = [CONV_DIM, CONV_DIM + KEY_DIM, CONV_DIM + KEY_DIM + HEADS, CONV_DIM + KEY_DIM + 2 * HEADS]
    w_ab = jnp.concatenate([wi[:, sp[1]:sp[3]], jnp.zeros((D_MODEL, LANES - 2 * HEADS), F32)], axis=1)
    lanes_pad = lambda a: jnp.concatenate([a, jnp.zeros((LANES - a.shape[0],), F32)])[None, :]
    row = lambda a: a.reshape(1, -1)
    w = {
        "norm_mix_g": row(norm_mix_g[l]),
        "w_main": jnp.concatenate([wi[:, :sp[1]], wi[:, sp[3]:]], axis=1).astype(BF16),
        "w_ab": w_ab.astype(BF16),
        "a_log": lanes_pad(gdn_a_log[l]),
        "dt_bias": lanes_pad(gdn_dt_bias[l]),
        "conv_w": gdn_conv_w[l],
        "gdn_norm_g": row(gdn_norm_g[l]),
        "diff_lambda": diff_lambda[l],
        "diff_norm_g": row(diff_norm_g[l]),
        "w_out": w_out[l].astype(BF16),
        "norm_mem_g": row(norm_mem_g[l]),
        "mem_norm_g": row(mem_norm_g[l]),
        "w_mq": w_mq[l].astype(BF16),
        "w_mkv": w_mkv[l].astype(BF16),
        "w_mo": w_mo[l].astype(BF16),
        "norm_ffn_g": row(norm_ffn_g[l]),
        "w_up": w_up[l].astype(BF16),
        "w_down": w_down[l].astype(BF16),
        "final_norm_g": row(final_norm_g),
    }
    n_mem = mem_prompt.shape[1]
    mk, mv = _mem_kv(mem_prompt.reshape(bp * n_mem, D_MODEL), w, _tile(bp * n_mem, 512))
    mk = mk.reshape(bp, n_mem, KEY_DIM)
    mv = mv.reshape(bp, n_mem, KEY_DIM)

    zeros_conv = jnp.zeros((bp, CONV_W - 1, CONV_DIM), F32)
    zeros_state = jnp.zeros((bp, HEADS, HEAD_DIM, HEAD_DIM), F32)
    yp, sp_, cp, kp, vp = _layer(x_prompt, 0, zeros_conv, zeros_state, None, mk, mv, w)
    kv_prev = (cache_diff_k[l].reshape(bs, p_len, KEY_DIM), cache_diff_v[l].reshape(bs, p_len, KEY_DIM))
    ys, ss, cs, ks_, vs = _layer(x_sample, p_len, state_gdn_conv[l], state_gdn[l], kv_prev,
                                 cache_mem_k[l].reshape(bs, n_mem, KEY_DIM),
                                 cache_mem_v[l].reshape(bs, n_mem, KEY_DIM), w)
    h4 = lambda a: a.reshape(a.shape[0], a.shape[1], HEADS, HEAD_DIM)[None]
    return (yp, ys, sp_[None], cp[None], h4(kp), h4(vp), h4(mk), h4(mv),
            ss[None], cs[None], h4(ks_), h4(vs))
```

```python
import functools
import math

import jax
import jax.numpy as jnp
from jax import lax
from jax.experimental import pallas as pl
from jax.experimental.pallas import tpu as pltpu

D_MODEL = 1024
CHUNK = 64
HEADS = 4
HEAD_DIM = 128
KEY_DIM = HEADS * HEAD_DIM
CONV_DIM = 3 * KEY_DIM
CONV_W = 4
DQK = 64
ROT_DIM = 16
ROPE_THETA = 500000.0
N_MEM = 256
D_FF = 4 * D_MODEL
NORM_EPS = 1e-6
MAIN_COLS = CONV_DIM + 4 * KEY_DIM
LANES = 128
CONV_PAD = 8
VMEM_LIMIT = 56 * 1024 * 1024

F32 = jnp.float32
BF16 = jnp.bfloat16
HIGHEST = lax.Precision.HIGHEST


def _dot(a, b, precision=None):
    return jnp.dot(a, b, preferred_element_type=F32, precision=precision)


def _dot_nt(a, b):
    return lax.dot_general(a, b, (((1,), (1,)), ((), ())), preferred_element_type=F32)


def _dot_tn(a, b):
    return lax.dot_general(a, b, (((0,), (0,)), ((), ())), preferred_element_type=F32)


def _rms(x, g):
    return x * lax.rsqrt(jnp.mean(x * x, axis=-1, keepdims=True) + NORM_EPS) * g


def _sigmoid(x):
    return 1.0 / (1.0 + jnp.exp(-x))


def _params(*sem):
    return pltpu.CompilerParams(dimension_semantics=sem, vmem_limit_bytes=VMEM_LIMIT)


def _proj_in_kernel(x_ref, g_ref, wm_ref, wab_ref, alog_ref, dtb_ref, cos_ref, sna_ref, snb_ref,
                    qkv_ref, z_ref, gb_ref, q_ref, kf_ref, kb_ref, vf_ref, vb_ref, vt_ref):
    x = x_ref[...]
    h = _rms(x, g_ref[...]).astype(BF16)
    main = _dot(h, wm_ref[...])
    ab = _dot(h, wab_ref[...])
    qkv_ref[...] = main[:, :CONV_DIM]
    z_ref[...] = main[:, CONV_DIM:CONV_DIM + KEY_DIM]
    xa = ab + dtb_ref[...]
    softplus = jnp.maximum(xa, 0.0) + jnp.log1p(jnp.exp(-jnp.abs(xa)))
    lane = lax.broadcasted_iota(jnp.int32, ab.shape, 1)
    gb_ref[...] = jnp.where(lane < HEADS, -jnp.exp(alog_ref[...]) * softplus, _sigmoid(ab))
    cos, sna, snb = cos_ref[...], sna_ref[...], snb_ref[...]
    q0 = CONV_DIM + KEY_DIM
    for hd in range(HEADS):
        sl = slice(hd * HEAD_DIM, (hd + 1) * HEAD_DIM)
        qh = main[:, q0 + hd * HEAD_DIM:q0 + (hd + 1) * HEAD_DIM]
        kh = main[:, q0 + KEY_DIM + hd * HEAD_DIM:q0 + KEY_DIM + (hd + 1) * HEAD_DIM]
        vh = main[:, q0 + 2 * KEY_DIM + hd * HEAD_DIM:q0 + 2 * KEY_DIM + (hd + 1) * HEAD_DIM]
        qr = qh * cos + pltpu.roll(qh, LANES - ROT_DIM // 2, 1) * sna + pltpu.roll(qh, ROT_DIM // 2, 1) * snb
        kr = kh * cos + pltpu.roll(kh, LANES - ROT_DIM // 2, 1) * sna + pltpu.roll(kh, ROT_DIM // 2, 1) * snb
        q_ref[:, sl] = (qr * (DQK ** -0.5 * math.log2(math.e))).astype(BF16)
        kf_ref[:, sl] = kr
        kb_ref[:, sl] = kr.astype(BF16)
        vf_ref[:, sl] = vh
        vb_ref[:, sl] = vh.astype(BF16)
        vt_ref[0, sl, :] = vh.T.astype(BF16)


def _rope_tables(pos):
    half = ROT_DIM // 2
    inv = ROPE_THETA ** (-jnp.arange(0, ROT_DIM, 2, dtype=F32) / ROT_DIM)
    ang = pos.astype(F32)[:, None] * inv[None, :]
    cos, sin = jnp.cos(ang), jnp.sin(ang)
    t = pos.shape[0]
    pad = jnp.zeros((t, DQK - ROT_DIM), F32)
    cos64 = jnp.concatenate([cos, cos, pad + 1.0], axis=1)
    sna64 = jnp.concatenate([-sin, jnp.zeros_like(sin), pad], axis=1)
    snb64 = jnp.concatenate([jnp.zeros_like(sin), sin, pad], axis=1)
    return tuple(jnp.concatenate([a, a], axis=1) for a in (cos64, sna64, snb64))


def _proj_in(x2d, pos_rows, tab_rows, w, tm):
    n = x2d.shape[0]
    nt = tab_rows // tm
    cos, sna, snb = _rope_tables(pos_rows)
    row = lambda i: (i, 0)
    fix = lambda i: (0, 0)
    tab = lambda i: (i % nt, 0)
    out_shape = (
        jax.ShapeDtypeStruct((n, CONV_DIM), F32),
        jax.ShapeDtypeStruct((n, KEY_DIM), F32),
        jax.ShapeDtypeStruct((n, LANES), F32),
        jax.ShapeDtypeStruct((n, KEY_DIM), BF16),
        jax.ShapeDtypeStruct((n, KEY_DIM), F32),
        jax.ShapeDtypeStruct((n, KEY_DIM), BF16),
        jax.ShapeDtypeStruct((n, KEY_DIM), F32),
        jax.ShapeDtypeStruct((n, KEY_DIM), BF16),
        jax.ShapeDtypeStruct((n // tm, KEY_DIM, tm), BF16),
    )
    out_specs = tuple(pl.BlockSpec((tm, s.shape[1]), row) for s in out_shape[:-1])
    out_specs += (pl.BlockSpec((1, KEY_DIM, tm), lambda i: (i, 0, 0)),)
    return pl.pallas_call(
        _proj_in_kernel,
        grid=(n // tm,),
        in_specs=[
            pl.BlockSpec((tm, D_MODEL), row),
            pl.BlockSpec((1, D_MODEL), fix),
            pl.BlockSpec((D_MODEL, MAIN_COLS), fix),
            pl.BlockSpec((D_MODEL, LANES), fix),
            pl.BlockSpec((1, LANES), fix),
            pl.BlockSpec((1, LANES), fix),
            pl.BlockSpec((tm, LANES), tab),
            pl.BlockSpec((tm, LANES), tab),
            pl.BlockSpec((tm, LANES), tab),
        ],
        out_specs=out_specs,
        out_shape=out_shape,
        compiler_params=_params("parallel"),
        name="proj_in",
    )(x2d, w["norm_mix_g"], w["w_main"], w["w_ab"], w["a_log"], w["dt_bias"], cos, sna, snb)


def _gdn_kernel(qkv_ref, z_ref, gb_ref, gbt_ref, cprev_ref, s0_ref, cw_ref, ng_ref,
                o_ref, sout_ref, xp_ref, y_ref, s_ref, *, c, nchunks):
    j = pl.program_id(1)
    rows = c * nchunks
    hc = HEADS * c

    @pl.when(j == 0)
    def _():
        xp_ref[0:CONV_PAD, :] = cprev_ref[0]
        s_ref[...] = s0_ref[0]

    xp_ref[CONV_PAD:CONV_PAD + rows, :] = qkv_ref[0]
    cw = cw_ref[...]
    off = CONV_PAD - (CONV_W - 1)
    y = xp_ref[off:off + rows, :] * cw[0:1, :]
    for i in range(1, CONV_W):
        y = y + xp_ref[off + i:off + i + rows, :] * cw[i:i + 1, :]
    tail = xp_ref[rows:rows + CONV_PAD, :]
    xp_ref[0:CONV_PAD, :] = tail
    y_ref[...] = y * _sigmoid(y)

    ri = lax.broadcasted_iota(jnp.int32, (hc, hc), 0)
    cj = lax.broadcasted_iota(jnp.int32, (hc, hc), 1)
    same = (ri // c) == (cj // c)
    tril_bd = same & (ri >= cj)
    strict_bd = same & (ri > cj)
    eye_bd = (ri == cj).astype(F32)
    r1 = lax.broadcasted_iota(jnp.int32, (c, c), 0)
    c1 = lax.broadcasted_iota(jnp.int32, (c, c), 1)
    tril_c = (r1 >= c1).astype(F32)
    triu_c = (r1 <= c1).astype(F32)
    ng = ng_ref[...]

    def chunk(ci, carry):
        r0 = pl.multiple_of(ci * c, c)
        rs = pl.ds(r0, c)
        gcol = gb_ref[0, rs, :]
        gc_col = _dot(tril_c, gcol, HIGHEST)
        gc_row = _dot(gbt_ref[0, ci], triu_c, HIGHEST)
        ks, kbs, qs, vbs, kbgs, qgs, kds, dcol, grow, glast = [], [], [], [], [], [], [], [], [], []
        for hd in range(HEADS):
            sl = slice(hd * HEAD_DIM, (hd + 1) * HEAD_DIM)
            qh = y_ref[rs, hd * HEAD_DIM:(hd + 1) * HEAD_DIM]
            kh = y_ref[rs, KEY_DIM + hd * HEAD_DIM:KEY_DIM + (hd + 1) * HEAD_DIM]
            vh = y_ref[rs, 2 * KEY_DIM + hd * HEAD_DIM:2 * KEY_DIM + (hd + 1) * HEAD_DIM]
            qh = qh * lax.rsqrt(jnp.sum(qh * qh, axis=-1, keepdims=True) + NORM_EPS) * (HEAD_DIM ** -0.5)
            kh = kh * lax.rsqrt(jnp.sum(kh * kh, axis=-1, keepdims=True) + NORM_EPS)
            g_h = gc_col[:, hd:hd + 1]
            beta_h = gcol[:, HEADS + hd:HEADS + hd + 1]
            g_last = gc_col[c - 1:c, hd:hd + 1]
            eg = jnp.exp(g_h)
            kb = kh * beta_h
            ks.append(kh)
            kbs.append(kb)
            qs.append(qh)
            vbs.append(vh * beta_h)
            kbgs.append(kb * eg)
            qgs.append(qh * eg)
            kds.append(kh * jnp.exp(g_last - g_h))
            glast.append(jnp.exp(g_last))
            dcol.append(jnp.broadcast_to(g_h, (c, hc)))
            grow.append(gc_row[hd:hd + 1, :])
        k_s = jnp.concatenate(ks, axis=0).astype(BF16)
        q_s = jnp.concatenate(qs, axis=0).astype(BF16)
        dmat = jnp.concatenate(dcol, axis=0) - jnp.concatenate(grow, axis=1)
        decay = jnp.exp(jnp.where(tril_bd, dmat, -jnp.inf))
        a_kk = _dot_nt(jnp.concatenate(kbs, axis=0).astype(BF16), k_s)
        x = jnp.where(strict_bd, -(a_kk * decay), 0.0)
        p = eye_bd + x
        xp = x
        for _ in range(int(math.log2(c)) - 1):
            xb = xp.astype(BF16)
            xp = _dot(xb, xb)
            p = p + _dot(p.astype(BF16), xp.astype(BF16))
        rhs = jnp.concatenate([jnp.concatenate(vbs, axis=0), jnp.concatenate(kbgs, axis=0)], axis=1)
        uw = _dot(p.astype(BF16), rhs.astype(BF16))
        a_qk = (_dot_nt(q_s, k_s) * decay).astype(BF16)
        vnews = []
        qss = []
        for hd in range(HEADS):
            hs = slice(hd * c, (hd + 1) * c)
            s_h = s_ref[hd].astype(BF16)
            wq = jnp.concatenate([uw[hs, HEAD_DIM:], qgs[hd]], axis=0).astype(BF16)
            ws = _dot(wq, s_h)
            v_new = uw[hs, :HEAD_DIM] - ws[:c]
            vnews.append(v_new)
            qss.append(ws[c:])
            s_ref[hd] = s_ref[hd] * glast[hd] + _dot_tn(kds[hd].astype(BF16), v_new.astype(BF16))
        o_s = jnp.concatenate(qss, axis=0) + _dot(a_qk, jnp.concatenate(vnews, axis=0).astype(BF16))
        for hd in range(HEADS):
            sl = slice(hd * HEAD_DIM, (hd + 1) * HEAD_DIM)
            zh = z_ref[0, rs, sl]
            o_h = _rms(o_s[hd * c:(hd + 1) * c], ng) * (zh * _sigmoid(zh))
            o_ref[0, rs, sl] = o_h.astype(BF16)
        return carry

    lax.fori_loop(0, nchunks, chunk, 0)

    @pl.when(j == pl.num_programs(1) - 1)
    def _():
        sout_ref[0] = s_ref[...]


def _gdn(qkv, z, gb, conv_prev, s0, w, nchunks):
    b, t, _ = qkv.shape
    c = min(t, CHUNK)
    rows = c * nchunks
    cprev = jnp.concatenate([jnp.zeros((b, CONV_PAD - (CONV_W - 1), CONV_DIM), F32), conv_prev], axis=1)
    gbt = gb[:, :, :2 * HEADS].reshape(b, t // c, c, 2 * HEADS).swapaxes(2, 3)
    blk = lambda i, j: (i, j, 0)
    kern = functools.partial(_gdn_kernel, c=c, nchunks=nchunks)
    return pl.pallas_call(
        kern,
        grid=(b, t // rows),
        in_specs=[
            pl.BlockSpec((1, rows, CONV_DIM), blk),
            pl.BlockSpec((1, rows, KEY_DIM), blk),
            pl.BlockSpec((1, rows, LANES), blk),
            pl.BlockSpec((1, nchunks, 2 * HEADS, c), lambda i, j: (i, j, 0, 0)),
            pl.BlockSpec((1, CONV_PAD, CONV_DIM), lambda i, j: (i, 0, 0)),
            pl.BlockSpec((1, HEADS, HEAD_DIM, HEAD_DIM), lambda i, j: (i, 0, 0, 0)),
            pl.BlockSpec((CONV_W, CONV_DIM), lambda i, j: (0, 0)),
            pl.BlockSpec((1, HEAD_DIM), lambda i, j: (0, 0)),
        ],
        out_specs=(
            pl.BlockSpec((1, rows, KEY_DIM), blk),
            pl.BlockSpec((1, HEADS, HEAD_DIM, HEAD_DIM), lambda i, j: (i, 0, 0, 0)),
        ),
        out_shape=(
            jax.ShapeDtypeStruct((b, t, KEY_DIM), BF16),
            jax.ShapeDtypeStruct((b, HEADS, HEAD_DIM, HEAD_DIM), F32),
        ),
        scratch_shapes=[
            pltpu.VMEM((rows + CONV_PAD, CONV_DIM), F32),
            pltpu.VMEM((rows, CONV_DIM), F32),
            pltpu.VMEM((HEADS, HEAD_DIM, HEAD_DIM), F32),
        ],
        compiler_params=_params("parallel", "arbitrary"),
        name="gdn",
    )(qkv, z, gb, gbt, cprev, s0, w["conv_w"], w["gdn_norm_g"])


def _diff_attn_kernel(q_ref, k_ref, vt_ref, lam_ref, ng_ref, o_ref, s0_ref, s1_ref, m_ref, l_ref, acc_ref,
                      *, tq, tqp, tk, nblk, n_keys, q_pos0, lam_init):
    qi = pl.program_id(2)
    q = q_ref[0]
    if tqp > tq:
        q = jnp.concatenate([q, jnp.zeros((tqp - tq, HEAD_DIM), q.dtype)], axis=0)
    lane = lax.broadcasted_iota(jnp.int32, q.shape, 1)
    zero = jnp.zeros_like(q)
    qs = jnp.concatenate([jnp.where(lane < DQK, q, zero), jnp.where(lane >= DQK, q, zero)], axis=0)
    pos_lo = q_pos0 + qi * tq
    k_lo = jnp.minimum((pos_lo // CHUNK + 1) * CHUNK, n_keys)
    k_hi = jnp.minimum(((pos_lo + tq - 1) // CHUNK + 1) * CHUNK, n_keys)
    n_full = k_lo // tk
    n_all = (k_hi + tk - 1) // tk

    def scores(jb, s_ref):
        jl = jnp.minimum(jb, nblk - 1)
        ks = pl.ds(pl.multiple_of(jl * tk, tk), tk)
        s_ref[...] = _dot_nt(k_ref[0, ks, :], qs)

    def mask(jb, s_ref):
        @pl.when(jb >= n_full)
        def _():
            col = lax.broadcasted_iota(jnp.int32, s_ref.shape, 1)
            qpos = pos_lo + jnp.where(col >= tqp, col - tqp, col)
            kpos = jb * tk + lax.broadcasted_iota(jnp.int32, s_ref.shape, 0)
            vis = kpos < jnp.minimum((qpos // CHUNK + 1) * CHUNK, n_keys)
            s_ref[...] = jnp.where(vis, s_ref[...], -jnp.inf)

    def softmax_pv(jb, s_ref):
        st = s_ref[...]
        m_old = m_ref[...]
        m_new = jnp.maximum(m_old, jnp.max(st, axis=0, keepdims=True))
        alpha = jnp.exp2(m_old - m_new)
        p = jnp.exp2(st - m_new)
        l_ref[...] = alpha * l_ref[...] + jnp.sum(p, axis=0, keepdims=True)
        acc_ref[...] = alpha * acc_ref[...] + _dot(vt_ref[jnp.minimum(jb, nblk - 1)], p.astype(BF16))
        m_ref[...] = m_new

    m_ref[...] = jnp.full(m_ref.shape, -jnp.inf, F32)
    l_ref[...] = jnp.zeros(l_ref.shape, F32)
    acc_ref[...] = jnp.zeros(acc_ref.shape, F32)
    scores(0, s0_ref)

    def full_pair(i, carry):
        j = 2 * i
        scores(j + 1, s1_ref)
        softmax_pv(j, s0_ref)
        scores(j + 2, s0_ref)
        softmax_pv(j + 1, s1_ref)
        return carry

    def edge_pair(i, carry):
        j = 2 * i
        mask(j, s0_ref)
        scores(j + 1, s1_ref)
        softmax_pv(j, s0_ref)

        @pl.when(j + 1 < n_all)
        def _():
            mask(j + 1, s1_ref)
            scores(j + 2, s0_ref)
            softmax_pv(j + 1, s1_ref)
        return carry

    lax.fori_loop(0, n_full // 2, full_pair, 0)
    lax.fori_loop(n_full // 2, (n_all + 1) // 2, edge_pair, 0)
    o = acc_ref[...] * (1.0 / l_ref[...])
    lf = lam_ref[...]
    lam = (jnp.exp(jnp.sum(lf[0:1] * lf[1:2], axis=-1, keepdims=True))
           - jnp.exp(jnp.sum(lf[2:3] * lf[3:4], axis=-1, keepdims=True)) + lam_init)
    od = o[:, :tqp] - lam * o[:, tqp:]
    ms = jnp.mean(od * od, axis=0, keepdims=True)
    on = od * lax.rsqrt(ms + NORM_EPS) * ng_ref[...] * (1.0 - lam_init)
    o_ref[0] = on.T[:tq].astype(BF16)


def _diff_attn(q, k, vt, w, n_keys, q_pos0, tq, tk):
    b, t, _ = q.shape
    tkp = k.shape[1]
    nblk = tkp // tk
    tqp = max(tq, LANES)
    lam_init = 0.8 - 0.6 * math.exp(-0.3 * 0)
    kern = functools.partial(_diff_attn_kernel, tq=tq, tqp=tqp, tk=tk, nblk=nblk, n_keys=n_keys,
                             q_pos0=q_pos0, lam_init=lam_init)
    return pl.pallas_call(
        kern,
        grid=(b, HEADS, t // tq),
        in_specs=[
            pl.BlockSpec((1, tq, HEAD_DIM), lambda i, h, j: (i, j, h)),
            pl.BlockSpec((1, tkp, HEAD_DIM), lambda i, h, j: (i, 0, h)),
            pl.BlockSpec((nblk, HEAD_DIM, tk), lambda i, h, j: (i, h, 0)),
            pl.BlockSpec((4, DQK), lambda i, h, j: (0, 0)),
            pl.BlockSpec((HEAD_DIM, 1), lambda i, h, j: (0, 0)),
        ],
        out_specs=pl.BlockSpec((1, tq, HEAD_DIM), lambda i, h, j: (i, j, h)),
        out_shape=jax.ShapeDtypeStruct((b, t, KEY_DIM), BF16),
        scratch_shapes=[
            pltpu.VMEM((tk, 2 * tqp), F32),
            pltpu.VMEM((tk, 2 * tqp), F32),
            pltpu.VMEM((1, 2 * tqp), F32),
            pltpu.VMEM((1, 2 * tqp), F32),
            pltpu.VMEM((HEAD_DIM, 2 * tqp), F32),
        ],
        compiler_params=_params("parallel", "parallel", "arbitrary"),
        name="diff_attn",
    )(q, k, vt, w["diff_lambda"], w["diff_norm_g_col"])


def _post_mix_kernel(x_ref, og_ref, od_ref, mk_ref, mv_ref, wo_ref, g_ref, wq_ref, wmo_ref, o_ref):
    x1 = x_ref[0] + _dot(og_ref[0], wo_ref[0:KEY_DIM, :]) + _dot(od_ref[0], wo_ref[KEY_DIM:, :])
    h2 = _rms(x1, g_ref[...]).astype(BF16)
    qm = _dot(h2, wq_ref[...]).astype(BF16)
    outs = []
    for hd in range(HEADS):
        sl = slice(hd * HEAD_DIM, (hd + 1) * HEAD_DIM)
        mk = mk_ref[0, :, sl].astype(BF16)
        mv = mv_ref[0, :, sl].astype(BF16)
        s = _dot_nt(qm[:, sl], mk) * (HEAD_DIM ** -0.5)
        p = jnp.exp(s - jnp.max(s, axis=-1, keepdims=True))
        l = jnp.sum(p, axis=-1, keepdims=True)
        outs.append(_dot((p / l).astype(BF16), mv))
    om = jnp.concatenate(outs, axis=1).astype(BF16)
    o_ref[0] = x1 + _dot(om, wmo_ref[...])


def _post_mix(x, og, od, mk, mv, w, tm):
    b, t, _ = x.shape
    blk = lambda i, j: (i, j, 0)
    fix = lambda i, j: (0, 0)
    mem = lambda i, j: (i, 0, 0)
    return pl.pallas_call(
        _post_mix_kernel,
        grid=(b, t // tm),
        in_specs=[
            pl.BlockSpec((1, tm, D_MODEL), blk),
            pl.BlockSpec((1, tm, KEY_DIM), blk),
            pl.BlockSpec((1, tm, KEY_DIM), blk),
            pl.BlockSpec((1, N_MEM, KEY_DIM), mem),
            pl.BlockSpec((1, N_MEM, KEY_DIM), mem),
            pl.BlockSpec((D_MODEL, D_MODEL), fix),
            pl.BlockSpec((1, D_MODEL), fix),
            pl.BlockSpec((D_MODEL, KEY_DIM), fix),
            pl.BlockSpec((KEY_DIM, D_MODEL), fix),
        ],
        out_specs=pl.BlockSpec((1, tm, D_MODEL), blk),
        out_shape=jax.ShapeDtypeStruct((b, t, D_MODEL), F32),
        compiler_params=_params("parallel", "parallel"),
        name="post_mix",
    )(x, og, od, mk, mv, w["w_out"], w["norm_mem_g"], w["w_mq"], w["w_mo"])


def _mlp_kernel(x_ref, g_ref, wu_ref, wd_ref, fg_ref, o_ref, *, ff_blk):
    x = x_ref[...]
    hf = _rms(x, g_ref[...]).astype(BF16)
    acc = x
    for c0 in range(0, D_FF, ff_blk):
        u = jnp.maximum(_dot(hf, wu_ref[:, c0:c0 + ff_blk]), 0.0)
        acc = acc + _dot((u * u).astype(BF16), wd_ref[c0:c0 + ff_blk, :])
    o_ref[...] = _rms(acc, fg_ref[...])


def _mlp(x2d, w, tm):
    n = x2d.shape[0]
    row = lambda i: (i, 0)
    fix = lambda i: (0, 0)
    return pl.pallas_call(
        functools.partial(_mlp_kernel, ff_blk=D_MODEL),
        grid=(n // tm,),
        in_specs=[
            pl.BlockSpec((tm, D_MODEL), row),
            pl.BlockSpec((1, D_MODEL), fix),
            pl.BlockSpec((D_MODEL, D_FF), fix),
            pl.BlockSpec((D_FF, D_MODEL), fix),
            pl.BlockSpec((1, D_MODEL), fix),
        ],
        out_specs=pl.BlockSpec((tm, D_MODEL), row),
        out_shape=jax.ShapeDtypeStruct((n, D_MODEL), F32),
        compiler_params=_params("parallel"),
        name="mlp",
    )(x2d, w["norm_ffn_g"], w["w_up"], w["w_down"], w["final_norm_g"])


def _mem_kv_kernel(m_ref, g_ref, w_ref, k_ref, v_ref):
    mh = _rms(m_ref[...], g_ref[...]).astype(BF16)
    kv = _dot(mh, w_ref[...])
    k_ref[...] = kv[:, :KEY_DIM]
    v_ref[...] = kv[:, KEY_DIM:]


def _mem_kv(mem2d, w, tm):
    n = mem2d.shape[0]
    row = lambda i: (i, 0)
    fix = lambda i: (0, 0)
    return pl.pallas_call(
        _mem_kv_kernel,
        grid=(n // tm,),
        in_specs=[
            pl.BlockSpec((tm, D_MODEL), row),
            pl.BlockSpec((1, D_MODEL), fix),
            pl.BlockSpec((D_MODEL, 2 * KEY_DIM), fix),
        ],
        out_specs=(pl.BlockSpec((tm, KEY_DIM), row), pl.BlockSpec((tm, KEY_DIM), row)),
        out_shape=(jax.ShapeDtypeStruct((n, KEY_DIM), F32), jax.ShapeDtypeStruct((n, KEY_DIM), F32)),
        compiler_params=_params("parallel"),
        name="mem_kv",
    )(mem2d, w["mem_norm_g"], w["w_mkv"])


def _tile(n, pref):
    t = min(n, pref)
    assert n % t == 0, (n, t)
    return t


def _layer(x, pos0, conv_prev, s0, kv_prev, mem_k, mem_v, w):
    b, t, _ = x.shape
    n = b * t
    c = min(t, CHUNK)
    x2d = x.reshape(n, D_MODEL)
    tm = _tile(n, 512)
    tab_rows = t if t % tm == 0 else n
    pos_rows = pos0 + (jnp.arange(tab_rows) % t)
    qkv, z, gb, qd, kf, kb, vf, vb, vt = _proj_in(x2d, pos_rows, tab_rows, w, tm)
    r3 = lambda a: a.reshape(b, t, a.shape[-1])
    qkv3 = r3(qkv)
    og, s_new = _gdn(qkv3, r3(z), r3(gb), conv_prev, s0, w, nchunks=_tile(t // c, 8))
    conv_new = qkv3[:, t - (CONV_W - 1):, :]
    kb3 = r3(kb)
    if kv_prev is None and t % tm == 0:
        n_keys, tk = t, tm
    else:
        vb3 = r3(vb)
        if kv_prev is not None:
            kb3 = jnp.concatenate([kv_prev[0].astype(BF16), kb3], axis=1)
            vb3 = jnp.concatenate([kv_prev[1].astype(BF16), vb3], axis=1)
        n_keys = kb3.shape[1]
        pad = (-n_keys) % LANES
        kb3 = jnp.pad(kb3, ((0, 0), (0, pad), (0, 0)))
        vb3 = jnp.pad(vb3, ((0, 0), (0, pad), (0, 0)))
        tk = _tile(n_keys + pad, 2048)
        nblk = (n_keys + pad) // tk
        vt = vb3.reshape(b, nblk, tk, KEY_DIM).swapaxes(2, 3).reshape(b * nblk, KEY_DIM, tk)
    od = _diff_attn(r3(qd), kb3, vt, w, n_keys=n_keys, q_pos0=pos0, tq=_tile(t, 512), tk=tk)
    x2 = _post_mix(x, og, od, mem_k, mem_v, w, tm=_tile(t, 512))
    y = _mlp(x2.reshape(n, D_MODEL), w, tm).reshape(b, t, D_MODEL)
    return y, s_new, conv_new, r3(kf), r3(vf)


def kernel(x_prompt, x_sample, mem_prompt, cache_diff_k, cache_diff_v, cache_mem_k, cache_mem_v, state_gdn, state_gdn_conv, norm_mix_g, w_in, gdn_conv_w, gdn_a_log, gdn_dt_bias, gdn_norm_g, diff_lambda, diff_norm_g, w_out, norm_mem_g, mem_norm_g, w_mq, w_mkv, w_mo, norm_ffn_g, w_up, w_down, final_norm_g):
    bp, tp, _ = x_prompt.shape
    bs, ts, _ = x_sample.shape
    p_len = cache_diff_k.shape[2]
    depth = w_in.shape[0]
    assert depth == 1
    l = 0
    wi = w_in[l]
    sp = [CONV_DIM, CONV_DIM + KEY_DIM, CONV_DIM + KEY_DIM + HEADS, CONV_DIM + KEY_DIM + 2 * HEADS]
    w_ab = jnp.concatenate([wi[:, sp[1]:sp[3]], jnp.zeros((D_MODEL, LANES - 2 * HEADS), F32)], axis=1)
    lanes_pad = lambda a: jnp.concatenate([a, jnp.zeros((LANES - a.shape[0],), F32)])[None, :]
    row = lambda a: a.reshape(1, -1)
    w = {
        "norm_mix_g": row(norm_mix_g[l]),
        "w_main": jnp.concatenate([wi[:, :sp[1]], wi[:, sp[3]:]], axis=1).astype(BF16),
        "w_ab": w_ab.astype(BF16),
        "a_log": lanes_pad(gdn_a_log[l]),
        "dt_bias": lanes_pad(gdn_dt_bias[l]),
        "conv_w": gdn_conv_w[l],
        "gdn_norm_g": row(gdn_norm_g[l]),
        "diff_lambda": diff_lambda[l],
        "diff_norm_g_col": diff_norm_g[l].reshape(HEAD_DIM, 1),
        "w_out": w_out[l].astype(BF16),
        "norm_mem_g": row(norm_mem_g[l]),
        "mem_norm_g": row(mem_norm_g[l]),
        "w_mq": w_mq[l].astype(BF16),
        "w_mkv": w_mkv[l].astype(BF16),
        "w_mo": w_mo[l].astype(BF16),
        "norm_ffn_g": row(norm_ffn_g[l]),
        "w_up": w_up[l].astype(BF16),
        "w_down": w_down[l].astype(BF16),
        "final_norm_g": row(final_norm_g),
    }
    n_mem = mem_prompt.shape[1]
    mk, mv = _mem_kv(mem_prompt.reshape(bp * n_mem, D_MODEL), w, _tile(bp * n_mem, 512))
    mk = mk.reshape(bp, n_mem, KEY_DIM)
    mv = mv.reshape(bp, n_mem, KEY_DIM)

    zeros_conv = jnp.zeros((bp, CONV_W - 1, CONV_DIM), F32)
    zeros_state = jnp.zeros((bp, HEADS, HEAD_DIM, HEAD_DIM), F32)
    yp, sp_, cp, kp, vp = _layer(x_prompt, 0, zeros_conv, zeros_state, None, mk, mv, w)
    kv_prev = (cache_diff_k[l].reshape(bs, p_len, KEY_DIM), cache_diff_v[l].reshape(bs, p_len, KEY_DIM))
    ys, ss, cs, ks_, vs = _layer(x_sample, p_len, state_gdn_conv[l], state_gdn[l], kv_prev,
                                 cache_mem_k[l].reshape(bs, n_mem, KEY_DIM),
                                 cache_mem_v[l].reshape(bs, n_mem, KEY_DIM), w)
    h4 = lambda a: a.reshape(a.shape[0], a.shape[1], HEADS, HEAD_DIM)[None]
    return (yp, ys, sp_[None], cp[None], h4(kp), h4(vp), h4(mk), h4(mv),
            ss[None], cs[None], h4(ks_), h4(vs))
```

```python
import functools
import math

import jax
import jax.numpy as jnp
from jax import lax
from jax.experimental import pallas as pl
from jax.experimental.pallas import tpu as pltpu

D_MODEL = 1024
CHUNK = 64
HEADS = 4
HEAD_DIM = 128
KEY_DIM = HEADS * HEAD_DIM
CONV_DIM = 3 * KEY_DIM
CONV_W = 4
DQK = 64
ROT_DIM = 16
ROPE_THETA = 500000.0
N_MEM = 256
D_FF = 4 * D_MODEL
NORM_EPS = 1e-6
MAIN_COLS = CONV_DIM + 4 * KEY_DIM
LANES = 128
CONV_PAD = 8
VMEM_LIMIT = 56 * 1024 * 1024

F32 = jnp.float32
BF16 = jnp.bfloat16
HIGHEST = lax.Precision.HIGHEST


def _dot(a, b, precision=None):
    return jnp.dot(a, b, preferred_element_type=F32, precision=precision)


def _dot_nt(a, b):
    return lax.dot_general(a, b, (((1,), (1,)), ((), ())), preferred_element_type=F32)


def _dot_tn(a, b):
    return lax.dot_general(a, b, (((0,), (0,)), ((), ())), preferred_element_type=F32)


def _rms(x, g):
    return x * lax.rsqrt(jnp.mean(x * x, axis=-1, keepdims=True) + NORM_EPS) * g


def _sigmoid(x):
    return 1.0 / (1.0 + jnp.exp(-x))


def _params(*sem):
    return pltpu.CompilerParams(dimension_semantics=sem, vmem_limit_bytes=VMEM_LIMIT)


def _proj_in_kernel(x_ref, g_ref, wm_ref, wab_ref, alog_ref, dtb_ref, cos_ref, sna_ref, snb_ref,
                    qkv_ref, z_ref, gb_ref, q_ref, kf_ref, kb_ref, vf_ref, vb_ref, vt_ref):
    x = x_ref[...]
    h = _rms(x, g_ref[...]).astype(BF16)
    main = _dot(h, wm_ref[...])
    ab = _dot(h, wab_ref[...])
    qkv_ref[...] = main[:, :CONV_DIM]
    z_ref[...] = main[:, CONV_DIM:CONV_DIM + KEY_DIM]
    xa = ab + dtb_ref[...]
    softplus = jnp.maximum(xa, 0.0) + jnp.log1p(jnp.exp(-jnp.abs(xa)))
    lane = lax.broadcasted_iota(jnp.int32, ab.shape, 1)
    gb_ref[...] = jnp.where(lane < HEADS, -jnp.exp(alog_ref[...]) * softplus, _sigmoid(ab))
    cos, sna, snb = cos_ref[...], sna_ref[...], snb_ref[...]
    q0 = CONV_DIM + KEY_DIM
    for hd in range(HEADS):
        sl = slice(hd * HEAD_DIM, (hd + 1) * HEAD_DIM)
        qh = main[:, q0 + hd * HEAD_DIM:q0 + (hd + 1) * HEAD_DIM]
        kh = main[:, q0 + KEY_DIM + hd * HEAD_DIM:q0 + KEY_DIM + (hd + 1) * HEAD_DIM]
        vh = main[:, q0 + 2 * KEY_DIM + hd * HEAD_DIM:q0 + 2 * KEY_DIM + (hd + 1) * HEAD_DIM]
        qr = qh * cos + pltpu.roll(qh, LANES - ROT_DIM // 2, 1) * sna + pltpu.roll(qh, ROT_DIM // 2, 1) * snb
        kr = kh * cos + pltpu.roll(kh, LANES - ROT_DIM // 2, 1) * sna + pltpu.roll(kh, ROT_DIM // 2, 1) * snb
        q_ref[:, sl] = (qr * (DQK ** -0.5 * math.log2(math.e))).astype(BF16)
        kf_ref[:, sl] = kr
        kb_ref[:, sl] = kr.astype(BF16)
        vf_ref[:, sl] = vh
        vb_ref[:, sl] = vh.astype(BF16)
        vt_ref[0, sl, :] = vh.T.astype(BF16)


def _rope_tables(pos):
    half = ROT_DIM // 2
    inv = ROPE_THETA ** (-jnp.arange(0, ROT_DIM, 2, dtype=F32) / ROT_DIM)
    ang = pos.astype(F32)[:, None] * inv[None, :]
    cos, sin = jnp.cos(ang), jnp.sin(ang)
    t = pos.shape[0]
    pad = jnp.zeros((t, DQK - ROT_DIM), F32)
    cos64 = jnp.concatenate([cos, cos, pad + 1.0], axis=1)
    sna64 = jnp.concatenate([-sin, jnp.zeros_like(sin), pad], axis=1)
    snb64 = jnp.concatenate([jnp.zeros_like(sin), sin, pad], axis=1)
    return tuple(jnp.concatenate([a, a], axis=1) for a in (cos64, sna64, snb64))


def _proj_in(x2d, pos_rows, tab_rows, w, tm):
    n = x2d.shape[0]
    nt = tab_rows // tm
    cos, sna, snb = _rope_tables(pos_rows)
    row = lambda i: (i, 0)
    fix = lambda i: (0, 0)
    tab = lambda i: (i % nt, 0)
    out_shape = (
        jax.ShapeDtypeStruct((n, CONV_DIM), F32),
        jax.ShapeDtypeStruct((n, KEY_DIM), F32),
        jax.ShapeDtypeStruct((n, LANES), F32),
        jax.ShapeDtypeStruct((n, KEY_DIM), BF16),
        jax.ShapeDtypeStruct((n, KEY_DIM), F32),
        jax.ShapeDtypeStruct((n, KEY_DIM), BF16),
        jax.ShapeDtypeStruct((n, KEY_DIM), F32),
        jax.ShapeDtypeStruct((n, KEY_DIM), BF16),
        jax.ShapeDtypeStruct((n // tm, KEY_DIM, tm), BF16),
    )
    out_specs = tuple(pl.BlockSpec((tm, s.shape[1]), row) for s in out_shape[:-1])
    out_specs += (pl.BlockSpec((1, KEY_DIM, tm), lambda i: (i, 0, 0)),)
    return pl.pallas_call(
        _proj_in_kernel,
        grid=(n // tm,),
        in_specs=[
            pl.BlockSpec((tm, D_MODEL), row),
            pl.BlockSpec((1, D_MODEL), fix),
            pl.BlockSpec((D_MODEL, MAIN_COLS), fix),
            pl.BlockSpec((D_MODEL, LANES), fix),
            pl.BlockSpec((1, LANES), fix),
            pl.BlockSpec((1, LANES), fix),
            pl.BlockSpec((tm, LANES), tab),
            pl.BlockSpec((tm, LANES), tab),
            pl.BlockSpec((tm, LANES), tab),
        ],
        out_specs=out_specs,
        out_shape=out_shape,
        compiler_params=_params("parallel"),
        name="proj_in",
    )(x2d, w["norm_mix_g"], w["w_main"], w["w_ab"], w["a_log"], w["dt_bias"], cos, sna, snb)


def _gdn_kernel(qkv_ref, z_ref, gb_ref, gbt_ref, cprev_ref, s0_ref, cw_ref, ng_ref,
                o_ref, sout_ref, xp_ref, y_ref, s_ref, *, c, nchunks):
    j = pl.program_id(1)
    rows = c * nchunks
    hc = HEADS * c

    @pl.when(j == 0)
    def _():
        xp_ref[0:CONV_PAD, :] = cprev_ref[0]
        s_ref[...] = s0_ref[0]

    xp_ref[CONV_PAD:CONV_PAD + rows, :] = qkv_ref[0]
    cw = cw_ref[...]
    xp = xp_ref[...]
    y = xp[CONV_PAD:] * cw[CONV_W - 1:CONV_W, :]
    for i in range(CONV_W - 1):
        y = y + pltpu.roll(xp, CONV_W - 1 - i, 0)[CONV_PAD:] * cw[i:i + 1, :]
    xp_ref[0:CONV_PAD, :] = xp[rows:rows + CONV_PAD]
    y_ref[...] = y * _sigmoid(y)

    ri = lax.broadcasted_iota(jnp.int32, (hc, hc), 0)
    cj = lax.broadcasted_iota(jnp.int32, (hc, hc), 1)
    same = (ri // c) == (cj // c)
    tril_bd = same & (ri >= cj)
    strict_bd = same & (ri > cj)
    eye_bd = (ri == cj).astype(F32)
    r1 = lax.broadcasted_iota(jnp.int32, (c, c), 0)
    c1 = lax.broadcasted_iota(jnp.int32, (c, c), 1)
    tril_c = (r1 >= c1).astype(F32)
    triu_c = (r1 <= c1).astype(F32)
    ng = ng_ref[...]

    def prep(ci):
        rs = slice(ci * c, (ci + 1) * c)
        gcol = gb_ref[0, rs, :]
        gc_col = _dot(tril_c, gcol, HIGHEST)
        gc_row = _dot(gbt_ref[0, ci], triu_c, HIGHEST)
        ks, kbs, qs, vbs, kbgs, qgs, kds, dcol, grow, glast = [], [], [], [], [], [], [], [], [], []
        for hd in range(HEADS):
            qh = y_ref[rs, hd * HEAD_DIM:(hd + 1) * HEAD_DIM]
            kh = y_ref[rs, KEY_DIM + hd * HEAD_DIM:KEY_DIM + (hd + 1) * HEAD_DIM]
            vh = y_ref[rs, 2 * KEY_DIM + hd * HEAD_DIM:2 * KEY_DIM + (hd + 1) * HEAD_DIM]
            qh = qh * lax.rsqrt(jnp.sum(qh * qh, axis=-1, keepdims=True) + NORM_EPS) * (HEAD_DIM ** -0.5)
            kh = kh * lax.rsqrt(jnp.sum(kh * kh, axis=-1, keepdims=True) + NORM_EPS)
            g_h = gc_col[:, hd:hd + 1]
            beta_h = gcol[:, HEADS + hd:HEADS + hd + 1]
            g_last = gc_col[c - 1:c, hd:hd + 1]
            eg = jnp.exp(g_h)
            kb = kh * beta_h
            ks.append(kh)
            kbs.append(kb)
            qs.append(qh)
            vbs.append(vh * beta_h)
            kbgs.append(kb * eg)
            qgs.append((qh * eg).astype(BF16))
            kds.append((kh * jnp.exp(g_last - g_h)).astype(BF16))
            glast.append(jnp.exp(g_last))
            dcol.append(jnp.broadcast_to(g_h, (c, hc)))
            grow.append(gc_row[hd:hd + 1, :])
        k_s = jnp.concatenate(ks, axis=0).astype(BF16)
        q_s = jnp.concatenate(qs, axis=0).astype(BF16)
        dmat = jnp.concatenate(dcol, axis=0) - jnp.concatenate(grow, axis=1)
        decay = jnp.exp(jnp.where(tril_bd, dmat, -jnp.inf))
        a_kk = _dot_nt(jnp.concatenate(kbs, axis=0).astype(BF16), k_s)
        x = jnp.where(strict_bd, -(a_kk * decay), 0.0)
        rhs = jnp.concatenate([jnp.concatenate(vbs, axis=0), jnp.concatenate(kbgs, axis=0)], axis=1)
        a_qk = (_dot_nt(q_s, k_s) * decay).astype(BF16)
        return dict(x=x, rhs=rhs.astype(BF16), a_qk=a_qk, qg=qgs, kd=kds, glast=glast)

    pre = [prep(ci) for ci in range(nchunks)]
    ps = [eye_bd + d["x"] for d in pre]
    xps = [d["x"] for d in pre]
    for _ in range(int(math.log2(c)) - 1):
        xbs = [xp.astype(BF16) for xp in xps]
        xps = [_dot(xb, xb) for xb in xbs]
        ps = [p + _dot(p.astype(BF16), xp.astype(BF16)) for p, xp in zip(ps, xps)]
    uws = [_dot(p.astype(BF16), d["rhs"]) for p, d in zip(ps, pre)]

    state = [s_ref[hd] for hd in range(HEADS)]
    for ci in range(nchunks):
        rs = slice(ci * c, (ci + 1) * c)
        d, uw = pre[ci], uws[ci]
        vnews, qss = [], []
        for hd in range(HEADS):
            hs = slice(hd * c, (hd + 1) * c)
            wq = jnp.concatenate([uw[hs, HEAD_DIM:].astype(BF16), d["qg"][hd]], axis=0)
            ws = _dot(wq, state[hd].astype(BF16))
            v_new = uw[hs, :HEAD_DIM] - ws[:c]
            vnews.append(v_new)
            qss.append(ws[c:])
            state[hd] = state[hd] * d["glast"][hd] + _dot_tn(d["kd"][hd], v_new.astype(BF16))
        o_s = jnp.concatenate(qss, axis=0) + _dot(d["a_qk"], jnp.concatenate(vnews, axis=0).astype(BF16))
        for hd in range(HEADS):
            sl = slice(hd * HEAD_DIM, (hd + 1) * HEAD_DIM)
            zh = z_ref[0, rs, sl]
            o_h = _rms(o_s[hd * c:(hd + 1) * c], ng) * (zh * _sigmoid(zh))
            o_ref[0, rs, sl] = o_h.astype(BF16)
    for hd in range(HEADS):
        s_ref[hd] = state[hd]

    @pl.when(j == pl.num_programs(1) - 1)
    def _():
        sout_ref[0] = s_ref[...]


def _gdn(qkv, z, gb, conv_prev, s0, w, nchunks):
    b, t, _ = qkv.shape
    c = min(t, CHUNK)
    rows = c * nchunks
    cprev = jnp.concatenate([jnp.zeros((b, CONV_PAD - (CONV_W - 1), CONV_DIM), F32), conv_prev], axis=1)
    gbt = gb[:, :, :2 * HEADS].reshape(b, t // c, c, 2 * HEADS).swapaxes(2, 3)
    blk = lambda i, j: (i, j, 0)
    kern = functools.partial(_gdn_kernel, c=c, nchunks=nchunks)
    return pl.pallas_call(
        kern,
        grid=(b, t // rows),
        in_specs=[
            pl.BlockSpec((1, rows, CONV_DIM), blk),
            pl.BlockSpec((1, rows, KEY_DIM), blk),
            pl.BlockSpec((1, rows, LANES), blk),
            pl.BlockSpec((1, nchunks, 2 * HEADS, c), lambda i, j: (i, j, 0, 0)),
            pl.BlockSpec((1, CONV_PAD, CONV_DIM), lambda i, j: (i, 0, 0)),
            pl.BlockSpec((1, HEADS, HEAD_DIM, HEAD_DIM), lambda i, j: (i, 0, 0, 0)),
            pl.BlockSpec((CONV_W, CONV_DIM), lambda i, j: (0, 0)),
            pl.BlockSpec((1, HEAD_DIM), lambda i, j: (0, 0)),
        ],
        out_specs=(
            pl.BlockSpec((1, rows, KEY_DIM), blk),
            pl.BlockSpec((1, HEADS, HEAD_DIM, HEAD_DIM), lambda i, j: (i, 0, 0, 0)),
        ),
        out_shape=(
            jax.ShapeDtypeStruct((b, t, KEY_DIM), BF16),
            jax.ShapeDtypeStruct((b, HEADS, HEAD_DIM, HEAD_DIM), F32),
        ),
        scratch_shapes=[
            pltpu.VMEM((rows + CONV_PAD, CONV_DIM), F32),
            pltpu.VMEM((rows, CONV_DIM), F32),
            pltpu.VMEM((HEADS, HEAD_DIM, HEAD_DIM), F32),
        ],
        compiler_params=_params("parallel", "arbitrary"),
        name="gdn",
    )(qkv, z, gb, gbt, cprev, s0, w["conv_w"], w["gdn_norm_g"])


def _diff_attn_kernel(q_ref, k_ref, vt_ref, lam_ref, ng_ref, o_ref, s0_ref, s1_ref, m_ref, l_ref, acc_ref,
                      *, tq, tqp, tk, nblk, n_keys, q_pos0, lam_init):
    qi = pl.program_id(2)
    q = q_ref[0]
    if tqp > tq:
        q = jnp.concatenate([q, jnp.zeros((tqp - tq, HEAD_DIM), q.dtype)], axis=0)
    lane = lax.broadcasted_iota(jnp.int32, q.shape, 1)
    zero = jnp.zeros_like(q)
    qs = jnp.concatenate([jnp.where(lane < DQK, q, zero), jnp.where(lane >= DQK, q, zero)], axis=0)
    pos_lo = q_pos0 + qi * tq
    k_lo = jnp.minimum((pos_lo // CHUNK + 1) * CHUNK, n_keys)
    k_hi = jnp.minimum(((pos_lo + tq - 1) // CHUNK + 1) * CHUNK, n_keys)
    n_full = k_lo // tk
    n_all = (k_hi + tk - 1) // tk

    def scores(jb, s_ref):
        jl = jnp.minimum(jb, nblk - 1)
        ks = pl.ds(pl.multiple_of(jl * tk, tk), tk)
        s_ref[...] = _dot_nt(k_ref[0, ks, :], qs)

    def mask(jb, s_ref):
        @pl.when(jb >= n_full)
        def _():
            col = lax.broadcasted_iota(jnp.int32, s_ref.shape, 1)
            qpos = pos_lo + jnp.where(col >= tqp, col - tqp, col)
            kpos = jb * tk + lax.broadcasted_iota(jnp.int32, s_ref.shape, 0)
            vis = kpos < jnp.minimum((qpos // CHUNK + 1) * CHUNK, n_keys)
            s_ref[...] = jnp.where(vis, s_ref[...], -jnp.inf)

    def softmax_pv(jb, s_ref):
        st = s_ref[...]
        m_old = m_ref[...]
        m_new = jnp.maximum(m_old, jnp.max(st, axis=0, keepdims=True))
        alpha = jnp.exp2(m_old - m_new)
        p = jnp.exp2(st - m_new)
        l_ref[...] = alpha * l_ref[...] + jnp.sum(p, axis=0, keepdims=True)
        acc_ref[...] = alpha * acc_ref[...] + _dot(vt_ref[jnp.minimum(jb, nblk - 1)], p.astype(BF16))
        m_ref[...] = m_new

    m_ref[...] = jnp.full(m_ref.shape, -jnp.inf, F32)
    l_ref[...] = jnp.zeros(l_ref.shape, F32)
    acc_ref[...] = jnp.zeros(acc_ref.shape, F32)
    scores(0, s0_ref)

    def full_pair(i, carry):
        j = 2 * i
        scores(j + 1, s1_ref)
        softmax_pv(j, s0_ref)
        scores(j + 2, s0_ref)
        softmax_pv(j + 1, s1_ref)
        return carry

    def edge_pair(i, carry):
        j = 2 * i
        mask(j, s0_ref)
        scores(j + 1, s1_ref)
        softmax_pv(j, s0_ref)

        @pl.when(j + 1 < n_all)
        def _():
            mask(j + 1, s1_ref)
            scores(j + 2, s0_ref)
            softmax_pv(j + 1, s1_ref)
        return carry

    lax.fori_loop(0, n_full // 2, full_pair, 0)
    lax.fori_loop(n_full // 2, (n_all + 1) // 2, edge_pair, 0)
    o = acc_ref[...] * (1.0 / l_ref[...])
    lf = lam_ref[...]
    lam = (jnp.exp(jnp.sum(lf[0:1] * lf[1:2], axis=-1, keepdims=True))
           - jnp.exp(jnp.sum(lf[2:3] * lf[3:4], axis=-1, keepdims=True)) + lam_init)
    od = o[:, :tqp] - lam * o[:, tqp:]
    ms = jnp.mean(od * od, axis=0, keepdims=True)
    on = od * lax.rsqrt(ms + NORM_EPS) * ng_ref[...] * (1.0 - lam_init)
    o_ref[0] = on.T[:tq].astype(BF16)


def _diff_attn(q, k, vt, w, n_keys, q_pos0, tq, tk):
    b, t, _ = q.shape
    tkp = k.shape[1]
    nblk = tkp // tk
    tqp = max(tq, LANES)
    lam_init = 0.8 - 0.6 * math.exp(-0.3 * 0)
    kern = functools.partial(_diff_attn_kernel, tq=tq, tqp=tqp, tk=tk, nblk=nblk, n_keys=n_keys,
                             q_pos0=q_pos0, lam_init=lam_init)
    return pl.pallas_call(
        kern,
        grid=(b, HEADS, t // tq),
        in_specs=[
            pl.BlockSpec((1, tq, HEAD_DIM), lambda i, h, j: (i, j, h)),
            pl.BlockSpec((1, tkp, HEAD_DIM), lambda i, h, j: (i, 0, h)),
            pl.BlockSpec((nblk, HEAD_DIM, tk), lambda i, h, j: (i, h, 0)),
            pl.BlockSpec((4, DQK), lambda i, h, j: (0, 0)),
            pl.BlockSpec((HEAD_DIM, 1), lambda i, h, j: (0, 0)),
        ],
        out_specs=pl.BlockSpec((1, tq, HEAD_DIM), lambda i, h, j: (i, j, h)),
        out_shape=jax.ShapeDtypeStruct((b, t, KEY_DIM), BF16),
        scratch_shapes=[
            pltpu.VMEM((tk, 2 * tqp), F32),
            pltpu.VMEM((tk, 2 * tqp), F32),
            pltpu.VMEM((1, 2 * tqp), F32),
            pltpu.VMEM((1, 2 * tqp), F32),
            pltpu.VMEM((HEAD_DIM, 2 * tqp), F32),
        ],
        compiler_params=_params("parallel", "parallel", "arbitrary"),
        name="diff_attn",
    )(q, k, vt, w["diff_lambda"], w["diff_norm_g_col"])


def _post_mix_kernel(x_ref, og_ref, od_ref, mk_ref, mv_ref, wo_ref, g_ref, wq_ref, wmo_ref, o_ref):
    x1 = x_ref[0] + _dot(og_ref[0], wo_ref[0:KEY_DIM, :]) + _dot(od_ref[0], wo_ref[KEY_DIM:, :])
    h2 = _rms(x1, g_ref[...]).astype(BF16)
    qm = _dot(h2, wq_ref[...]).astype(BF16)
    outs = []
    for hd in range(HEADS):
        sl = slice(hd * HEAD_DIM, (hd + 1) * HEAD_DIM)
        mk = mk_ref[0, :, sl].astype(BF16)
        mv = mv_ref[0, :, sl].astype(BF16)
        s = _dot_nt(qm[:, sl], mk) * (HEAD_DIM ** -0.5)
        p = jnp.exp(s - jnp.max(s, axis=-1, keepdims=True))
        l = jnp.sum(p, axis=-1, keepdims=True)
        outs.append(_dot((p / l).astype(BF16), mv))
    om = jnp.concatenate(outs, axis=1).astype(BF16)
    o_ref[0] = x1 + _dot(om, wmo_ref[...])


def _post_mix(x, og, od, mk, mv, w, tm):
    b, t, _ = x.shape
    blk = lambda i, j: (i, j, 0)
    fix = lambda i, j: (0, 0)
    mem = lambda i, j: (i, 0, 0)
    return pl.pallas_call(
        _post_mix_kernel,
        grid=(b, t // tm),
        in_specs=[
            pl.BlockSpec((1, tm, D_MODEL), blk),
            pl.BlockSpec((1, tm, KEY_DIM), blk),
            pl.BlockSpec((1, tm, KEY_DIM), blk),
            pl.BlockSpec((1, N_MEM, KEY_DIM), mem),
            pl.BlockSpec((1, N_MEM, KEY_DIM), mem),
            pl.BlockSpec((D_MODEL, D_MODEL), fix),
            pl.BlockSpec((1, D_MODEL), fix),
            pl.BlockSpec((D_MODEL, KEY_DIM), fix),
            pl.BlockSpec((KEY_DIM, D_MODEL), fix),
        ],
        out_specs=pl.BlockSpec((1, tm, D_MODEL), blk),
        out_shape=jax.ShapeDtypeStruct((b, t, D_MODEL), F32),
        compiler_params=_params("parallel", "parallel"),
        name="post_mix",
    )(x, og, od, mk, mv, w["w_out"], w["norm_mem_g"], w["w_mq"], w["w_mo"])


def _mlp_kernel(x_ref, g_ref, wu_ref, wd_ref, fg_ref, o_ref, *, ff_blk):
    x = x_ref[...]
    hf = _rms(x, g_ref[...]).astype(BF16)
    acc = x
    for c0 in range(0, D_FF, ff_blk):
        u = jnp.maximum(_dot(hf, wu_ref[:, c0:c0 + ff_blk]), 0.0)
        acc = acc + _dot((u * u).astype(BF16), wd_ref[c0:c0 + ff_blk, :])
    o_ref[...] = _rms(acc, fg_ref[...])


def _mlp(x2d, w, tm):
    n = x2d.shape[0]
    row = lambda i: (i, 0)
    fix = lambda i: (0, 0)
    return pl.pallas_call(
        functools.partial(_mlp_kernel, ff_blk=D_MODEL),
        grid=(n // tm,),
        in_specs=[
            pl.BlockSpec((tm, D_MODEL), row),
            pl.BlockSpec((1, D_MODEL), fix),
            pl.BlockSpec((D_MODEL, D_FF), fix),
            pl.BlockSpec((D_FF, D_MODEL), fix),
            pl.BlockSpec((1, D_MODEL), fix),
        ],
        out_specs=pl.BlockSpec((tm, D_MODEL), row),
        out_shape=jax.ShapeDtypeStruct((n, D_MODEL), F32),
        compiler_params=_params("parallel"),
        name="mlp",
    )(x2d, w["norm_ffn_g"], w["w_up"], w["w_down"], w["final_norm_g"])


def _mem_kv_kernel(m_ref, g_ref, w_ref, k_ref, v_ref):
    mh = _rms(m_ref[...], g_ref[...]).astype(BF16)
    kv = _dot(mh, w_ref[...])
    k_ref[...] = kv[:, :KEY_DIM]
    v_ref[...] = kv[:, KEY_DIM:]


def _mem_kv(mem2d, w, tm):
    n = mem2d.shape[0]
    row = lambda i: (i, 0)
    fix = lambda i: (0, 0)
    return pl.pallas_call(
        _mem_kv_kernel,
        grid=(n // tm,),
        in_specs=[
            pl.BlockSpec((tm, D_MODEL), row),
            pl.BlockSpec((1, D_MODEL), fix),
            pl.BlockSpec((D_MODEL, 2 * KEY_DIM), fix),
        ],
        out_specs=(pl.BlockSpec((tm, KEY_DIM), row), pl.BlockSpec((tm, KEY_DIM), row)),
        out_shape=(jax.ShapeDtypeStruct((n, KEY_DIM), F32), jax.ShapeDtypeStruct((n, KEY_DIM), F32)),
        compiler_params=_params("parallel"),
        name="mem_kv",
    )(mem2d, w["mem_norm_g"], w["w_mkv"])


def _tile(n, pref):
    t = min(n, pref)
    assert n % t == 0, (n, t)
    return t


def _layer(x, pos0, conv_prev, s0, kv_prev, mem_k, mem_v, w):
    b, t, _ = x.shape
    n = b * t
    c = min(t, CHUNK)
    x2d = x.reshape(n, D_MODEL)
    tm = _tile(n, 512)
    tab_rows = t if t % tm == 0 else n
    pos_rows = pos0 + (jnp.arange(tab_rows) % t)
    qkv, z, gb, qd, kf, kb, vf, vb, vt = _proj_in(x2d, pos_rows, tab_rows, w, tm)
    r3 = lambda a: a.reshape(b, t, a.shape[-1])
    qkv3 = r3(qkv)
    og, s_new = _gdn(qkv3, r3(z), r3(gb), conv_prev, s0, w, nchunks=_tile(t // c, 4))
    conv_new = qkv3[:, t - (CONV_W - 1):, :]
    kb3 = r3(kb)
    if kv_prev is None and t % tm == 0:
        n_keys, tk = t, tm
    else:
        vb3 = r3(vb)
        if kv_prev is not None:
            kb3 = jnp.concatenate([kv_prev[0].astype(BF16), kb3], axis=1)
            vb3 = jnp.concatenate([kv_prev[1].astype(BF16), vb3], axis=1)
        n_keys = kb3.shape[1]
        pad = (-n_keys) % LANES
        kb3 = jnp.pad(kb3, ((0, 0), (0, pad), (0, 0)))
        vb3 = jnp.pad(vb3, ((0, 0), (0, pad), (0, 0)))
        tk = _tile(n_keys + pad, 2048)
        nblk = (n_keys + pad) // tk
        vt = vb3.reshape(b, nblk, tk, KEY_DIM).swapaxes(2, 3).reshape(b * nblk, KEY_DIM, tk)
    od = _diff_attn(r3(qd), kb3, vt, w, n_keys=n_keys, q_pos0=pos0, tq=_tile(t, 512), tk=tk)
    x2 = _post_mix(x, og, od, mem_k, mem_v, w, tm=_tile(t, 512))
    y = _mlp(x2.reshape(n, D_MODEL), w, tm).reshape(b, t, D_MODEL)
    return y, s_new, conv_new, r3(kf), r3(vf)


def kernel(x_prompt, x_sample, mem_prompt, cache_diff_k, cache_diff_v, cache_mem_k, cache_mem_v, state_gdn, state_gdn_conv, norm_mix_g, w_in, gdn_conv_w, gdn_a_log, gdn_dt_bias, gdn_norm_g, diff_lambda, diff_norm_g, w_out, norm_mem_g, mem_norm_g, w_mq, w_mkv, w_mo, norm_ffn_g, w_up, w_down, final_norm_g):
    bp, tp, _ = x_prompt.shape
    bs, ts, _ = x_sample.shape
    p_len = cache_diff_k.shape[2]
    depth = w_in.shape[0]
    assert depth == 1
    l = 0
    wi = w_in[l]
    sp = [CONV_DIM, CONV_DIM + KEY_DIM, CONV_DIM + KEY_DIM + HEADS, CONV_DIM + KEY_DIM + 2 * HEADS]
    w_ab = jnp.concatenate([wi[:, sp[1]:sp[3]], jnp.zeros((D_MODEL, LANES - 2 * HEADS), F32)], axis=1)
    lanes_pad = lambda a: jnp.concatenate([a, jnp.zeros((LANES - a.shape[0],), F32)])[None, :]
    row = lambda a: a.reshape(1, -1)
    w = {
        "norm_mix_g": row(norm_mix_g[l]),
        "w_main": jnp.concatenate([wi[:, :sp[1]], wi[:, sp[3]:]], axis=1).astype(BF16),
        "w_ab": w_ab.astype(BF16),
        "a_log": lanes_pad(gdn_a_log[l]),
        "dt_bias": lanes_pad(gdn_dt_bias[l]),
        "conv_w": gdn_conv_w[l],
        "gdn_norm_g": row(gdn_norm_g[l]),
        "diff_lambda": diff_lambda[l],
        "diff_norm_g_col": diff_norm_g[l].reshape(HEAD_DIM, 1),
        "w_out": w_out[l].astype(BF16),
        "norm_mem_g": row(norm_mem_g[l]),
        "mem_norm_g": row(mem_norm_g[l]),
        "w_mq": w_mq[l].astype(BF16),
        "w_mkv": w_mkv[l].astype(BF16),
        "w_mo": w_mo[l].astype(BF16),
        "norm_ffn_g": row(norm_ffn_g[l]),
        "w_up": w_up[l].astype(BF16),
        "w_down": w_down[l].astype(BF16),
        "final_norm_g": row(final_norm_g),
    }
    n_mem = mem_prompt.shape[1]
    mk, mv = _mem_kv(mem_prompt.reshape(bp * n_mem, D_MODEL), w, _tile(bp * n_mem, 512))
    mk = mk.reshape(bp, n_mem, KEY_DIM)
    mv = mv.reshape(bp, n_mem, KEY_DIM)

    zeros_conv = jnp.zeros((bp, CONV_W - 1, CONV_DIM), F32)
    zeros_state = jnp.zeros((bp, HEADS, HEAD_DIM, HEAD_DIM), F32)
    yp, sp_, cp, kp, vp = _layer(x_prompt, 0, zeros_conv, zeros_state, None, mk, mv, w)
    kv_prev = (cache_diff_k[l].reshape(bs, p_len, KEY_DIM), cache_diff_v[l].reshape(bs, p_len, KEY_DIM))
    ys, ss, cs, ks_, vs = _layer(x_sample, p_len, state_gdn_conv[l], state_gdn[l], kv_prev,
                                 cache_mem_k[l].reshape(bs, n_mem, KEY_DIM),
                                 cache_mem_v[l].reshape(bs, n_mem, KEY_DIM), w)
    h4 = lambda a: a.reshape(a.shape[0], a.shape[1], HEADS, HEAD_DIM)[None]
    return (yp, ys, sp_[None], cp[None], h4(kp), h4(vp), h4(mk), h4(mv),
            ss[None], cs[None], h4(ks_), h4(vs))
```

```python
import functools
import math

import jax
import jax.numpy as jnp
from jax import lax
from jax.experimental import pallas as pl
from jax.experimental.pallas import tpu as pltpu

D_MODEL = 1024
CHUNK = 64
HEADS = 4
HEAD_DIM = 128
KEY_DIM = HEADS * HEAD_DIM
CONV_DIM = 3 * KEY_DIM
CONV_W = 4
DQK = 64
ROT_DIM = 16
ROPE_THETA = 500000.0
N_MEM = 256
D_FF = 4 * D_MODEL
NORM_EPS = 1e-6
MAIN_COLS = CONV_DIM + 4 * KEY_DIM
LANES = 128
CONV_PAD = 8
SUM_ROWS = 16
VMEM_LIMIT = 56 * 1024 * 1024

F32 = jnp.float32
BF16 = jnp.bfloat16
HIGHEST = lax.Precision.HIGHEST


def _dot(a, b, precision=None):
    return jnp.dot(a, b, preferred_element_type=F32, precision=precision)


def _dot_nt(a, b):
    return lax.dot_general(a, b, (((1,), (1,)), ((), ())), preferred_element_type=F32)


def _dot_tn(a, b):
    return lax.dot_general(a, b, (((0,), (0,)), ((), ())), preferred_element_type=F32)


def _rms(x, g):
    return x * lax.rsqrt(jnp.mean(x * x, axis=-1, keepdims=True) + NORM_EPS) * g


def _sigmoid(x):
    return 1.0 / (1.0 + jnp.exp(-x))


def _params(*sem):
    return pltpu.CompilerParams(dimension_semantics=sem, vmem_limit_bytes=VMEM_LIMIT)


def _proj_in_kernel(x_ref, g_ref, wm_ref, wab_ref, alog_ref, dtb_ref, cos_ref, sna_ref, snb_ref,
                    qkv_ref, z_ref, gb_ref, q_ref, kf_ref, kb_ref, vf_ref, vb_ref, vt_ref):
    x = x_ref[...]
    h = _rms(x, g_ref[...]).astype(BF16)
    main = _dot(h, wm_ref[...])
    ab = _dot(h, wab_ref[...])
    qkv_ref[...] = main[:, :CONV_DIM]
    z_ref[...] = main[:, CONV_DIM:CONV_DIM + KEY_DIM]
    xa = ab + dtb_ref[...]
    softplus = jnp.maximum(xa, 0.0) + jnp.log1p(jnp.exp(-jnp.abs(xa)))
    lane = lax.broadcasted_iota(jnp.int32, ab.shape, 1)
    gb_ref[...] = jnp.where(lane < HEADS, -jnp.exp(alog_ref[...]) * softplus, _sigmoid(ab))
    cos, sna, snb = cos_ref[...], sna_ref[...], snb_ref[...]
    q0 = CONV_DIM + KEY_DIM
    for hd in range(HEADS):
        sl = slice(hd * HEAD_DIM, (hd + 1) * HEAD_DIM)
        qh = main[:, q0 + hd * HEAD_DIM:q0 + (hd + 1) * HEAD_DIM]
        kh = main[:, q0 + KEY_DIM + hd * HEAD_DIM:q0 + KEY_DIM + (hd + 1) * HEAD_DIM]
        vh = main[:, q0 + 2 * KEY_DIM + hd * HEAD_DIM:q0 + 2 * KEY_DIM + (hd + 1) * HEAD_DIM]
        qr = qh * cos + pltpu.roll(qh, LANES - ROT_DIM // 2, 1) * sna + pltpu.roll(qh, ROT_DIM // 2, 1) * snb
        kr = kh * cos + pltpu.roll(kh, LANES - ROT_DIM // 2, 1) * sna + pltpu.roll(kh, ROT_DIM // 2, 1) * snb
        q_ref[:, sl] = (qr * (DQK ** -0.5 * math.log2(math.e))).astype(BF16)
        kf_ref[:, hd, :] = kr
        kb_ref[:, sl] = kr.astype(BF16)
        vf_ref[:, hd, :] = vh
        vb_ref[:, sl] = vh.astype(BF16)
        vt_ref[0, sl, :] = vh.T.astype(BF16)


def _rope_tables(pos):
    half = ROT_DIM // 2
    inv = ROPE_THETA ** (-jnp.arange(0, ROT_DIM, 2, dtype=F32) / ROT_DIM)
    ang = pos.astype(F32)[:, None] * inv[None, :]
    cos, sin = jnp.cos(ang), jnp.sin(ang)
    t = pos.shape[0]
    pad = jnp.zeros((t, DQK - ROT_DIM), F32)
    cos64 = jnp.concatenate([cos, cos, pad + 1.0], axis=1)
    sna64 = jnp.concatenate([-sin, jnp.zeros_like(sin), pad], axis=1)
    snb64 = jnp.concatenate([jnp.zeros_like(sin), sin, pad], axis=1)
    return tuple(jnp.concatenate([a, a], axis=1) for a in (cos64, sna64, snb64))


def _proj_in(x2d, pos_rows, tab_rows, w, tm):
    n = x2d.shape[0]
    nt = tab_rows // tm
    cos, sna, snb = _rope_tables(pos_rows)
    row = lambda i: (i, 0)
    fix = lambda i: (0, 0)
    tab = lambda i: (i % nt, 0)
    out_shape = (
        jax.ShapeDtypeStruct((n, CONV_DIM), F32),
        jax.ShapeDtypeStruct((n, KEY_DIM), F32),
        jax.ShapeDtypeStruct((n, LANES), F32),
        jax.ShapeDtypeStruct((n, KEY_DIM), BF16),
        jax.ShapeDtypeStruct((n, HEADS, HEAD_DIM), F32),
        jax.ShapeDtypeStruct((n, KEY_DIM), BF16),
        jax.ShapeDtypeStruct((n, HEADS, HEAD_DIM), F32),
        jax.ShapeDtypeStruct((n, KEY_DIM), BF16),
        jax.ShapeDtypeStruct((n // tm, KEY_DIM, tm), BF16),
    )

    def spec(sd):
        if len(sd.shape) == 2:
            return pl.BlockSpec((tm, sd.shape[1]), row)
        return pl.BlockSpec((sd.shape[0] * tm // n,) + sd.shape[1:], lambda i: (i, 0, 0))
    out_specs = tuple(spec(sd) for sd in out_shape)
    return pl.pallas_call(
        _proj_in_kernel,
        grid=(n // tm,),
        in_specs=[
            pl.BlockSpec((tm, D_MODEL), row),
            pl.BlockSpec((1, D_MODEL), fix),
            pl.BlockSpec((D_MODEL, MAIN_COLS), fix),
            pl.BlockSpec((D_MODEL, LANES), fix),
            pl.BlockSpec((1, LANES), fix),
            pl.BlockSpec((1, LANES), fix),
            pl.BlockSpec((tm, LANES), tab),
            pl.BlockSpec((tm, LANES), tab),
            pl.BlockSpec((tm, LANES), tab),
        ],
        out_specs=out_specs,
        out_shape=out_shape,
        compiler_params=_params("parallel"),
        name="proj_in",
    )(x2d, w["norm_mix_g"], w["w_main"], w["w_ab"], w["a_log"], w["dt_bias"], cos, sna, snb)


def _gdn_kernel(qkv_ref, z_ref, gb_ref, gbt_ref, cprev_ref, s0_ref, cw_ref, ng_ref,
                o_ref, sout_ref, xp_ref, y_ref, s_ref, *, c, nchunks):
    j = pl.program_id(1)
    rows = c * nchunks
    hc = HEADS * c

    @pl.when(j == 0)
    def _():
        xp_ref[0:CONV_PAD, :] = cprev_ref[0]
        s_ref[...] = s0_ref[0]

    xp_ref[CONV_PAD:CONV_PAD + rows, :] = qkv_ref[0]
    cw = cw_ref[...]
    xp = xp_ref[...]
    y = xp[CONV_PAD:] * cw[CONV_W - 1:CONV_W, :]
    for i in range(CONV_W - 1):
        y = y + pltpu.roll(xp, CONV_W - 1 - i, 0)[CONV_PAD:] * cw[i:i + 1, :]
    xp_ref[0:CONV_PAD, :] = xp[rows:rows + CONV_PAD]
    y_ref[...] = y * _sigmoid(y)

    ri = lax.broadcasted_iota(jnp.int32, (hc, hc), 0)
    cj = lax.broadcasted_iota(jnp.int32, (hc, hc), 1)
    same = (ri // c) == (cj // c)
    tril_bd = same & (ri >= cj)
    strict_bd = same & (ri > cj)
    eye_bd = (ri == cj).astype(F32)
    r1 = lax.broadcasted_iota(jnp.int32, (c, c), 0)
    c1 = lax.broadcasted_iota(jnp.int32, (c, c), 1)
    tril_c = (r1 >= c1).astype(F32)
    triu_c = (r1 <= c1).astype(F32)
    ng = ng_ref[...]

    def prep(ci):
        rs = slice(ci * c, (ci + 1) * c)
        gcol = gb_ref[0, rs, :]
        gc_col = _dot(tril_c, gcol, HIGHEST)
        gc_row = _dot(gbt_ref[0, ci], triu_c, HIGHEST)
        ks, kbs, qs, vbs, kbgs, qgs, kds, dcol, grow, glast = [], [], [], [], [], [], [], [], [], []
        for hd in range(HEADS):
            qh = y_ref[rs, hd * HEAD_DIM:(hd + 1) * HEAD_DIM]
            kh = y_ref[rs, KEY_DIM + hd * HEAD_DIM:KEY_DIM + (hd + 1) * HEAD_DIM]
            vh = y_ref[rs, 2 * KEY_DIM + hd * HEAD_DIM:2 * KEY_DIM + (hd + 1) * HEAD_DIM]
            qh = qh * lax.rsqrt(jnp.sum(qh * qh, axis=-1, keepdims=True) + NORM_EPS) * (HEAD_DIM ** -0.5)
            kh = kh * lax.rsqrt(jnp.sum(kh * kh, axis=-1, keepdims=True) + NORM_EPS)
            g_h = gc_col[:, hd:hd + 1]
            beta_h = gcol[:, HEADS + hd:HEADS + hd + 1]
            g_last = gc_col[c - 1:c, hd:hd + 1]
            eg = jnp.exp(g_h)
            kb = kh * beta_h
            ks.append(kh)
            kbs.append(kb)
            qs.append(qh)
            vbs.append(vh * beta_h)
            kbgs.append(kb * eg)
            qgs.append((qh * eg).astype(BF16))
            kds.append((kh * jnp.exp(g_last - g_h)).astype(BF16))
            glast.append(jnp.exp(g_last))
            dcol.append(jnp.broadcast_to(g_h, (c, hc)))
            grow.append(gc_row[hd:hd + 1, :])
        k_s = jnp.concatenate(ks, axis=0).astype(BF16)
        q_s = jnp.concatenate(qs, axis=0).astype(BF16)
        dmat = jnp.concatenate(dcol, axis=0) - jnp.concatenate(grow, axis=1)
        decay = jnp.exp(jnp.where(tril_bd, dmat, -jnp.inf))
        a_kk = _dot_nt(jnp.concatenate(kbs, axis=0).astype(BF16), k_s)
        x = jnp.where(strict_bd, -(a_kk * decay), 0.0)
        rhs = jnp.concatenate([jnp.concatenate(vbs, axis=0), jnp.concatenate(kbgs, axis=0)], axis=1)
        a_qk = (_dot_nt(q_s, k_s) * decay).astype(BF16)
        return dict(x=x, rhs=rhs.astype(BF16), a_qk=a_qk, qg=qgs, kd=kds, glast=glast)

    pre = [prep(ci) for ci in range(nchunks)]
    ps = [eye_bd + d["x"] for d in pre]
    xps = [d["x"] for d in pre]
    for _ in range(int(math.log2(c)) - 1):
        xbs = [xp.astype(BF16) for xp in xps]
        xps = [_dot(xb, xb) for xb in xbs]
        ps = [p + _dot(p.astype(BF16), xp.astype(BF16)) for p, xp in zip(ps, xps)]
    uws = [_dot(p.astype(BF16), d["rhs"]) for p, d in zip(ps, pre)]

    state = [s_ref[hd] for hd in range(HEADS)]
    for ci in range(nchunks):
        rs = slice(ci * c, (ci + 1) * c)
        d, uw = pre[ci], uws[ci]
        vnews, qss = [], []
        for hd in range(HEADS):
            hs = slice(hd * c, (hd + 1) * c)
            wq = jnp.concatenate([uw[hs, HEAD_DIM:].astype(BF16), d["qg"][hd]], axis=0)
            ws = _dot(wq, state[hd].astype(BF16))
            v_new = uw[hs, :HEAD_DIM] - ws[:c]
            vnews.append(v_new)
            qss.append(ws[c:])
            state[hd] = state[hd] * d["glast"][hd] + _dot_tn(d["kd"][hd], v_new.astype(BF16))
        o_s = jnp.concatenate(qss, axis=0) + _dot(d["a_qk"], jnp.concatenate(vnews, axis=0).astype(BF16))
        for hd in range(HEADS):
            sl = slice(hd * HEAD_DIM, (hd + 1) * HEAD_DIM)
            zh = z_ref[0, rs, sl]
            o_h = _rms(o_s[hd * c:(hd + 1) * c], ng) * (zh * _sigmoid(zh))
            o_ref[0, rs, sl] = o_h.astype(BF16)
    for hd in range(HEADS):
        s_ref[hd] = state[hd]

    @pl.when(j == pl.num_programs(1) - 1)
    def _():
        sout_ref[0] = s_ref[...]


def _gdn(qkv, z, gb, conv_prev, s0, w, nchunks):
    b, t, _ = qkv.shape
    c = min(t, CHUNK)
    rows = c * nchunks
    cprev = jnp.concatenate([jnp.zeros((b, CONV_PAD - (CONV_W - 1), CONV_DIM), F32), conv_prev], axis=1)
    gbt = gb[:, :, :2 * HEADS].reshape(b, t // c, c, 2 * HEADS).swapaxes(2, 3)
    blk = lambda i, j: (i, j, 0)
    kern = functools.partial(_gdn_kernel, c=c, nchunks=nchunks)
    return pl.pallas_call(
        kern,
        grid=(b, t // rows),
        in_specs=[
            pl.BlockSpec((1, rows, CONV_DIM), blk),
            pl.BlockSpec((1, rows, KEY_DIM), blk),
            pl.BlockSpec((1, rows, LANES), blk),
            pl.BlockSpec((1, nchunks, 2 * HEADS, c), lambda i, j: (i, j, 0, 0)),
            pl.BlockSpec((1, CONV_PAD, CONV_DIM), lambda i, j: (i, 0, 0)),
            pl.BlockSpec((1, HEADS, HEAD_DIM, HEAD_DIM), lambda i, j: (i, 0, 0, 0)),
            pl.BlockSpec((CONV_W, CONV_DIM), lambda i, j: (0, 0)),
            pl.BlockSpec((1, HEAD_DIM), lambda i, j: (0, 0)),
        ],
        out_specs=(
            pl.BlockSpec((1, rows, KEY_DIM), blk),
            pl.BlockSpec((1, HEADS, HEAD_DIM, HEAD_DIM), lambda i, j: (i, 0, 0, 0)),
        ),
        out_shape=(
            jax.ShapeDtypeStruct((b, t, KEY_DIM), BF16),
            jax.ShapeDtypeStruct((b, HEADS, HEAD_DIM, HEAD_DIM), F32),
        ),
        scratch_shapes=[
            pltpu.VMEM((rows + CONV_PAD, CONV_DIM), F32),
            pltpu.VMEM((rows, CONV_DIM), F32),
            pltpu.VMEM((HEADS, HEAD_DIM, HEAD_DIM), F32),
        ],
        compiler_params=_params("parallel", "arbitrary"),
        name="gdn",
    )(qkv, z, gb, gbt, cprev, s0, w["conv_w"], w["gdn_norm_g"])


def _diff_attn_kernel(q_ref, k_ref, vt_ref, lam_ref, ng_ref, o_ref, s0_ref, s1_ref, m_ref, acc_ref,
                      *, tq, tqp, tk, nblk, n_keys, q_pos0, lam_init):
    qi = pl.program_id(2)
    q = q_ref[0]
    if tqp > tq:
        q = jnp.concatenate([q, jnp.zeros((tqp - tq, HEAD_DIM), q.dtype)], axis=0)
    lane = lax.broadcasted_iota(jnp.int32, q.shape, 1)
    zero = jnp.zeros_like(q)
    qs = jnp.concatenate([jnp.where(lane < DQK, q, zero), jnp.where(lane >= DQK, q, zero)], axis=0)
    pos_lo = q_pos0 + qi * tq
    k_lo = jnp.minimum((pos_lo // CHUNK + 1) * CHUNK, n_keys)
    k_hi = jnp.minimum(((pos_lo + tq - 1) // CHUNK + 1) * CHUNK, n_keys)
    n_full = k_lo // tk
    n_all = (k_hi + tk - 1) // tk

    def scores(jb, s_ref):
        jl = jnp.minimum(jb, nblk - 1)
        ks = pl.ds(pl.multiple_of(jl * tk, tk), tk)
        s_ref[...] = _dot_nt(k_ref[0, ks, :], qs)

    def mask(jb, s_ref):
        @pl.when(jb >= n_full)
        def _():
            col = lax.broadcasted_iota(jnp.int32, s_ref.shape, 1)
            qpos = pos_lo + jnp.where(col >= tqp, col - tqp, col)
            kpos = jb * tk + lax.broadcasted_iota(jnp.int32, s_ref.shape, 0)
            vis = kpos < jnp.minimum((qpos // CHUNK + 1) * CHUNK, n_keys)
            s_ref[...] = jnp.where(vis, s_ref[...], -jnp.inf)

    def softmax_pv(jb, s_ref):
        st = s_ref[...]
        m_old = m_ref[...]
        m_new = jnp.maximum(m_old, jnp.max(st, axis=0, keepdims=True))
        alpha = jnp.exp2(m_old - m_new)
        p = jnp.exp2(st - m_new)
        vt1 = jnp.concatenate([vt_ref[jnp.minimum(jb, nblk - 1)], jnp.ones((SUM_ROWS, tk), BF16)], axis=0)
        acc_ref[...] = alpha * acc_ref[...] + _dot(vt1, p.astype(BF16))
        m_ref[...] = m_new

    m_ref[...] = jnp.full(m_ref.shape, -jnp.inf, F32)
    acc_ref[...] = jnp.zeros(acc_ref.shape, F32)
    scores(0, s0_ref)

    def full_pair(i, carry):
        j = 2 * i
        scores(j + 1, s1_ref)
        softmax_pv(j, s0_ref)
        scores(j + 2, s0_ref)
        softmax_pv(j + 1, s1_ref)
        return carry

    def edge_pair(i, carry):
        j = 2 * i
        mask(j, s0_ref)
        scores(j + 1, s1_ref)
        softmax_pv(j, s0_ref)

        @pl.when(j + 1 < n_all)
        def _():
            mask(j + 1, s1_ref)
            scores(j + 2, s0_ref)
            softmax_pv(j + 1, s1_ref)
        return carry

    lax.fori_loop(0, n_full // 2, full_pair, 0)
    lax.fori_loop(n_full // 2, (n_all + 1) // 2, edge_pair, 0)
    o = acc_ref[0:HEAD_DIM, :] * (1.0 / acc_ref[HEAD_DIM:HEAD_DIM + 1, :])
    lf = lam_ref[...]
    lam = (jnp.exp(jnp.sum(lf[0:1] * lf[1:2], axis=-1, keepdims=True))
           - jnp.exp(jnp.sum(lf[2:3] * lf[3:4], axis=-1, keepdims=True)) + lam_init)
    od = o[:, :tqp] - lam * o[:, tqp:]
    ms = jnp.mean(od * od, axis=0, keepdims=True)
    on = od * lax.rsqrt(ms + NORM_EPS) * ng_ref[...] * (1.0 - lam_init)
    o_ref[0] = on.T[:tq].astype(BF16)


def _diff_attn(q, k, vt, w, n_keys, q_pos0, tq, tk):
    b, t, _ = q.shape
    tkp = k.shape[1]
    nblk = tkp // tk
    tqp = max(tq, LANES)
    lam_init = 0.8 - 0.6 * math.exp(-0.3 * 0)
    kern = functools.partial(_diff_attn_kernel, tq=tq, tqp=tqp, tk=tk, nblk=nblk, n_keys=n_keys,
                             q_pos0=q_pos0, lam_init=lam_init)
    return pl.pallas_call(
        kern,
        grid=(b, HEADS, t // tq),
        in_specs=[
            pl.BlockSpec((1, tq, HEAD_DIM), lambda i, h, j: (i, j, h)),
            pl.BlockSpec((1, tkp, HEAD_DIM), lambda i, h, j: (i, 0, h)),
            pl.BlockSpec((nblk, HEAD_DIM, tk), lambda i, h, j: (i, h, 0)),
            pl.BlockSpec((4, DQK), lambda i, h, j: (0, 0)),
            pl.BlockSpec((HEAD_DIM, 1), lambda i, h, j: (0, 0)),
        ],
        out_specs=pl.BlockSpec((1, tq, HEAD_DIM), lambda i, h, j: (i, j, h)),
        out_shape=jax.ShapeDtypeStruct((b, t, KEY_DIM), BF16),
        scratch_shapes=[
            pltpu.VMEM((tk, 2 * tqp), F32),
            pltpu.VMEM((tk, 2 * tqp), F32),
            pltpu.VMEM((1, 2 * tqp), F32),
            pltpu.VMEM((HEAD_DIM + SUM_ROWS, 2 * tqp), F32),
        ],
        compiler_params=_params("parallel", "parallel", "arbitrary"),
        name="diff_attn",
    )(q, k, vt, w["diff_lambda"], w["diff_norm_g_col"])


def _post_mix_kernel(x_ref, og_ref, od_ref, mk_ref, mv_ref, wo_ref, g_ref, wq_ref, wmo_ref, o_ref):
    x1 = x_ref[0] + _dot(og_ref[0], wo_ref[0:KEY_DIM, :]) + _dot(od_ref[0], wo_ref[KEY_DIM:, :])
    h2 = _rms(x1, g_ref[...]).astype(BF16)
    qm = _dot(h2, wq_ref[...]).astype(BF16)
    outs = []
    for hd in range(HEADS):
        sl = slice(hd * HEAD_DIM, (hd + 1) * HEAD_DIM)
        mk = mk_ref[0, :, sl]
        mv = mv_ref[0, :, sl]
        s = _dot_nt(qm[:, sl], mk) * (HEAD_DIM ** -0.5)
        p = jnp.exp(s - jnp.max(s, axis=-1, keepdims=True))
        l = jnp.sum(p, axis=-1, keepdims=True)
        outs.append(_dot((p / l).astype(BF16), mv))
    om = jnp.concatenate(outs, axis=1).astype(BF16)
    o_ref[0] = x1 + _dot(om, wmo_ref[...])


def _post_mix(x, og, od, mk, mv, w, tm):
    b, t, _ = x.shape
    blk = lambda i, j: (i, j, 0)
    fix = lambda i, j: (0, 0)
    mem = lambda i, j: (i, 0, 0)
    return pl.pallas_call(
        _post_mix_kernel,
        grid=(b, t // tm),
        in_specs=[
            pl.BlockSpec((1, tm, D_MODEL), blk),
            pl.BlockSpec((1, tm, KEY_DIM), blk),
            pl.BlockSpec((1, tm, KEY_DIM), blk),
            pl.BlockSpec((1, N_MEM, KEY_DIM), mem),
            pl.BlockSpec((1, N_MEM, KEY_DIM), mem),
            pl.BlockSpec((D_MODEL, D_MODEL), fix),
            pl.BlockSpec((1, D_MODEL), fix),
            pl.BlockSpec((D_MODEL, KEY_DIM), fix),
            pl.BlockSpec((KEY_DIM, D_MODEL), fix),
        ],
        out_specs=pl.BlockSpec((1, tm, D_MODEL), blk),
        out_shape=jax.ShapeDtypeStruct((b, t, D_MODEL), F32),
        compiler_params=_params("parallel", "parallel"),
        name="post_mix",
    )(x, og, od, mk, mv, w["w_out"], w["norm_mem_g"], w["w_mq"], w["w_mo"])


def _mlp_kernel(x_ref, g_ref, wu_ref, wd_ref, fg_ref, o_ref, *, ff_blk):
    x = x_ref[...]
    hf = _rms(x, g_ref[...]).astype(BF16)
    acc = x
    for c0 in range(0, D_FF, ff_blk):
        u = jnp.maximum(_dot(hf, wu_ref[:, c0:c0 + ff_blk]), 0.0)
        acc = acc + _dot((u * u).astype(BF16), wd_ref[c0:c0 + ff_blk, :])
    o_ref[...] = _rms(acc, fg_ref[...])


def _mlp(x2d, w, tm):
    n = x2d.shape[0]
    row = lambda i: (i, 0)
    fix = lambda i: (0, 0)
    return pl.pallas_call(
        functools.partial(_mlp_kernel, ff_blk=D_MODEL),
        grid=(n // tm,),
        in_specs=[
            pl.BlockSpec((tm, D_MODEL), row),
            pl.BlockSpec((1, D_MODEL), fix),
            pl.BlockSpec((D_MODEL, D_FF), fix),
            pl.BlockSpec((D_FF, D_MODEL), fix),
            pl.BlockSpec((1, D_MODEL), fix),
        ],
        out_specs=pl.BlockSpec((tm, D_MODEL), row),
        out_shape=jax.ShapeDtypeStruct((n, D_MODEL), F32),
        compiler_params=_params("parallel"),
        name="mlp",
    )(x2d, w["norm_ffn_g"], w["w_up"], w["w_down"], w["final_norm_g"])


def _mem_kv_kernel(m_ref, g_ref, w_ref, k_ref, v_ref, kb_ref, vb_ref):
    mh = _rms(m_ref[...], g_ref[...]).astype(BF16)
    kv = _dot(mh, w_ref[...])
    kb_ref[...] = kv[:, :KEY_DIM].astype(BF16)
    vb_ref[...] = kv[:, KEY_DIM:].astype(BF16)
    for hd in range(HEADS):
        k_ref[:, hd, :] = kv[:, hd * HEAD_DIM:(hd + 1) * HEAD_DIM]
        v_ref[:, hd, :] = kv[:, KEY_DIM + hd * HEAD_DIM:KEY_DIM + (hd + 1) * HEAD_DIM]


def _mem_kv(mem2d, w, tm):
    n = mem2d.shape[0]
    row = lambda i: (i, 0)
    fix = lambda i: (0, 0)
    return pl.pallas_call(
        _mem_kv_kernel,
        grid=(n // tm,),
        in_specs=[
            pl.BlockSpec((tm, D_MODEL), row),
            pl.BlockSpec((1, D_MODEL), fix),
            pl.BlockSpec((D_MODEL, 2 * KEY_DIM), fix),
        ],
        out_specs=(pl.BlockSpec((tm, HEADS, HEAD_DIM), lambda i: (i, 0, 0)),) * 2
        + (pl.BlockSpec((tm, KEY_DIM), row),) * 2,
        out_shape=(jax.ShapeDtypeStruct((n, HEADS, HEAD_DIM), F32),) * 2
        + (jax.ShapeDtypeStruct((n, KEY_DIM), BF16),) * 2,
        compiler_params=_params("parallel"),
        name="mem_kv",
    )(mem2d, w["mem_norm_g"], w["w_mkv"])


def _tile(n, pref):
    t = min(n, pref)
    assert n % t == 0, (n, t)
    return t


def _layer(x, pos0, conv_prev, s0, kv_prev, mem_k, mem_v, w):
    b, t, _ = x.shape
    n = b * t
    c = min(t, CHUNK)
    x2d = x.reshape(n, D_MODEL)
    tm = _tile(n, 512)
    tab_rows = t if t % tm == 0 else n
    pos_rows = pos0 + (jnp.arange(tab_rows) % t)
    qkv, z, gb, qd, kf, kb, vf, vb, vt = _proj_in(x2d, pos_rows, tab_rows, w, tm)
    r3 = lambda a: a.reshape(b, t, -1)
    qkv3 = r3(qkv)
    og, s_new = _gdn(qkv3, r3(z), r3(gb), conv_prev, s0, w, nchunks=_tile(t // c, 8))
    conv_new = qkv3[:, t - (CONV_W - 1):, :]
    kb3 = r3(kb)
    if kv_prev is None and t % tm == 0:
        n_keys, tk = t, tm
    else:
        vb3 = r3(vb)
        if kv_prev is not None:
            kb3 = jnp.concatenate([kv_prev[0].astype(BF16), kb3], axis=1)
            vb3 = jnp.concatenate([kv_prev[1].astype(BF16), vb3], axis=1)
        n_keys = kb3.shape[1]
        pad = (-n_keys) % LANES
        kb3 = jnp.pad(kb3, ((0, 0), (0, pad), (0, 0)))
        vb3 = jnp.pad(vb3, ((0, 0), (0, pad), (0, 0)))
        tk = _tile(n_keys + pad, 2048)
        nblk = (n_keys + pad) // tk
        vt = vb3.reshape(b, nblk, tk, KEY_DIM).swapaxes(2, 3).reshape(b * nblk, KEY_DIM, tk)
    od = _diff_attn(r3(qd), kb3, vt, w, n_keys=n_keys, q_pos0=pos0, tq=_tile(t, 512), tk=tk)
    x2 = _post_mix(x, og, od, mem_k, mem_v, w, tm=_tile(t, 512))
    y = _mlp(x2.reshape(n, D_MODEL), w, tm).reshape(b, t, D_MODEL)
    h4 = lambda a: a.reshape(b, t, HEADS, HEAD_DIM)
    return y, s_new, conv_new, h4(kf), h4(vf)


def kernel(x_prompt, x_sample, mem_prompt, cache_diff_k, cache_diff_v, cache_mem_k, cache_mem_v, state_gdn, state_gdn_conv, norm_mix_g, w_in, gdn_conv_w, gdn_a_log, gdn_dt_bias, gdn_norm_g, diff_lambda, diff_norm_g, w_out, norm_mem_g, mem_norm_g, w_mq, w_mkv, w_mo, norm_ffn_g, w_up, w_down, final_norm_g):
    bp, tp, _ = x_prompt.shape
    bs, ts, _ = x_sample.shape
    p_len = cache_diff_k.shape[2]
    depth = w_in.shape[0]
    assert depth == 1
    l = 0
    wi = w_in[l]
    sp = [CONV_DIM, CONV_DIM + KEY_DIM, CONV_DIM + KEY_DIM + HEADS, CONV_DIM + KEY_DIM + 2 * HEADS]
    w_ab = jnp.concatenate([wi[:, sp[1]:sp[3]], jnp.zeros((D_MODEL, LANES - 2 * HEADS), F32)], axis=1)
    lanes_pad = lambda a: jnp.concatenate([a, jnp.zeros((LANES - a.shape[0],), F32)])[None, :]
    row = lambda a: a.reshape(1, -1)
    w = {
        "norm_mix_g": row(norm_mix_g[l]),
        "w_main": jnp.concatenate([wi[:, :sp[1]], wi[:, sp[3]:]], axis=1).astype(BF16),
        "w_ab": w_ab.astype(BF16),
        "a_log": lanes_pad(gdn_a_log[l]),
        "dt_bias": lanes_pad(gdn_dt_bias[l]),
        "conv_w": gdn_conv_w[l],
        "gdn_norm_g": row(gdn_norm_g[l]),
        "diff_lambda": diff_lambda[l],
        "diff_norm_g_col": diff_norm_g[l].reshape(HEAD_DIM, 1),
        "w_out": w_out[l].astype(BF16),
        "norm_mem_g": row(norm_mem_g[l]),
        "mem_norm_g": row(mem_norm_g[l]),
        "w_mq": w_mq[l].astype(BF16),
        "w_mkv": w_mkv[l].astype(BF16),
        "w_mo": w_mo[l].astype(BF16),
        "norm_ffn_g": row(norm_ffn_g[l]),
        "w_up": w_up[l].astype(BF16),
        "w_down": w_down[l].astype(BF16),
        "final_norm_g": row(final_norm_g),
    }
    n_mem = mem_prompt.shape[1]
    mk, mv, mkb, mvb = _mem_kv(mem_prompt.reshape(bp * n_mem, D_MODEL), w, _tile(bp * n_mem, 512))
    mk = mk.reshape(bp, n_mem, HEADS, HEAD_DIM)
    mv = mv.reshape(bp, n_mem, HEADS, HEAD_DIM)
    mkb = mkb.reshape(bp, n_mem, KEY_DIM)
    mvb = mvb.reshape(bp, n_mem, KEY_DIM)

    zeros_conv = jnp.zeros((bp, CONV_W - 1, CONV_DIM), F32)
    zeros_state = jnp.zeros((bp, HEADS, HEAD_DIM, HEAD_DIM), F32)
    yp, sp_, cp, kp, vp = _layer(x_prompt, 0, zeros_conv, zeros_state, None, mkb, mvb, w)
    kv_prev = (cache_diff_k[l].reshape(bs, p_len, KEY_DIM), cache_diff_v[l].reshape(bs, p_len, KEY_DIM))
    ys, ss, cs, ks_, vs = _layer(x_sample, p_len, state_gdn_conv[l], state_gdn[l], kv_prev,
                                 cache_mem_k[l].reshape(bs, n_mem, KEY_DIM).astype(BF16),
                                 cache_mem_v[l].reshape(bs, n_mem, KEY_DIM).astype(BF16), w)
    return (yp, ys, sp_[None], cp[None], kp[None], vp[None], mk[None], mv[None],
            ss[None], cs[None], ks_[None], vs[None])
```

```python
import functools
import math

import jax
import jax.numpy as jnp
import numpy as np
from jax import lax
from jax.experimental import pallas as pl
from jax.experimental.pallas import tpu as pltpu

D_MODEL = 1024
CHUNK = 64
HEADS = 4
HEAD_DIM = 128
KEY_DIM = HEADS * HEAD_DIM
CONV_DIM = 3 * KEY_DIM
CONV_W = 4
DQK = 64
ROT_DIM = 16
ROPE_THETA = 500000.0
N_MEM = 256
D_FF = 4 * D_MODEL
NORM_EPS = 1e-6
MAIN_COLS = CONV_DIM + 4 * KEY_DIM
LANES = 128
CONV_PAD = 8
SUM_ROWS = 16
VMEM_LIMIT = 56 * 1024 * 1024

F32 = jnp.float32
BF16 = jnp.bfloat16
HIGHEST = lax.Precision.HIGHEST


def _dot(a, b, precision=None):
    return jnp.dot(a, b, preferred_element_type=F32, precision=precision)


def _dot_nt(a, b):
    return lax.dot_general(a, b, (((1,), (1,)), ((), ())), preferred_element_type=F32)


def _dot_tn(a, b):
    return lax.dot_general(a, b, (((0,), (0,)), ((), ())), preferred_element_type=F32)


def _rms(x, g):
    return x * lax.rsqrt(jnp.mean(x * x, axis=-1, keepdims=True) + NORM_EPS) * g


def _sigmoid(x):
    return 1.0 / (1.0 + jnp.exp(-x))


def _params(*sem):
    return pltpu.CompilerParams(dimension_semantics=sem, vmem_limit_bytes=VMEM_LIMIT)


def _proj_in_kernel(x_ref, g_ref, wm_ref, wab_ref, alog_ref, dtb_ref, cos_ref, sna_ref, snb_ref,
                    qkv_ref, z_ref, gb_ref, q_ref, kf_ref, kb_ref, vf_ref, vb_ref, vt_ref):
    x = x_ref[...]
    h = _rms(x, g_ref[...]).astype(BF16)
    main = _dot(h, wm_ref[...])
    ab = _dot(h, wab_ref[...])
    qkv_ref[...] = main[:, :CONV_DIM]
    z_ref[...] = main[:, CONV_DIM:CONV_DIM + KEY_DIM]
    xa = ab + dtb_ref[...]
    softplus = jnp.maximum(xa, 0.0) + jnp.log1p(jnp.exp(-jnp.abs(xa)))
    lane = lax.broadcasted_iota(jnp.int32, ab.shape, 1)
    gb_ref[...] = jnp.where(lane < HEADS, -jnp.exp(alog_ref[...]) * softplus, _sigmoid(ab))
    cos, sna, snb = cos_ref[...], sna_ref[...], snb_ref[...]
    q0 = CONV_DIM + KEY_DIM
    krs, vhs = [], []
    for hd in range(HEADS):
        sl = slice(hd * HEAD_DIM, (hd + 1) * HEAD_DIM)
        qh = main[:, q0 + hd * HEAD_DIM:q0 + (hd + 1) * HEAD_DIM]
        kh = main[:, q0 + KEY_DIM + hd * HEAD_DIM:q0 + KEY_DIM + (hd + 1) * HEAD_DIM]
        vh = main[:, q0 + 2 * KEY_DIM + hd * HEAD_DIM:q0 + 2 * KEY_DIM + (hd + 1) * HEAD_DIM]
        qr = qh * cos + pltpu.roll(qh, LANES - ROT_DIM // 2, 1) * sna + pltpu.roll(qh, ROT_DIM // 2, 1) * snb
        kr = kh * cos + pltpu.roll(kh, LANES - ROT_DIM // 2, 1) * sna + pltpu.roll(kh, ROT_DIM // 2, 1) * snb
        q_ref[:, sl] = (qr * (DQK ** -0.5 * math.log2(math.e))).astype(BF16)
        krs.append(kr)
        vhs.append(vh)
        kb_ref[:, sl] = kr.astype(BF16)
        vb_ref[:, sl] = vh.astype(BF16)
        vt_ref[0, sl, :] = vh.T.astype(BF16)
    kf_ref[...] = jnp.concatenate(krs, axis=1).reshape(kf_ref.shape)
    vf_ref[...] = jnp.concatenate(vhs, axis=1).reshape(vf_ref.shape)


def _rope_tables(pos):
    inv = ROPE_THETA ** (-np.arange(0, ROT_DIM, 2, dtype=np.float64) / ROT_DIM)
    ang = pos.astype(np.float64)[:, None] * inv[None, :]
    cos, sin = np.cos(ang), np.sin(ang)
    pad = np.zeros((pos.shape[0], DQK - ROT_DIM))
    cos64 = np.concatenate([cos, cos, pad + 1.0], axis=1)
    sna64 = np.concatenate([-sin, np.zeros_like(sin), pad], axis=1)
    snb64 = np.concatenate([np.zeros_like(sin), sin, pad], axis=1)
    return tuple(jnp.asarray(np.concatenate([a, a], axis=1), F32) for a in (cos64, sna64, snb64))


def _proj_in(x2d, pos_rows, tab_rows, w, tm):
    n = x2d.shape[0]
    nt = tab_rows // tm
    cos, sna, snb = _rope_tables(pos_rows)
    row = lambda i: (i, 0)
    fix = lambda i: (0, 0)
    tab = lambda i: (i % nt, 0)
    out_shape = (
        jax.ShapeDtypeStruct((n, CONV_DIM), F32),
        jax.ShapeDtypeStruct((n, KEY_DIM), F32),
        jax.ShapeDtypeStruct((n, LANES), F32),
        jax.ShapeDtypeStruct((n, KEY_DIM), BF16),
        jax.ShapeDtypeStruct((n, HEADS, HEAD_DIM), F32),
        jax.ShapeDtypeStruct((n, KEY_DIM), BF16),
        jax.ShapeDtypeStruct((n, HEADS, HEAD_DIM), F32),
        jax.ShapeDtypeStruct((n, KEY_DIM), BF16),
        jax.ShapeDtypeStruct((n // tm, KEY_DIM, tm), BF16),
    )

    def spec(sd):
        if len(sd.shape) == 2:
            return pl.BlockSpec((tm, sd.shape[1]), row)
        return pl.BlockSpec((sd.shape[0] * tm // n,) + sd.shape[1:], lambda i: (i, 0, 0))
    out_specs = tuple(spec(sd) for sd in out_shape)
    return pl.pallas_call(
        _proj_in_kernel,
        grid=(n // tm,),
        in_specs=[
            pl.BlockSpec((tm, D_MODEL), row),
            pl.BlockSpec((1, D_MODEL), fix),
            pl.BlockSpec((D_MODEL, MAIN_COLS), fix),
            pl.BlockSpec((D_MODEL, LANES), fix),
            pl.BlockSpec((1, LANES), fix),
            pl.BlockSpec((1, LANES), fix),
            pl.BlockSpec((tm, LANES), tab),
            pl.BlockSpec((tm, LANES), tab),
            pl.BlockSpec((tm, LANES), tab),
        ],
        out_specs=out_specs,
        out_shape=out_shape,
        compiler_params=_params("parallel"),
        name="proj_in",
    )(x2d, w["norm_mix_g"], w["w_main"], w["w_ab"], w["a_log"], w["dt_bias"], cos, sna, snb)


def _gdn_kernel(qkv_ref, z_ref, gb_ref, cprev_ref, s0_ref, cw_ref, ng_ref,
                o_ref, sout_ref, xp_ref, y_ref, s_ref, *, c, nchunks):
    j = pl.program_id(1)
    rows = c * nchunks
    hc = HEADS * c

    @pl.when(j == 0)
    def _():
        xp_ref[0:CONV_PAD, :] = cprev_ref[0]
        s_ref[...] = s0_ref[0]

    xp_ref[CONV_PAD:CONV_PAD + rows, :] = qkv_ref[0]
    cw = cw_ref[...]
    xp = xp_ref[...]
    y = xp[CONV_PAD:] * cw[CONV_W - 1:CONV_W, :]
    for i in range(CONV_W - 1):
        y = y + pltpu.roll(xp, CONV_W - 1 - i, 0)[CONV_PAD:] * cw[i:i + 1, :]
    xp_ref[0:CONV_PAD, :] = xp[rows:rows + CONV_PAD]
    y_ref[...] = y * _sigmoid(y)

    ri = lax.broadcasted_iota(jnp.int32, (hc, hc), 0)
    cj = lax.broadcasted_iota(jnp.int32, (hc, hc), 1)
    same = (ri // c) == (cj // c)
    tril_bd = same & (ri >= cj)
    strict_bd = same & (ri > cj)
    eye_bd = (ri == cj).astype(F32)
    r1 = lax.broadcasted_iota(jnp.int32, (c, c), 0)
    c1 = lax.broadcasted_iota(jnp.int32, (c, c), 1)
    tril_c = (r1 >= c1).astype(F32)
    triu_c = (r1 <= c1).astype(F32)
    ng = ng_ref[...]
    gb_all = gb_ref[0]
    if rows % LANES:
        gb_all = jnp.concatenate([gb_all, jnp.zeros((LANES - rows % LANES, LANES), F32)], axis=0)
    gbt_all = gb_all.T[0:2 * HEADS, :]

    def prep(ci):
        rs = slice(ci * c, (ci + 1) * c)
        gcol = gb_ref[0, rs, :]
        gc_col = _dot(tril_c, gcol, HIGHEST)
        gc_row = _dot(gbt_all[:, rs], triu_c, HIGHEST)
        ks, kbs, qs, vbs, kbgs, qgs, kds, dcol, grow, glast = [], [], [], [], [], [], [], [], [], []
        for hd in range(HEADS):
            qh = y_ref[rs, hd * HEAD_DIM:(hd + 1) * HEAD_DIM]
            kh = y_ref[rs, KEY_DIM + hd * HEAD_DIM:KEY_DIM + (hd + 1) * HEAD_DIM]
            vh = y_ref[rs, 2 * KEY_DIM + hd * HEAD_DIM:2 * KEY_DIM + (hd + 1) * HEAD_DIM]
            qh = qh * lax.rsqrt(jnp.sum(qh * qh, axis=-1, keepdims=True) + NORM_EPS) * (HEAD_DIM ** -0.5)
            kh = kh * lax.rsqrt(jnp.sum(kh * kh, axis=-1, keepdims=True) + NORM_EPS)
            g_h = gc_col[:, hd:hd + 1]
            beta_h = gcol[:, HEADS + hd:HEADS + hd + 1]
            g_last = gc_col[c - 1:c, hd:hd + 1]
            eg = jnp.exp(g_h)
            kb = kh * beta_h
            ks.append(kh)
            kbs.append(kb)
            qs.append(qh)
            vbs.append(vh * beta_h)
            kbgs.append(kb * eg)
            qgs.append((qh * eg).astype(BF16))
            kds.append((kh * jnp.exp(g_last - g_h)).astype(BF16))
            glast.append(jnp.exp(g_last))
            dcol.append(jnp.broadcast_to(g_h, (c, hc)))
            grow.append(gc_row[hd:hd + 1, :])
        k_s = jnp.concatenate(ks, axis=0).astype(BF16)
        q_s = jnp.concatenate(qs, axis=0).astype(BF16)
        dmat = jnp.concatenate(dcol, axis=0) - jnp.concatenate(grow, axis=1)
        decay = jnp.exp(jnp.where(tril_bd, dmat, -jnp.inf))
        a_kk = _dot_nt(jnp.concatenate(kbs, axis=0).astype(BF16), k_s)
        x = jnp.where(strict_bd, -(a_kk * decay), 0.0)
        rhs = jnp.concatenate([jnp.concatenate(vbs, axis=0), jnp.concatenate(kbgs, axis=0)], axis=1)
        a_qk = (_dot_nt(q_s, k_s) * decay).astype(BF16)
        return dict(x=x, rhs=rhs.astype(BF16), a_qk=a_qk, qg=qgs, kd=kds, glast=glast)

    pre = [prep(ci) for ci in range(nchunks)]
    ps = [eye_bd + d["x"] for d in pre]
    xps = [d["x"] for d in pre]
    for _ in range(int(math.log2(c)) - 1):
        xbs = [xp.astype(BF16) for xp in xps]
        xps = [_dot(xb, xb) for xb in xbs]
        ps = [p + _dot(p.astype(BF16), xp.astype(BF16)) for p, xp in zip(ps, xps)]
    uws = [_dot(p.astype(BF16), d["rhs"]) for p, d in zip(ps, pre)]

    state = [s_ref[hd] for hd in range(HEADS)]
    for ci in range(nchunks):
        rs = slice(ci * c, (ci + 1) * c)
        d, uw = pre[ci], uws[ci]
        vnews, qss = [], []
        for hd in range(HEADS):
            hs = slice(hd * c, (hd + 1) * c)
            wq = jnp.concatenate([uw[hs, HEAD_DIM:].astype(BF16), d["qg"][hd]], axis=0)
            ws = _dot(wq, state[hd].astype(BF16))
            v_new = uw[hs, :HEAD_DIM] - ws[:c]
            vnews.append(v_new)
            qss.append(ws[c:])
            state[hd] = state[hd] * d["glast"][hd] + _dot_tn(d["kd"][hd], v_new.astype(BF16))
        o_s = jnp.concatenate(qss, axis=0) + _dot(d["a_qk"], jnp.concatenate(vnews, axis=0).astype(BF16))
        for hd in range(HEADS):
            sl = slice(hd * HEAD_DIM, (hd + 1) * HEAD_DIM)
            zh = z_ref[0, rs, sl]
            o_h = _rms(o_s[hd * c:(hd + 1) * c], ng) * (zh * _sigmoid(zh))
            o_ref[0, rs, sl] = o_h.astype(BF16)
    for hd in range(HEADS):
        s_ref[hd] = state[hd]

    @pl.when(j == pl.num_programs(1) - 1)
    def _():
        sout_ref[0] = s_ref[...]


def _gdn(qkv, z, gb, conv_prev, s0, w, nchunks):
    b, t, _ = qkv.shape
    c = min(t, CHUNK)
    rows = c * nchunks
    cprev = jnp.concatenate([jnp.zeros((b, CONV_PAD - (CONV_W - 1), CONV_DIM), F32), conv_prev], axis=1)
    blk = lambda i, j: (i, j, 0)
    kern = functools.partial(_gdn_kernel, c=c, nchunks=nchunks)
    return pl.pallas_call(
        kern,
        grid=(b, t // rows),
        in_specs=[
            pl.BlockSpec((1, rows, CONV_DIM), blk),
            pl.BlockSpec((1, rows, KEY_DIM), blk),
            pl.BlockSpec((1, rows, LANES), blk),
            pl.BlockSpec((1, CONV_PAD, CONV_DIM), lambda i, j: (i, 0, 0)),
            pl.BlockSpec((1, HEADS, HEAD_DIM, HEAD_DIM), lambda i, j: (i, 0, 0, 0)),
            pl.BlockSpec((CONV_W, CONV_DIM), lambda i, j: (0, 0)),
            pl.BlockSpec((1, HEAD_DIM), lambda i, j: (0, 0)),
        ],
        out_specs=(
            pl.BlockSpec((1, rows, KEY_DIM), blk),
            pl.BlockSpec((1, HEADS, HEAD_DIM, HEAD_DIM), lambda i, j: (i, 0, 0, 0)),
        ),
        out_shape=(
            jax.ShapeDtypeStruct((b, t, KEY_DIM), BF16),
            jax.ShapeDtypeStruct((b, HEADS, HEAD_DIM, HEAD_DIM), F32),
        ),
        scratch_shapes=[
            pltpu.VMEM((rows + CONV_PAD, CONV_DIM), F32),
            pltpu.VMEM((rows, CONV_DIM), F32),
            pltpu.VMEM((HEADS, HEAD_DIM, HEAD_DIM), F32),
        ],
        compiler_params=_params("parallel", "arbitrary"),
        name="gdn",
    )(qkv, z, gb, cprev, s0, w["conv_w"], w["gdn_norm_g"])


def _diff_attn_kernel(q_ref, k_ref, vt_ref, lam_ref, ng_ref, o_ref, s0_ref, s1_ref, m_ref, acc_ref,
                      *, tq, tqp, tk, nblk, n_keys, q_pos0, lam_init):
    qi = pl.program_id(2)
    width = 2 * tqp
    q = q_ref[0]
    if tqp > tq:
        q = jnp.concatenate([q, jnp.zeros((tqp - tq, HEAD_DIM), q.dtype)], axis=0)
    lane = lax.broadcasted_iota(jnp.int32, q.shape, 1)
    zero = jnp.zeros_like(q)
    qs = jnp.concatenate([jnp.where(lane < DQK, q, zero), jnp.where(lane >= DQK, q, zero)], axis=0)
    pos_lo = q_pos0 + qi * tq
    k_lo = jnp.minimum((pos_lo // CHUNK + 1) * CHUNK, n_keys)
    k_hi = jnp.minimum(((pos_lo + tq - 1) // CHUNK + 1) * CHUNK, n_keys)
    n_full = k_lo // tk
    n_all = (k_hi + tk - 1) // tk

    def scores(jb, s_ref):
        jl = jnp.minimum(jb, nblk - 1)
        ks = pl.ds(pl.multiple_of(jl * tk, tk), tk)
        s_ref[:, 0:width] = _dot_nt(k_ref[0, ks, :], qs)

    def mask(jb, s_ref):
        @pl.when(jb >= n_full)
        def _():
            col = lax.broadcasted_iota(jnp.int32, (tk, width), 1)
            qpos = pos_lo + jnp.where(col >= tqp, col - tqp, col)
            kpos = jb * tk + lax.broadcasted_iota(jnp.int32, (tk, width), 0)
            vis = kpos < jnp.minimum((qpos // CHUNK + 1) * CHUNK, n_keys)
            s_ref[:, 0:width] = jnp.where(vis, s_ref[:, 0:width], -jnp.inf)

    def softmax_pv(jb, s_ref):
        st = s_ref[:, 0:width]
        m_old = m_ref[...]
        m_new = jnp.maximum(m_old, jnp.max(st, axis=0, keepdims=True))
        alpha = jnp.exp2(m_old - m_new)
        p = jnp.exp2(st - m_new)
        vt1 = jnp.concatenate([vt_ref[jnp.minimum(jb, nblk - 1)], jnp.ones((SUM_ROWS, tk), BF16)], axis=0)
        acc_ref[...] = alpha * acc_ref[...] + _dot(vt1, p.astype(BF16))
        m_ref[...] = m_new

    m_ref[...] = jnp.full(m_ref.shape, -jnp.inf, F32)
    acc_ref[...] = jnp.zeros(acc_ref.shape, F32)
    scores(0, s0_ref)

    def full_pair(i, carry):
        j = 2 * i
        scores(j + 1, s1_ref)
        softmax_pv(j, s0_ref)
        scores(j + 2, s0_ref)
        softmax_pv(j + 1, s1_ref)
        return carry

    def edge_pair(i, carry):
        j = 2 * i
        mask(j, s0_ref)
        scores(j + 1, s1_ref)
        softmax_pv(j, s0_ref)

        @pl.when(j + 1 < n_all)
        def _():
            mask(j + 1, s1_ref)
            scores(j + 2, s0_ref)
            softmax_pv(j + 1, s1_ref)
        return carry

    lax.fori_loop(0, n_full // 2, full_pair, 0)
    lax.fori_loop(n_full // 2, (n_all + 1) // 2, edge_pair, 0)
    o = acc_ref[0:HEAD_DIM, :] * (1.0 / acc_ref[HEAD_DIM:HEAD_DIM + 1, :])
    lf = lam_ref[...]
    lam = (jnp.exp(jnp.sum(lf[0:1] * lf[1:2], axis=-1, keepdims=True))
           - jnp.exp(jnp.sum(lf[2:3] * lf[3:4], axis=-1, keepdims=True)) + lam_init)
    od = o[:, :tqp] - lam * o[:, tqp:]
    ms = jnp.mean(od * od, axis=0, keepdims=True)
    on = od * lax.rsqrt(ms + NORM_EPS) * ng_ref[...] * (1.0 - lam_init)
    o_ref[0] = on.T[:tq].astype(BF16)


def _diff_attn(q, k, vt, w, n_keys, q_pos0, tq, tk):
    b, t, _ = q.shape
    tkp = k.shape[1]
    nblk = tkp // tk
    tqp = max(tq, LANES)
    lam_init = 0.8 - 0.6 * math.exp(-0.3 * 0)
    kern = functools.partial(_diff_attn_kernel, tq=tq, tqp=tqp, tk=tk, nblk=nblk, n_keys=n_keys,
                             q_pos0=q_pos0, lam_init=lam_init)
    return pl.pallas_call(
        kern,
        grid=(b, HEADS, t // tq),
        in_specs=[
            pl.BlockSpec((1, tq, HEAD_DIM), lambda i, h, j: (i, j, h)),
            pl.BlockSpec((1, tkp, HEAD_DIM), lambda i, h, j: (i, 0, h)),
            pl.BlockSpec((nblk, HEAD_DIM, tk), lambda i, h, j: (i, h, 0)),
            pl.BlockSpec((4, DQK), lambda i, h, j: (0, 0)),
            pl.BlockSpec((HEAD_DIM, 1), lambda i, h, j: (0, 0)),
        ],
        out_specs=pl.BlockSpec((1, tq, HEAD_DIM), lambda i, h, j: (i, j, h)),
        out_shape=jax.ShapeDtypeStruct((b, t, KEY_DIM), BF16),
        scratch_shapes=[
            pltpu.VMEM((tk, 2 * tqp + LANES), F32),
            pltpu.VMEM((tk, 2 * tqp + LANES), F32),
            pltpu.VMEM((1, 2 * tqp), F32),
            pltpu.VMEM((HEAD_DIM + SUM_ROWS, 2 * tqp), F32),
        ],
        compiler_params=_params("parallel", "parallel", "arbitrary"),
        name="diff_attn",
    )(q, k, vt, w["diff_lambda"], w["diff_norm_g_col"])


def _post_mix_kernel(x_ref, og_ref, od_ref, mk_ref, mv_ref, wo_ref, g_ref, wq_ref, wmo_ref, o_ref):
    x1 = x_ref[0] + _dot(og_ref[0], wo_ref[0:KEY_DIM, :]) + _dot(od_ref[0], wo_ref[KEY_DIM:, :])
    h2 = _rms(x1, g_ref[...]).astype(BF16)
    qm = _dot(h2, wq_ref[...]).astype(BF16)
    outs = []
    for hd in range(HEADS):
        sl = slice(hd * HEAD_DIM, (hd + 1) * HEAD_DIM)
        mk = mk_ref[0, :, sl]
        mv = mv_ref[0, :, sl]
        s = _dot_nt(qm[:, sl], mk) * (HEAD_DIM ** -0.5)
        p = jnp.exp(s - jnp.max(s, axis=-1, keepdims=True))
        l = jnp.sum(p, axis=-1, keepdims=True)
        outs.append(_dot((p / l).astype(BF16), mv))
    om = jnp.concatenate(outs, axis=1).astype(BF16)
    o_ref[0] = x1 + _dot(om, wmo_ref[...])


def _post_mix(x, og, od, mk, mv, w, tm):
    b, t, _ = x.shape
    blk = lambda i, j: (i, j, 0)
    fix = lambda i, j: (0, 0)
    mem = lambda i, j: (i, 0, 0)
    return pl.pallas_call(
        _post_mix_kernel,
        grid=(b, t // tm),
        in_specs=[
            pl.BlockSpec((1, tm, D_MODEL), blk),
            pl.BlockSpec((1, tm, KEY_DIM), blk),
            pl.BlockSpec((1, tm, KEY_DIM), blk),
            pl.BlockSpec((1, N_MEM, KEY_DIM), mem),
            pl.BlockSpec((1, N_MEM, KEY_DIM), mem),
            pl.BlockSpec((D_MODEL, D_MODEL), fix),
            pl.BlockSpec((1, D_MODEL), fix),
            pl.BlockSpec((D_MODEL, KEY_DIM), fix),
            pl.BlockSpec((KEY_DIM, D_MODEL), fix),
        ],
        out_specs=pl.BlockSpec((1, tm, D_MODEL), blk),
        out_shape=jax.ShapeDtypeStruct((b, t, D_MODEL), F32),
        compiler_params=_params("parallel", "parallel"),
        name="post_mix",
    )(x, og, od, mk, mv, w["w_out"], w["norm_mem_g"], w["w_mq"], w["w_mo"])


def _mlp_kernel(x_ref, g_ref, wu_ref, wd_ref, fg_ref, o_ref, *, ff_blk):
    x = x_ref[...]
    hf = _rms(x, g_ref[...]).astype(BF16)
    acc = x
    for c0 in range(0, D_FF, ff_blk):
        u = jnp.maximum(_dot(hf, wu_ref[:, c0:c0 + ff_blk]), 0.0)
        acc = acc + _dot((u * u).astype(BF16), wd_ref[c0:c0 + ff_blk, :])
    o_ref[...] = _rms(acc, fg_ref[...])


def _mlp(x2d, w, tm):
    n = x2d.shape[0]
    row = lambda i: (i, 0)
    fix = lambda i: (0, 0)
    return pl.pallas_call(
        functools.partial(_mlp_kernel, ff_blk=D_MODEL),
        grid=(n // tm,),
        in_specs=[
            pl.BlockSpec((tm, D_MODEL), row),
            pl.BlockSpec((1, D_MODEL), fix),
            pl.BlockSpec((D_MODEL, D_FF), fix),
            pl.BlockSpec((D_FF, D_MODEL), fix),
            pl.BlockSpec((1, D_MODEL), fix),
        ],
        out_specs=pl.BlockSpec((tm, D_MODEL), row),
        out_shape=jax.ShapeDtypeStruct((n, D_MODEL), F32),
        compiler_params=_params("parallel"),
        name="mlp",
    )(x2d, w["norm_ffn_g"], w["w_up"], w["w_down"], w["final_norm_g"])


def _mem_kv_kernel(m_ref, g_ref, w_ref, k_ref, v_ref, kb_ref, vb_ref):
    mh = _rms(m_ref[...], g_ref[...]).astype(BF16)
    kv = _dot(mh, w_ref[...])
    kb_ref[...] = kv[:, :KEY_DIM].astype(BF16)
    vb_ref[...] = kv[:, KEY_DIM:].astype(BF16)
    k_ref[...] = kv[:, :KEY_DIM].reshape(k_ref.shape)
    v_ref[...] = kv[:, KEY_DIM:].reshape(v_ref.shape)


def _mem_kv(mem2d, w, tm):
    n = mem2d.shape[0]
    row = lambda i: (i, 0)
    fix = lambda i: (0, 0)
    return pl.pallas_call(
        _mem_kv_kernel,
        grid=(n // tm,),
        in_specs=[
            pl.BlockSpec((tm, D_MODEL), row),
            pl.BlockSpec((1, D_MODEL), fix),
            pl.BlockSpec((D_MODEL, 2 * KEY_DIM), fix),
        ],
        out_specs=(pl.BlockSpec((tm, HEADS, HEAD_DIM), lambda i: (i, 0, 0)),) * 2
        + (pl.BlockSpec((tm, KEY_DIM), row),) * 2,
        out_shape=(jax.ShapeDtypeStruct((n, HEADS, HEAD_DIM), F32),) * 2
        + (jax.ShapeDtypeStruct((n, KEY_DIM), BF16),) * 2,
        compiler_params=_params("parallel"),
        name="mem_kv",
    )(mem2d, w["mem_norm_g"], w["w_mkv"])


def _tile(n, pref):
    t = min(n, pref)
    assert n % t == 0, (n, t)
    return t


def _layer(x, pos0, conv_prev, s0, kv_prev, mem_k, mem_v, w):
    b, t, _ = x.shape
    n = b * t
    c = min(t, CHUNK)
    x2d = x.reshape(n, D_MODEL)
    tm = _tile(n, 512)
    tab_rows = t if t % tm == 0 else n
    pos_rows = pos0 + (np.arange(tab_rows) % t)
    qkv, z, gb, qd, kf, kb, vf, vb, vt = _proj_in(x2d, pos_rows, tab_rows, w, tm)
    r3 = lambda a: a.reshape(b, t, -1)
    qkv3 = r3(qkv)
    og, s_new = _gdn(qkv3, r3(z), r3(gb), conv_prev, s0, w, nchunks=_tile(t // c, 8))
    conv_new = qkv3[:, t - (CONV_W - 1):, :]
    kb3 = r3(kb)
    if kv_prev is None and t % tm == 0:
        n_keys, tk = t, tm
    else:
        vb3 = r3(vb)
        if kv_prev is not None:
            kb3 = jnp.concatenate([kv_prev[0].astype(BF16), kb3], axis=1)
            vb3 = jnp.concatenate([kv_prev[1].astype(BF16), vb3], axis=1)
        n_keys = kb3.shape[1]
        pad = (-n_keys) % LANES
        kb3 = jnp.pad(kb3, ((0, 0), (0, pad), (0, 0)))
        vb3 = jnp.pad(vb3, ((0, 0), (0, pad), (0, 0)))
        tk = _tile(n_keys + pad, 2048)
        nblk = (n_keys + pad) // tk
        vt = vb3.reshape(b, nblk, tk, KEY_DIM).swapaxes(2, 3).reshape(b * nblk, KEY_DIM, tk)
    od = _diff_attn(r3(qd), kb3, vt, w, n_keys=n_keys, q_pos0=pos0, tq=_tile(t, 512), tk=tk)
    x2 = _post_mix(x, og, od, mem_k, mem_v, w, tm=_tile(t, 512))
    y = _mlp(x2.reshape(n, D_MODEL), w, tm).reshape(b, t, D_MODEL)
    h4 = lambda a: a.reshape(b, t, HEADS, HEAD_DIM)
    return y, s_new, conv_new, h4(kf), h4(vf)


def kernel(x_prompt, x_sample, mem_prompt, cache_diff_k, cache_diff_v, cache_mem_k, cache_mem_v, state_gdn, state_gdn_conv, norm_mix_g, w_in, gdn_conv_w, gdn_a_log, gdn_dt_bias, gdn_norm_g, diff_lambda, diff_norm_g, w_out, norm_mem_g, mem_norm_g, w_mq, w_mkv, w_mo, norm_ffn_g, w_up, w_down, final_norm_g):
    bp, tp, _ = x_prompt.shape
    bs, ts, _ = x_sample.shape
    p_len = cache_diff_k.shape[2]
    depth = w_in.shape[0]
    assert depth == 1
    l = 0
    wi = w_in[l]
    sp = [CONV_DIM, CONV_DIM + KEY_DIM, CONV_DIM + KEY_DIM + HEADS, CONV_DIM + KEY_DIM + 2 * HEADS]
    w_ab = jnp.concatenate([wi[:, sp[1]:sp[3]], jnp.zeros((D_MODEL, LANES - 2 * HEADS), F32)], axis=1)
    lanes_pad = lambda a: jnp.concatenate([a, jnp.zeros((LANES - a.shape[0],), F32)])[None, :]
    row = lambda a: a.reshape(1, -1)
    w = {
        "norm_mix_g": row(norm_mix_g[l]),
        "w_main": jnp.concatenate([wi[:, :sp[1]], wi[:, sp[3]:]], axis=1).astype(BF16),
        "w_ab": w_ab.astype(BF16),
        "a_log": lanes_pad(gdn_a_log[l]),
        "dt_bias": lanes_pad(gdn_dt_bias[l]),
        "conv_w": gdn_conv_w[l],
        "gdn_norm_g": row(gdn_norm_g[l]),
        "diff_lambda": diff_lambda[l],
        "diff_norm_g_col": diff_norm_g[l].reshape(HEAD_DIM, 1),
        "w_out": w_out[l].astype(BF16),
        "norm_mem_g": row(norm_mem_g[l]),
        "mem_norm_g": row(mem_norm_g[l]),
        "w_mq": w_mq[l].astype(BF16),
        "w_mkv": w_mkv[l].astype(BF16),
        "w_mo": w_mo[l].astype(BF16),
        "norm_ffn_g": row(norm_ffn_g[l]),
        "w_up": w_up[l].astype(BF16),
        "w_down": w_down[l].astype(BF16),
        "final_norm_g": row(final_norm_g),
    }
    n_mem = mem_prompt.shape[1]
    mk, mv, mkb, mvb = _mem_kv(mem_prompt.reshape(bp * n_mem, D_MODEL), w, _tile(bp * n_mem, 512))
    mk = mk.reshape(bp, n_mem, HEADS, HEAD_DIM)
    mv = mv.reshape(bp, n_mem, HEADS, HEAD_DIM)
    mkb = mkb.reshape(bp, n_mem, KEY_DIM)
    mvb = mvb.reshape(bp, n_mem, KEY_DIM)

    zeros_conv = jnp.zeros((bp, CONV_W - 1, CONV_DIM), F32)
    zeros_state = jnp.zeros((bp, HEADS, HEAD_DIM, HEAD_DIM), F32)
    yp, sp_, cp, kp, vp = _layer(x_prompt, 0, zeros_conv, zeros_state, None, mkb, mvb, w)
    kv_prev = (cache_diff_k[l].reshape(bs, p_len, KEY_DIM), cache_diff_v[l].reshape(bs, p_len, KEY_DIM))
    ys, ss, cs, ks_, vs = _layer(x_sample, p_len, state_gdn_conv[l], state_gdn[l], kv_prev,
                                 cache_mem_k[l].reshape(bs, n_mem, KEY_DIM).astype(BF16),
                                 cache_mem_v[l].reshape(bs, n_mem, KEY_DIM).astype(BF16), w)
    return (yp, ys, sp_[None], cp[None], kp[None], vp[None], mk[None], mv[None],
            ss[None], cs[None], ks_[None], vs[None])
```

```python
import functools
import math

import jax
import jax.numpy as jnp
import numpy as np
from jax import lax
from jax.experimental import pallas as pl
from jax.experimental.pallas import tpu as pltpu

D_MODEL = 1024
CHUNK = 64
HEADS = 4
HEAD_DIM = 128
KEY_DIM = HEADS * HEAD_DIM
CONV_DIM = 3 * KEY_DIM
CONV_W = 4
DQK = 64
ROT_DIM = 16
ROPE_THETA = 500000.0
N_MEM = 256
D_FF = 4 * D_MODEL
NORM_EPS = 1e-6
MAIN_COLS = CONV_DIM + 4 * KEY_DIM
LANES = 128
CONV_PAD = 8
SUM_ROWS = 16
VMEM_LIMIT = 56 * 1024 * 1024
ROW_TILE = 512

F32 = jnp.float32
BF16 = jnp.bfloat16
HIGHEST = lax.Precision.HIGHEST


def _dot(a, b, precision=None):
    return jnp.dot(a, b, preferred_element_type=F32, precision=precision)


def _dot_nt(a, b):
    return lax.dot_general(a, b, (((1,), (1,)), ((), ())), preferred_element_type=F32)


def _dot_tn(a, b):
    return lax.dot_general(a, b, (((0,), (0,)), ((), ())), preferred_element_type=F32)


def _rms(x, g):
    return x * lax.rsqrt(jnp.mean(x * x, axis=-1, keepdims=True) + NORM_EPS) * g


def _sigmoid(x):
    return 1.0 / (1.0 + jnp.exp(-x))


def _params(*sem):
    return pltpu.CompilerParams(dimension_semantics=sem, vmem_limit_bytes=VMEM_LIMIT)


def _proj_in_kernel(x_ref, g_ref, wm_ref, wab_ref, alog_ref, dtb_ref, cos_ref, sna_ref, snb_ref,
                    qkv_ref, z_ref, gb_ref, q_ref, kf_ref, kb_ref, vf_ref, vb_ref, vt_ref):
    x = x_ref[...]
    h = _rms(x, g_ref[...]).astype(BF16)
    main = _dot(h, wm_ref[...])
    ab = _dot(h, wab_ref[...])
    qkv_ref[...] = main[:, :CONV_DIM]
    z_ref[...] = main[:, CONV_DIM:CONV_DIM + KEY_DIM]
    xa = ab + dtb_ref[...]
    softplus = jnp.maximum(xa, 0.0) + jnp.log1p(jnp.exp(-jnp.abs(xa)))
    lane = lax.broadcasted_iota(jnp.int32, ab.shape, 1)
    gb_ref[...] = jnp.where(lane < HEADS, -jnp.exp(alog_ref[...]) * softplus, _sigmoid(ab))
    cos, sna, snb = cos_ref[...], sna_ref[...], snb_ref[...]
    q0 = CONV_DIM + KEY_DIM
    krs, vhs = [], []
    for hd in range(HEADS):
        sl = slice(hd * HEAD_DIM, (hd + 1) * HEAD_DIM)
        qh = main[:, q0 + hd * HEAD_DIM:q0 + (hd + 1) * HEAD_DIM]
        kh = main[:, q0 + KEY_DIM + hd * HEAD_DIM:q0 + KEY_DIM + (hd + 1) * HEAD_DIM]
        vh = main[:, q0 + 2 * KEY_DIM + hd * HEAD_DIM:q0 + 2 * KEY_DIM + (hd + 1) * HEAD_DIM]
        qr = qh * cos + pltpu.roll(qh, LANES - ROT_DIM // 2, 1) * sna + pltpu.roll(qh, ROT_DIM // 2, 1) * snb
        kr = kh * cos + pltpu.roll(kh, LANES - ROT_DIM // 2, 1) * sna + pltpu.roll(kh, ROT_DIM // 2, 1) * snb
        q_ref[:, sl] = (qr * (DQK ** -0.5 * math.log2(math.e))).astype(BF16)
        krs.append(kr)
        vhs.append(vh)
        kb_ref[:, sl] = kr.astype(BF16)
        vb_ref[:, sl] = vh.astype(BF16)
        vt_ref[0, sl, :] = vh.T.astype(BF16)
    kf_ref[...] = jnp.concatenate(krs, axis=1).reshape(kf_ref.shape)
    vf_ref[...] = jnp.concatenate(vhs, axis=1).reshape(vf_ref.shape)


def _rope_tables(pos):
    inv = ROPE_THETA ** (-np.arange(0, ROT_DIM, 2, dtype=np.float64) / ROT_DIM)
    ang = pos.astype(np.float64)[:, None] * inv[None, :]
    cos, sin = np.cos(ang), np.sin(ang)
    pad = np.zeros((pos.shape[0], DQK - ROT_DIM))
    cos64 = np.concatenate([cos, cos, pad + 1.0], axis=1)
    sna64 = np.concatenate([-sin, np.zeros_like(sin), pad], axis=1)
    snb64 = np.concatenate([np.zeros_like(sin), sin, pad], axis=1)
    return tuple(jnp.asarray(np.concatenate([a, a], axis=1), F32) for a in (cos64, sna64, snb64))


def _proj_in(x2d, pos_rows, tab_rows, w, tm):
    n = x2d.shape[0]
    nt = tab_rows // tm
    cos, sna, snb = _rope_tables(pos_rows)
    row = lambda i: (i, 0)
    fix = lambda i: (0, 0)
    tab = lambda i: (i % nt, 0)
    out_shape = (
        jax.ShapeDtypeStruct((n, CONV_DIM), F32),
        jax.ShapeDtypeStruct((n, KEY_DIM), F32),
        jax.ShapeDtypeStruct((n, LANES), F32),
        jax.ShapeDtypeStruct((n, KEY_DIM), BF16),
        jax.ShapeDtypeStruct((n, HEADS, HEAD_DIM), F32),
        jax.ShapeDtypeStruct((n, KEY_DIM), BF16),
        jax.ShapeDtypeStruct((n, HEADS, HEAD_DIM), F32),
        jax.ShapeDtypeStruct((n, KEY_DIM), BF16),
        jax.ShapeDtypeStruct((n // tm, KEY_DIM, tm), BF16),
    )

    def spec(sd):
        if len(sd.shape) == 2:
            return pl.BlockSpec((tm, sd.shape[1]), row)
        return pl.BlockSpec((sd.shape[0] * tm // n,) + sd.shape[1:], lambda i: (i, 0, 0))
    out_specs = tuple(spec(sd) for sd in out_shape)
    return pl.pallas_call(
        _proj_in_kernel,
        grid=(n // tm,),
        in_specs=[
            pl.BlockSpec((tm, D_MODEL), row),
            pl.BlockSpec((1, D_MODEL), fix),
            pl.BlockSpec((D_MODEL, MAIN_COLS), fix),
            pl.BlockSpec((D_MODEL, LANES), fix),
            pl.BlockSpec((1, LANES), fix),
            pl.BlockSpec((1, LANES), fix),
            pl.BlockSpec((tm, LANES), tab),
            pl.BlockSpec((tm, LANES), tab),
            pl.BlockSpec((tm, LANES), tab),
        ],
        out_specs=out_specs,
        out_shape=out_shape,
        compiler_params=_params("parallel"),
        name="proj_in",
    )(x2d, w["norm_mix_g"], w["w_main"], w["w_ab"], w["a_log"], w["dt_bias"], cos, sna, snb)


def _gdn_kernel(qkv_ref, z_ref, gb_ref, cprev_ref, s0_ref, cw_ref, ng_ref,
                o_ref, sout_ref, xp_ref, y_ref, s_ref, *, c, nchunks):
    j = pl.program_id(1)
    rows = c * nchunks
    hc = HEADS * c

    @pl.when(j == 0)
    def _():
        xp_ref[0:CONV_PAD, :] = cprev_ref[0]
        s_ref[...] = s0_ref[0]

    xp_ref[CONV_PAD:CONV_PAD + rows, :] = qkv_ref[0]
    cw = cw_ref[...]
    xp = xp_ref[...]
    y = xp[CONV_PAD:] * cw[CONV_W - 1:CONV_W, :]
    for i in range(CONV_W - 1):
        y = y + pltpu.roll(xp, CONV_W - 1 - i, 0)[CONV_PAD:] * cw[i:i + 1, :]
    xp_ref[0:CONV_PAD, :] = xp[rows:rows + CONV_PAD]
    y_ref[...] = y * _sigmoid(y)

    ri = lax.broadcasted_iota(jnp.int32, (hc, hc), 0)
    cj = lax.broadcasted_iota(jnp.int32, (hc, hc), 1)
    same = (ri // c) == (cj // c)
    tril_bd = same & (ri >= cj)
    strict_bd = same & (ri > cj)
    eye_bd = (ri == cj).astype(F32)
    r1 = lax.broadcasted_iota(jnp.int32, (c, c), 0)
    c1 = lax.broadcasted_iota(jnp.int32, (c, c), 1)
    tril_c = (r1 >= c1).astype(F32)
    triu_c = (r1 <= c1).astype(F32)
    ng = ng_ref[...]
    gb_all = gb_ref[0]
    if rows % LANES:
        gb_all = jnp.concatenate([gb_all, jnp.zeros((LANES - rows % LANES, LANES), F32)], axis=0)
    gbt_all = gb_all.T[0:2 * HEADS, :]

    def prep(ci):
        rs = slice(ci * c, (ci + 1) * c)
        gcol = gb_ref[0, rs, :]
        gc_col = _dot(tril_c, gcol, HIGHEST)
        gc_row = _dot(gbt_all[:, rs], triu_c, HIGHEST)
        ks, kbs, qs, vbs, kbgs, qgs, kds, dcol, grow, glast = [], [], [], [], [], [], [], [], [], []
        for hd in range(HEADS):
            qh = y_ref[rs, hd * HEAD_DIM:(hd + 1) * HEAD_DIM]
            kh = y_ref[rs, KEY_DIM + hd * HEAD_DIM:KEY_DIM + (hd + 1) * HEAD_DIM]
            vh = y_ref[rs, 2 * KEY_DIM + hd * HEAD_DIM:2 * KEY_DIM + (hd + 1) * HEAD_DIM]
            qh = qh * lax.rsqrt(jnp.sum(qh * qh, axis=-1, keepdims=True) + NORM_EPS) * (HEAD_DIM ** -0.5)
            kh = kh * lax.rsqrt(jnp.sum(kh * kh, axis=-1, keepdims=True) + NORM_EPS)
            g_h = gc_col[:, hd:hd + 1]
            beta_h = gcol[:, HEADS + hd:HEADS + hd + 1]
            g_last = gc_col[c - 1:c, hd:hd + 1]
            eg = jnp.exp(g_h)
            kb = kh * beta_h
            ks.append(kh)
            kbs.append(kb)
            qs.append(qh)
            vbs.append(vh * beta_h)
            kbgs.append(kb * eg)
            qgs.append((qh * eg).astype(BF16))
            kds.append((kh * jnp.exp(g_last - g_h)).astype(BF16))
            glast.append(jnp.exp(g_last))
            dcol.append(jnp.broadcast_to(g_h, (c, hc)))
            grow.append(gc_row[hd:hd + 1, :])
        k_s = jnp.concatenate(ks, axis=0).astype(BF16)
        q_s = jnp.concatenate(qs, axis=0).astype(BF16)
        dmat = jnp.concatenate(dcol, axis=0) - jnp.concatenate(grow, axis=1)
        decay = jnp.exp(jnp.where(tril_bd, dmat, -jnp.inf))
        a_kk = _dot_nt(jnp.concatenate(kbs, axis=0).astype(BF16), k_s)
        x = jnp.where(strict_bd, -(a_kk * decay), 0.0)
        rhs = jnp.concatenate([jnp.concatenate(vbs, axis=0), jnp.concatenate(kbgs, axis=0)], axis=1)
        a_qk = (_dot_nt(q_s, k_s) * decay).astype(BF16)
        return dict(x=x, rhs=rhs.astype(BF16), a_qk=a_qk, qg=qgs, kd=kds, glast=glast)

    pre = [prep(ci) for ci in range(nchunks)]
    ps = [eye_bd + d["x"] for d in pre]
    xps = [d["x"] for d in pre]
    for _ in range(int(math.log2(c)) - 1):
        xbs = [xp.astype(BF16) for xp in xps]
        xps = [_dot(xb, xb) for xb in xbs]
        ps = [p + _dot(p.astype(BF16), xp.astype(BF16)) for p, xp in zip(ps, xps)]
    uws = [_dot(p.astype(BF16), d["rhs"]) for p, d in zip(ps, pre)]

    state = [s_ref[hd] for hd in range(HEADS)]
    for ci in range(nchunks):
        rs = slice(ci * c, (ci + 1) * c)
        d, uw = pre[ci], uws[ci]
        vnews, qss = [], []
        for hd in range(HEADS):
            hs = slice(hd * c, (hd + 1) * c)
            wq = jnp.concatenate([uw[hs, HEAD_DIM:].astype(BF16), d["qg"][hd]], axis=0)
            ws = _dot(wq, state[hd].astype(BF16))
            v_new = uw[hs, :HEAD_DIM] - ws[:c]
            vnews.append(v_new)
            qss.append(ws[c:])
            state[hd] = state[hd] * d["glast"][hd] + _dot_tn(d["kd"][hd], v_new.astype(BF16))
        o_s = jnp.concatenate(qss, axis=0) + _dot(d["a_qk"], jnp.concatenate(vnews, axis=0).astype(BF16))
        for hd in range(HEADS):
            sl = slice(hd * HEAD_DIM, (hd + 1) * HEAD_DIM)
            zh = z_ref[0, rs, sl]
            o_h = _rms(o_s[hd * c:(hd + 1) * c], ng) * (zh * _sigmoid(zh))
            o_ref[0, rs, sl] = o_h.astype(BF16)
    for hd in range(HEADS):
        s_ref[hd] = state[hd]

    @pl.when(j == pl.num_programs(1) - 1)
    def _():
        sout_ref[0] = s_ref[...]


def _gdn(qkv, z, gb, conv_prev, s0, w, nchunks):
    b, t, _ = qkv.shape
    c = min(t, CHUNK)
    rows = c * nchunks
    cprev = jnp.concatenate([jnp.zeros((b, CONV_PAD - (CONV_W - 1), CONV_DIM), F32), conv_prev], axis=1)
    blk = lambda i, j: (i, j, 0)
    kern = functools.partial(_gdn_kernel, c=c, nchunks=nchunks)
    return pl.pallas_call(
        kern,
        grid=(b, t // rows),
        in_specs=[
            pl.BlockSpec((1, rows, CONV_DIM), blk),
            pl.BlockSpec((1, rows, KEY_DIM), blk),
            pl.BlockSpec((1, rows, LANES), blk),
            pl.BlockSpec((1, CONV_PAD, CONV_DIM), lambda i, j: (i, 0, 0)),
            pl.BlockSpec((1, HEADS, HEAD_DIM, HEAD_DIM), lambda i, j: (i, 0, 0, 0)),
            pl.BlockSpec((CONV_W, CONV_DIM), lambda i, j: (0, 0)),
            pl.BlockSpec((1, HEAD_DIM), lambda i, j: (0, 0)),
        ],
        out_specs=(
            pl.BlockSpec((1, rows, KEY_DIM), blk),
            pl.BlockSpec((1, HEADS, HEAD_DIM, HEAD_DIM), lambda i, j: (i, 0, 0, 0)),
        ),
        out_shape=(
            jax.ShapeDtypeStruct((b, t, KEY_DIM), BF16),
            jax.ShapeDtypeStruct((b, HEADS, HEAD_DIM, HEAD_DIM), F32),
        ),
        scratch_shapes=[
            pltpu.VMEM((rows + CONV_PAD, CONV_DIM), F32),
            pltpu.VMEM((rows, CONV_DIM), F32),
            pltpu.VMEM((HEADS, HEAD_DIM, HEAD_DIM), F32),
        ],
        compiler_params=_params("parallel", "arbitrary"),
        name="gdn",
    )(qkv, z, gb, cprev, s0, w["conv_w"], w["gdn_norm_g"])


def _diff_attn_kernel(q_ref, qn_ref, k_ref, vt_ref, lam_ref, ng_ref, o_ref, s0_ref, s1_ref, m_ref, acc_ref, qst_ref,
                      *, tq, tqp, tk, nblk, n_keys, q_pos0, lam_init):
    qi = pl.program_id(2)
    width = 2 * tqp
    pos_lo = q_pos0 + qi * tq
    k_lo = jnp.minimum((pos_lo // CHUNK + 1) * CHUNK, n_keys)
    k_hi = jnp.minimum(((pos_lo + tq - 1) // CHUNK + 1) * CHUNK, n_keys)
    n_full = k_lo // tk
    n_all = (k_hi + tk - 1) // tk

    def scores(jb, s_ref):
        jl = jnp.minimum(jb, nblk - 1)
        ks = pl.ds(pl.multiple_of(jl * tk, tk), tk)
        s_ref[:, 0:width] = _dot(k_ref[0, ks, :], qst_ref[...])

    def start(qx_ref):
        q = qx_ref[0]
        if tqp > tq:
            q = jnp.concatenate([q, jnp.zeros((tqp - tq, HEAD_DIM), q.dtype)], axis=0)
        lane = lax.broadcasted_iota(jnp.int32, q.shape, 1)
        zero = jnp.zeros_like(q)
        qs = jnp.concatenate([jnp.where(lane < DQK, q, zero), jnp.where(lane >= DQK, q, zero)], axis=0)
        qst_ref[...] = qs.T
        m_ref[...] = jnp.full(m_ref.shape, -jnp.inf, F32)
        acc_ref[...] = jnp.zeros(acc_ref.shape, F32)
        scores(0, s0_ref)

    def mask(jb, s_ref):
        @pl.when(jb >= n_full)
        def _():
            col = lax.broadcasted_iota(jnp.int32, (tk, width), 1)
            qpos = pos_lo + jnp.where(col >= tqp, col - tqp, col)
            kpos = jb * tk + lax.broadcasted_iota(jnp.int32, (tk, width), 0)
            vis = kpos < jnp.minimum((qpos // CHUNK + 1) * CHUNK, n_keys)
            s_ref[:, 0:width] = jnp.where(vis, s_ref[:, 0:width], -jnp.inf)

    def softmax_pv(jb, s_ref):
        st = s_ref[:, 0:width]
        m_old = m_ref[...]
        m_new = jnp.maximum(m_old, jnp.max(st, axis=0, keepdims=True))
        alpha = jnp.exp2(m_old - m_new)
        p = jnp.exp2(st - m_new)
        vt1 = jnp.concatenate([vt_ref[jnp.minimum(jb, nblk - 1)], jnp.ones((SUM_ROWS, tk), BF16)], axis=0)
        acc_ref[...] = alpha * acc_ref[...] + _dot(vt1, p.astype(BF16))
        m_ref[...] = m_new

    @pl.when(qi == 0)
    def _():
        start(q_ref)

    def full_pair(i, carry):
        j = 2 * i
        scores(j + 1, s1_ref)
        softmax_pv(j, s0_ref)
        scores(j + 2, s0_ref)
        softmax_pv(j + 1, s1_ref)
        return carry

    def edge_pair(i, carry):
        j = 2 * i
        mask(j, s0_ref)
        scores(j + 1, s1_ref)
        softmax_pv(j, s0_ref)

        @pl.when(j + 1 < n_all)
        def _():
            mask(j + 1, s1_ref)
            scores(j + 2, s0_ref)
            softmax_pv(j + 1, s1_ref)
        return carry

    lax.fori_loop(0, n_full // 2, full_pair, 0)
    lax.fori_loop(n_full // 2, (n_all + 1) // 2, edge_pair, 0)
    o = acc_ref[0:HEAD_DIM, :] * (1.0 / acc_ref[HEAD_DIM:HEAD_DIM + 1, :])
    start(qn_ref)
    lf = lam_ref[...]
    lam = (jnp.exp(jnp.sum(lf[0:1] * lf[1:2], axis=-1, keepdims=True))
           - jnp.exp(jnp.sum(lf[2:3] * lf[3:4], axis=-1, keepdims=True)) + lam_init)
    od = o[:, :tqp] - lam * o[:, tqp:]
    ms = jnp.mean(od * od, axis=0, keepdims=True)
    on = od * lax.rsqrt(ms + NORM_EPS) * ng_ref[...] * (1.0 - lam_init)
    o_ref[0] = on.T[:tq].astype(BF16)


def _diff_attn(q, k, vt, w, n_keys, q_pos0, tq, tk):
    b, t, _ = q.shape
    tkp = k.shape[1]
    nblk = tkp // tk
    tqp = max(tq, LANES)
    lam_init = 0.8 - 0.6 * math.exp(-0.3 * 0)
    kern = functools.partial(_diff_attn_kernel, tq=tq, tqp=tqp, tk=tk, nblk=nblk, n_keys=n_keys,
                             q_pos0=q_pos0, lam_init=lam_init)
    return pl.pallas_call(
        kern,
        grid=(b, HEADS, t // tq),
        in_specs=[
            pl.BlockSpec((1, tq, HEAD_DIM), lambda i, h, j: (i, j, h)),
            pl.BlockSpec((1, tq, HEAD_DIM), lambda i, h, j: (i, jnp.minimum(j + 1, t // tq - 1), h)),
            pl.BlockSpec((1, tkp, HEAD_DIM), lambda i, h, j: (i, 0, h)),
            pl.BlockSpec((nblk, HEAD_DIM, tk), lambda i, h, j: (i, h, 0)),
            pl.BlockSpec((4, DQK), lambda i, h, j: (0, 0)),
            pl.BlockSpec((HEAD_DIM, 1), lambda i, h, j: (0, 0)),
        ],
        out_specs=pl.BlockSpec((1, tq, HEAD_DIM), lambda i, h, j: (i, j, h)),
        out_shape=jax.ShapeDtypeStruct((b, t, KEY_DIM), BF16),
        scratch_shapes=[
            pltpu.VMEM((tk, 2 * tqp + LANES), F32),
            pltpu.VMEM((tk, 2 * tqp + LANES), F32),
            pltpu.VMEM((1, 2 * tqp), F32),
            pltpu.VMEM((HEAD_DIM + SUM_ROWS, 2 * tqp), F32),
            pltpu.VMEM((HEAD_DIM, 2 * tqp), BF16),
        ],
        compiler_params=_params("arbitrary", "arbitrary", "arbitrary"),
        name="diff_attn",
    )(q, q, k, vt, w["diff_lambda"], w["diff_norm_g_col"])


def _post_mix_kernel(x_ref, og_ref, od_ref, mk_ref, mv_ref, wo_ref, g_ref, wq_ref, wmo_ref, o_ref):
    nb, tm, _ = x_ref.shape
    rows = lambda ref: ref[...].reshape(nb * tm, ref.shape[-1])
    x1 = rows(x_ref) + _dot(rows(og_ref), wo_ref[0:KEY_DIM, :]) + _dot(rows(od_ref), wo_ref[KEY_DIM:, :])
    h2 = _rms(x1, g_ref[...]).astype(BF16)
    qm = _dot(h2, wq_ref[...]).astype(BF16)
    oms = []
    for bi in range(nb):
        outs = []
        for hd in range(HEADS):
            sl = slice(hd * HEAD_DIM, (hd + 1) * HEAD_DIM)
            mk = mk_ref[bi, :, sl]
            mv = mv_ref[bi, :, sl]
            s = _dot_nt(qm[bi * tm:(bi + 1) * tm, sl], mk) * (HEAD_DIM ** -0.5)
            p = jnp.exp(s - jnp.max(s, axis=-1, keepdims=True))
            l = jnp.sum(p, axis=-1, keepdims=True)
            outs.append(_dot((p / l).astype(BF16), mv))
        oms.append(jnp.concatenate(outs, axis=1))
    om = jnp.concatenate(oms, axis=0).astype(BF16)
    o_ref[...] = (x1 + _dot(om, wmo_ref[...])).reshape(o_ref.shape)


def _post_mix(x, og, od, mk, mv, w, tm):
    b, t, _ = x.shape
    nb = _tile(b, max(1, ROW_TILE // tm)) if tm == t else 1
    blk = lambda i, j: (i, j, 0)
    fix = lambda i, j: (0, 0)
    mem = lambda i, j: (i, 0, 0)
    return pl.pallas_call(
        _post_mix_kernel,
        grid=(b // nb, t // tm),
        in_specs=[
            pl.BlockSpec((nb, tm, D_MODEL), blk),
            pl.BlockSpec((nb, tm, KEY_DIM), blk),
            pl.BlockSpec((nb, tm, KEY_DIM), blk),
            pl.BlockSpec((nb, N_MEM, KEY_DIM), mem),
            pl.BlockSpec((nb, N_MEM, KEY_DIM), mem),
            pl.BlockSpec((D_MODEL, D_MODEL), fix),
            pl.BlockSpec((1, D_MODEL), fix),
            pl.BlockSpec((D_MODEL, KEY_DIM), fix),
            pl.BlockSpec((KEY_DIM, D_MODEL), fix),
        ],
        out_specs=pl.BlockSpec((nb, tm, D_MODEL), blk),
        out_shape=jax.ShapeDtypeStruct((b, t, D_MODEL), F32),
        compiler_params=_params("parallel", "parallel"),
        name="post_mix",
    )(x, og, od, mk, mv, w["w_out"], w["norm_mem_g"], w["w_mq"], w["w_mo"])


def _mlp_kernel(x_ref, g_ref, wu_ref, wd_ref, fg_ref, o_ref, *, ff_blk):
    x = x_ref[...]
    hf = _rms(x, g_ref[...]).astype(BF16)
    acc = x
    for c0 in range(0, D_FF, ff_blk):
        u = jnp.maximum(_dot(hf, wu_ref[:, c0:c0 + ff_blk]), 0.0)
        acc = acc + _dot((u * u).astype(BF16), wd_ref[c0:c0 + ff_blk, :])
    o_ref[...] = _rms(acc, fg_ref[...])


def _mlp(x2d, w, tm):
    n = x2d.shape[0]
    row = lambda i: (i, 0)
    fix = lambda i: (0, 0)
    return pl.pallas_call(
        functools.partial(_mlp_kernel, ff_blk=D_MODEL),
        grid=(n // tm,),
        in_specs=[
            pl.BlockSpec((tm, D_MODEL), row),
            pl.BlockSpec((1, D_MODEL), fix),
            pl.BlockSpec((D_MODEL, D_FF), fix),
            pl.BlockSpec((D_FF, D_MODEL), fix),
            pl.BlockSpec((1, D_MODEL), fix),
        ],
        out_specs=pl.BlockSpec((tm, D_MODEL), row),
        out_shape=jax.ShapeDtypeStruct((n, D_MODEL), F32),
        compiler_params=_params("parallel"),
        name="mlp",
    )(x2d, w["norm_ffn_g"], w["w_up"], w["w_down"], w["final_norm_g"])


def _mem_kv_kernel(m_ref, g_ref, w_ref, k_ref, v_ref, kb_ref, vb_ref):
    mh = _rms(m_ref[...], g_ref[...]).astype(BF16)
    kv = _dot(mh, w_ref[...])
    kb_ref[...] = kv[:, :KEY_DIM].astype(BF16)
    vb_ref[...] = kv[:, KEY_DIM:].astype(BF16)
    k_ref[...] = kv[:, :KEY_DIM].reshape(k_ref.shape)
    v_ref[...] = kv[:, KEY_DIM:].reshape(v_ref.shape)


def _mem_kv(mem2d, w, tm):
    n = mem2d.shape[0]
    row = lambda i: (i, 0)
    fix = lambda i: (0, 0)
    return pl.pallas_call(
        _mem_kv_kernel,
        grid=(n // tm,),
        in_specs=[
            pl.BlockSpec((tm, D_MODEL), row),
            pl.BlockSpec((1, D_MODEL), fix),
            pl.BlockSpec((D_MODEL, 2 * KEY_DIM), fix),
        ],
        out_specs=(pl.BlockSpec((tm, HEADS, HEAD_DIM), lambda i: (i, 0, 0)),) * 2
        + (pl.BlockSpec((tm, KEY_DIM), row),) * 2,
        out_shape=(jax.ShapeDtypeStruct((n, HEADS, HEAD_DIM), F32),) * 2
        + (jax.ShapeDtypeStruct((n, KEY_DIM), BF16),) * 2,
        compiler_params=_params("parallel"),
        name="mem_kv",
    )(mem2d, w["mem_norm_g"], w["w_mkv"])


def _tile(n, pref):
    t = min(n, pref)
    assert n % t == 0, (n, t)
    return t


def _layer(x, pos0, conv_prev, s0, kv_prev, mem_k, mem_v, w):
    b, t, _ = x.shape
    n = b * t
    c = min(t, CHUNK)
    x2d = x.reshape(n, D_MODEL)
    tm = _tile(n, ROW_TILE)
    tab_rows = t if t % tm == 0 else n
    pos_rows = pos0 + (np.arange(tab_rows) % t)
    qkv, z, gb, qd, kf, kb, vf, vb, vt = _proj_in(x2d, pos_rows, tab_rows, w, tm)
    r3 = lambda a: a.reshape(b, t, -1)
    qkv3 = r3(qkv)
    og, s_new = _gdn(qkv3, r3(z), r3(gb), conv_prev, s0, w, nchunks=_tile(t // c, 8))
    conv_new = qkv3[:, t - (CONV_W - 1):, :]
    kb3 = r3(kb)
    if kv_prev is None and t % tm == 0:
        n_keys, tk = t, tm
    else:
        vb3 = r3(vb)
        if kv_prev is not None:
            kb3 = jnp.concatenate([kv_prev[0].astype(BF16), kb3], axis=1)
            vb3 = jnp.concatenate([kv_prev[1].astype(BF16), vb3], axis=1)
        n_keys = kb3.shape[1]
        pad = (-n_keys) % LANES
        kb3 = jnp.pad(kb3, ((0, 0), (0, pad), (0, 0)))
        vb3 = jnp.pad(vb3, ((0, 0), (0, pad), (0, 0)))
        tk = _tile(n_keys + pad, 2048)
        nblk = (n_keys + pad) // tk
        vt = vb3.reshape(b, nblk, tk, KEY_DIM).swapaxes(2, 3).reshape(b * nblk, KEY_DIM, tk)
    od = _diff_attn(r3(qd), kb3, vt, w, n_keys=n_keys, q_pos0=pos0, tq=_tile(t, ROW_TILE), tk=tk)
    x2 = _post_mix(x, og, od, mem_k, mem_v, w, tm=_tile(t, ROW_TILE))
    y = _mlp(x2.reshape(n, D_MODEL), w, tm).reshape(b, t, D_MODEL)
    h4 = lambda a: a.reshape(b, t, HEADS, HEAD_DIM)
    return y, s_new, conv_new, h4(kf), h4(vf)


def kernel(x_prompt, x_sample, mem_prompt, cache_diff_k, cache_diff_v, cache_mem_k, cache_mem_v, state_gdn, state_gdn_conv, norm_mix_g, w_in, gdn_conv_w, gdn_a_log, gdn_dt_bias, gdn_norm_g, diff_lambda, diff_norm_g, w_out, norm_mem_g, mem_norm_g, w_mq, w_mkv, w_mo, norm_ffn_g, w_up, w_down, final_norm_g):
    bp, tp, _ = x_prompt.shape
    bs, ts, _ = x_sample.shape
    p_len = cache_diff_k.shape[2]
    depth = w_in.shape[0]
    assert depth == 1
    l = 0
    wi = w_in[l]
    sp = [CONV_DIM, CONV_DIM + KEY_DIM, CONV_DIM + KEY_DIM + HEADS, CONV_DIM + KEY_DIM + 2 * HEADS]
    w_ab = jnp.concatenate([wi[:, sp[1]:sp[3]], jnp.zeros((D_MODEL, LANES - 2 * HEADS), F32)], axis=1)
    lanes_pad = lambda a: jnp.concatenate([a, jnp.zeros((LANES - a.shape[0],), F32)])[None, :]
    row = lambda a: a.reshape(1, -1)
    w = {
        "norm_mix_g": row(norm_mix_g[l]),
        "w_main": jnp.concatenate([wi[:, :sp[1]], wi[:, sp[3]:]], axis=1).astype(BF16),
        "w_ab": w_ab.astype(BF16),
        "a_log": lanes_pad(gdn_a_log[l]),
        "dt_bias": lanes_pad(gdn_dt_bias[l]),
        "conv_w": gdn_conv_w[l],
        "gdn_norm_g": row(gdn_norm_g[l]),
        "diff_lambda": diff_lambda[l],
        "diff_norm_g_col": diff_norm_g[l].reshape(HEAD_DIM, 1),
        "w_out": w_out[l].astype(BF16),
        "norm_mem_g": row(norm_mem_g[l]),
        "mem_norm_g": row(mem_norm_g[l]),
        "w_mq": w_mq[l].astype(BF16),
        "w_mkv": w_mkv[l].astype(BF16),
        "w_mo": w_mo[l].astype(BF16),
        "norm_ffn_g": row(norm_ffn_g[l]),
        "w_up": w_up[l].astype(BF16),
        "w_down": w_down[l].astype(BF16),
        "final_norm_g": row(final_norm_g),
    }
    n_mem = mem_prompt.shape[1]
    mk, mv, mkb, mvb = _mem_kv(mem_prompt.reshape(bp * n_mem, D_MODEL), w, _tile(bp * n_mem, ROW_TILE))
    mk = mk.reshape(bp, n_mem, HEADS, HEAD_DIM)
    mv = mv.reshape(bp, n_mem, HEADS, HEAD_DIM)
    mkb = mkb.reshape(bp, n_mem, KEY_DIM)
    mvb = mvb.reshape(bp, n_mem, KEY_DIM)

    zeros_conv = jnp.zeros((bp, CONV_W - 1, CONV_DIM), F32)
    zeros_state = jnp.zeros((bp, HEADS, HEAD_DIM, HEAD_DIM), F32)
    yp, sp_, cp, kp, vp = _layer(x_prompt, 0, zeros_conv, zeros_state, None, mkb, mvb, w)
    kv_prev = (cache_diff_k[l].astype(BF16).reshape(bs, p_len, KEY_DIM),
               cache_diff_v[l].astype(BF16).reshape(bs, p_len, KEY_DIM))
    ys, ss, cs, ks_, vs = _layer(x_sample, p_len, state_gdn_conv[l], state_gdn[l], kv_prev,
                                 cache_mem_k[l].astype(BF16).reshape(bs, n_mem, KEY_DIM),
                                 cache_mem_v[l].astype(BF16).reshape(bs, n_mem, KEY_DIM), w)
    return (yp, ys, sp_[None], cp[None], kp[None], vp[None], mk[None], mv[None],
            ss[None], cs[None], ks_[None], vs[None])
```

```python
import functools
import math

import jax
import jax.numpy as jnp
import numpy as np
from jax import lax
from jax.experimental import pallas as pl
from jax.experimental.pallas import tpu as pltpu

D_MODEL = 1024
CHUNK = 64
HEADS = 4
HEAD_DIM = 128
KEY_DIM = HEADS * HEAD_DIM
CONV_DIM = 3 * KEY_DIM
CONV_W = 4
DQK = 64
ROT_DIM = 16
ROPE_THETA = 500000.0
N_MEM = 256
D_FF = 4 * D_MODEL
NORM_EPS = 1e-6
MAIN_COLS = CONV_DIM + 4 * KEY_DIM
LANES = 128
CONV_PAD = 8
SUM_ROWS = 16
VMEM_LIMIT = 56 * 1024 * 1024
ROW_TILE = 512

F32 = jnp.float32
BF16 = jnp.bfloat16
HIGHEST = lax.Precision.HIGHEST


def _dot(a, b, precision=None):
    return jnp.dot(a, b, preferred_element_type=F32, precision=precision)


def _dot_nt(a, b):
    return lax.dot_general(a, b, (((1,), (1,)), ((), ())), preferred_element_type=F32)


def _dot_tn(a, b):
    return lax.dot_general(a, b, (((0,), (0,)), ((), ())), preferred_element_type=F32)


def _rms(x, g):
    return x * lax.rsqrt(jnp.mean(x * x, axis=-1, keepdims=True) + NORM_EPS) * g


def _sigmoid(x):
    return 1.0 / (1.0 + jnp.exp(-x))


def _params(*sem):
    return pltpu.CompilerParams(dimension_semantics=sem, vmem_limit_bytes=VMEM_LIMIT)


def _proj_in_kernel(x_ref, g_ref, wm_ref, wab_ref, alog_ref, dtb_ref, cos_ref, sna_ref, snb_ref,
                    qkv_ref, z_ref, gb_ref, q_ref, kf_ref, kb_ref, vf_ref, vb_ref, vt_ref):
    x = x_ref[...]
    h = _rms(x, g_ref[...]).astype(BF16)
    main = _dot(h, wm_ref[...])
    ab = _dot(h, wab_ref[...])
    qkv_ref[...] = main[:, :CONV_DIM]
    z_ref[...] = main[:, CONV_DIM:CONV_DIM + KEY_DIM]
    xa = ab + dtb_ref[...]
    softplus = jnp.maximum(xa, 0.0) + jnp.log1p(jnp.exp(-jnp.abs(xa)))
    lane = lax.broadcasted_iota(jnp.int32, ab.shape, 1)
    gb_ref[...] = jnp.where(lane < HEADS, -jnp.exp(alog_ref[...]) * softplus, _sigmoid(ab))
    cos, sna, snb = cos_ref[...], sna_ref[...], snb_ref[...]
    q0 = CONV_DIM + KEY_DIM
    krs, vhs = [], []
    for hd in range(HEADS):
        sl = slice(hd * HEAD_DIM, (hd + 1) * HEAD_DIM)
        qh = main[:, q0 + hd * HEAD_DIM:q0 + (hd + 1) * HEAD_DIM]
        kh = main[:, q0 + KEY_DIM + hd * HEAD_DIM:q0 + KEY_DIM + (hd + 1) * HEAD_DIM]
        vh = main[:, q0 + 2 * KEY_DIM + hd * HEAD_DIM:q0 + 2 * KEY_DIM + (hd + 1) * HEAD_DIM]
        qr = qh * cos + pltpu.roll(qh, LANES - ROT_DIM // 2, 1) * sna + pltpu.roll(qh, ROT_DIM // 2, 1) * snb
        kr = kh * cos + pltpu.roll(kh, LANES - ROT_DIM // 2, 1) * sna + pltpu.roll(kh, ROT_DIM // 2, 1) * snb
        q_ref[:, sl] = (qr * (DQK ** -0.5 * math.log2(math.e))).astype(BF16)
        krs.append(kr)
        vhs.append(vh)
        kb_ref[:, sl] = kr.astype(BF16)
        vb_ref[:, sl] = vh.astype(BF16)
        vt_ref[0, sl, :] = vh.T.astype(BF16)
    kf_ref[...] = jnp.concatenate(krs, axis=1).reshape(kf_ref.shape)
    vf_ref[...] = jnp.concatenate(vhs, axis=1).reshape(vf_ref.shape)


def _rope_tables(pos):
    inv = ROPE_THETA ** (-np.arange(0, ROT_DIM, 2, dtype=np.float64) / ROT_DIM)
    ang = pos.astype(np.float64)[:, None] * inv[None, :]
    cos, sin = np.cos(ang), np.sin(ang)
    pad = np.zeros((pos.shape[0], DQK - ROT_DIM))
    cos64 = np.concatenate([cos, cos, pad + 1.0], axis=1)
    sna64 = np.concatenate([-sin, np.zeros_like(sin), pad], axis=1)
    snb64 = np.concatenate([np.zeros_like(sin), sin, pad], axis=1)
    return tuple(jnp.asarray(np.concatenate([a, a], axis=1), F32) for a in (cos64, sna64, snb64))


def _proj_in(x2d, pos_rows, tab_rows, w, tm):
    n = x2d.shape[0]
    nt = tab_rows // tm
    cos, sna, snb = _rope_tables(pos_rows)
    row = lambda i: (i, 0)
    fix = lambda i: (0, 0)
    tab = lambda i: (i % nt, 0)
    out_shape = (
        jax.ShapeDtypeStruct((n, CONV_DIM), F32),
        jax.ShapeDtypeStruct((n, KEY_DIM), F32),
        jax.ShapeDtypeStruct((n, LANES), F32),
        jax.ShapeDtypeStruct((n, KEY_DIM), BF16),
        jax.ShapeDtypeStruct((n, HEADS, HEAD_DIM), F32),
        jax.ShapeDtypeStruct((n, KEY_DIM), BF16),
        jax.ShapeDtypeStruct((n, HEADS, HEAD_DIM), F32),
        jax.ShapeDtypeStruct((n, KEY_DIM), BF16),
        jax.ShapeDtypeStruct((n // tm, KEY_DIM, tm), BF16),
    )

    def spec(sd):
        if len(sd.shape) == 2:
            return pl.BlockSpec((tm, sd.shape[1]), row)
        return pl.BlockSpec((sd.shape[0] * tm // n,) + sd.shape[1:], lambda i: (i, 0, 0))
    out_specs = tuple(spec(sd) for sd in out_shape)
    return pl.pallas_call(
        _proj_in_kernel,
        grid=(n // tm,),
        in_specs=[
            pl.BlockSpec((tm, D_MODEL), row),
            pl.BlockSpec((1, D_MODEL), fix),
            pl.BlockSpec((D_MODEL, MAIN_COLS), fix),
            pl.BlockSpec((D_MODEL, LANES), fix),
            pl.BlockSpec((1, LANES), fix),
            pl.BlockSpec((1, LANES), fix),
            pl.BlockSpec((tm, LANES), tab),
            pl.BlockSpec((tm, LANES), tab),
            pl.BlockSpec((tm, LANES), tab),
        ],
        out_specs=out_specs,
        out_shape=out_shape,
        compiler_params=_params("parallel"),
        name="proj_in",
    )(x2d, w["norm_mix_g"], w["w_main"], w["w_ab"], w["a_log"], w["dt_bias"], cos, sna, snb)


def _gdn_kernel(qkv_ref, z_ref, gb_ref, cprev_ref, s0_ref, cw_ref, ng_ref,
                o_ref, sout_ref, xp_ref, y_ref, s_ref, *, c, nchunks):
    j = pl.program_id(1)
    rows = c * nchunks
    hc = HEADS * c

    @pl.when(j == 0)
    def _():
        xp_ref[0:CONV_PAD, :] = cprev_ref[0]
        s_ref[...] = s0_ref[0]

    xp_ref[CONV_PAD:CONV_PAD + rows, :] = qkv_ref[0]
    cw = cw_ref[...]
    xp = xp_ref[...]
    y = xp[CONV_PAD:] * cw[CONV_W - 1:CONV_W, :]
    for i in range(CONV_W - 1):
        y = y + pltpu.roll(xp, CONV_W - 1 - i, 0)[CONV_PAD:] * cw[i:i + 1, :]
    xp_ref[0:CONV_PAD, :] = xp[rows:rows + CONV_PAD]
    y_ref[...] = y * _sigmoid(y)

    ri = lax.broadcasted_iota(jnp.int32, (hc, hc), 0)
    cj = lax.broadcasted_iota(jnp.int32, (hc, hc), 1)
    same = (ri // c) == (cj // c)
    tril_bd = same & (ri >= cj)
    strict_bd = same & (ri > cj)
    eye_bd = (ri == cj).astype(F32)
    r1 = lax.broadcasted_iota(jnp.int32, (c, c), 0)
    c1 = lax.broadcasted_iota(jnp.int32, (c, c), 1)
    tril_c = (r1 >= c1).astype(F32)
    triu_c = (r1 <= c1).astype(F32)
    ng = ng_ref[...]
    gb_all = gb_ref[0]
    if rows % LANES:
        gb_all = jnp.concatenate([gb_all, jnp.zeros((LANES - rows % LANES, LANES), F32)], axis=0)
    gbt_all = gb_all.T[0:2 * HEADS, :]

    def prep(ci):
        rs = slice(ci * c, (ci + 1) * c)
        gcol = gb_ref[0, rs, :]
        gc_col = _dot(tril_c, gcol, HIGHEST)
        gc_row = _dot(gbt_all[:, rs], triu_c, HIGHEST)
        ks, kbs, qs, vbs, kbgs, qgs, kds, dcol, grow, glast = [], [], [], [], [], [], [], [], [], []
        for hd in range(HEADS):
            qh = y_ref[rs, hd * HEAD_DIM:(hd + 1) * HEAD_DIM]
            kh = y_ref[rs, KEY_DIM + hd * HEAD_DIM:KEY_DIM + (hd + 1) * HEAD_DIM]
            vh = y_ref[rs, 2 * KEY_DIM + hd * HEAD_DIM:2 * KEY_DIM + (hd + 1) * HEAD_DIM]
            qh = qh * lax.rsqrt(jnp.sum(qh * qh, axis=-1, keepdims=True) + NORM_EPS) * (HEAD_DIM ** -0.5)
            kh = kh * lax.rsqrt(jnp.sum(kh * kh, axis=-1, keepdims=True) + NORM_EPS)
            g_h = gc_col[:, hd:hd + 1]
            beta_h = gcol[:, HEADS + hd:HEADS + hd + 1]
            g_last = gc_col[c - 1:c, hd:hd + 1]
            eg = jnp.exp(g_h)
            kb = kh * beta_h
            ks.append(kh)
            kbs.append(kb)
            qs.append(qh)
            vbs.append(vh * beta_h)
            kbgs.append(kb * eg)
            qgs.append((qh * eg).astype(BF16))
            kds.append((kh * jnp.exp(g_last - g_h)).astype(BF16))
            glast.append(jnp.exp(g_last))
            dcol.append(jnp.broadcast_to(g_h, (c, hc)))
            grow.append(gc_row[hd:hd + 1, :])
        k_s = jnp.concatenate(ks, axis=0).astype(BF16)
        q_s = jnp.concatenate(qs, axis=0).astype(BF16)
        dmat = jnp.concatenate(dcol, axis=0) - jnp.concatenate(grow, axis=1)
        decay = jnp.exp(jnp.where(tril_bd, dmat, -jnp.inf))
        a_kk = _dot_nt(jnp.concatenate(kbs, axis=0).astype(BF16), k_s)
        x = jnp.where(strict_bd, -(a_kk * decay), 0.0)
        rhs = jnp.concatenate([jnp.concatenate(vbs, axis=0), jnp.concatenate(kbgs, axis=0)], axis=1)
        a_qk = (_dot_nt(q_s, k_s) * decay).astype(BF16)
        return dict(x=x, rhs=rhs.astype(BF16), a_qk=a_qk, qg=qgs, kd=kds, glast=glast)

    pre = [prep(ci) for ci in range(nchunks)]
    ps = [eye_bd + d["x"] for d in pre]
    xps = [d["x"] for d in pre]
    for _ in range(int(math.log2(c)) - 1):
        xbs = [xp.astype(BF16) for xp in xps]
        xps = [_dot(xb, xb) for xb in xbs]
        ps = [p + _dot(p.astype(BF16), xp.astype(BF16)) for p, xp in zip(ps, xps)]
    uws = [_dot(p.astype(BF16), d["rhs"]) for p, d in zip(ps, pre)]

    state = [s_ref[hd] for hd in range(HEADS)]
    for ci in range(nchunks):
        rs = slice(ci * c, (ci + 1) * c)
        d, uw = pre[ci], uws[ci]
        vnews, qss = [], []
        for hd in range(HEADS):
            hs = slice(hd * c, (hd + 1) * c)
            wq = jnp.concatenate([uw[hs, HEAD_DIM:].astype(BF16), d["qg"][hd]], axis=0)
            ws = _dot(wq, state[hd].astype(BF16))
            v_new = uw[hs, :HEAD_DIM] - ws[:c]
            vnews.append(v_new)
            qss.append(ws[c:])
            state[hd] = state[hd] * d["glast"][hd] + _dot_tn(d["kd"][hd], v_new.astype(BF16))
        o_s = jnp.concatenate(qss, axis=0) + _dot(d["a_qk"], jnp.concatenate(vnews, axis=0).astype(BF16))
        for hd in range(HEADS):
            sl = slice(hd * HEAD_DIM, (hd + 1) * HEAD_DIM)
            zh = z_ref[0, rs, sl]
            o_h = _rms(o_s[hd * c:(hd + 1) * c], ng) * (zh * _sigmoid(zh))
            o_ref[0, rs, sl] = o_h.astype(BF16)
    for hd in range(HEADS):
        s_ref[hd] = state[hd]

    @pl.when(j == pl.num_programs(1) - 1)
    def _():
        sout_ref[0] = s_ref[...]


def _gdn(qkv, z, gb, conv_prev, s0, w, nchunks):
    b, t, _ = qkv.shape
    c = min(t, CHUNK)
    rows = c * nchunks
    cprev = jnp.concatenate([jnp.zeros((b, CONV_PAD - (CONV_W - 1), CONV_DIM), F32), conv_prev], axis=1)
    blk = lambda i, j: (i, j, 0)
    kern = functools.partial(_gdn_kernel, c=c, nchunks=nchunks)
    return pl.pallas_call(
        kern,
        grid=(b, t // rows),
        in_specs=[
            pl.BlockSpec((1, rows, CONV_DIM), blk),
            pl.BlockSpec((1, rows, KEY_DIM), blk),
            pl.BlockSpec((1, rows, LANES), blk),
            pl.BlockSpec((1, CONV_PAD, CONV_DIM), lambda i, j: (i, 0, 0)),
            pl.BlockSpec((1, HEADS, HEAD_DIM, HEAD_DIM), lambda i, j: (i, 0, 0, 0)),
            pl.BlockSpec((CONV_W, CONV_DIM), lambda i, j: (0, 0)),
            pl.BlockSpec((1, HEAD_DIM), lambda i, j: (0, 0)),
        ],
        out_specs=(
            pl.BlockSpec((1, rows, KEY_DIM), blk),
            pl.BlockSpec((1, HEADS, HEAD_DIM, HEAD_DIM), lambda i, j: (i, 0, 0, 0)),
        ),
        out_shape=(
            jax.ShapeDtypeStruct((b, t, KEY_DIM), BF16),
            jax.ShapeDtypeStruct((b, HEADS, HEAD_DIM, HEAD_DIM), F32),
        ),
        scratch_shapes=[
            pltpu.VMEM((rows + CONV_PAD, CONV_DIM), F32),
            pltpu.VMEM((rows, CONV_DIM), F32),
            pltpu.VMEM((HEADS, HEAD_DIM, HEAD_DIM), F32),
        ],
        compiler_params=_params("parallel", "arbitrary"),
        name="gdn",
    )(qkv, z, gb, cprev, s0, w["conv_w"], w["gdn_norm_g"])


def _diff_attn_kernel(q_ref, qn_ref, k_ref, vt_ref, lam_ref, ng_ref, o_ref, s0_ref, s1_ref, m_ref, acc_ref, qst_ref,
                      *, tq, tqp, tk, nblk, nq, n_keys, q_pos0, lam_init):
    qi = pl.program_id(2)
    width = 2 * tqp
    pos_lo = q_pos0 + qi * tq
    k_lo = jnp.minimum((pos_lo // CHUNK + 1) * CHUNK, n_keys)
    k_hi = jnp.minimum(((pos_lo + tq - 1) // CHUNK + 1) * CHUNK, n_keys)
    n_full = k_lo // tk
    n_all = (k_hi + tk - 1) // tk

    def scores(jb, s_ref):
        jl = jnp.minimum(jb, nblk - 1)
        ks = pl.ds(pl.multiple_of(jl * tk, tk), tk)
        s_ref[:, 0:width] = _dot(k_ref[0, ks, :], qst_ref[...])

    def start(qx_ref):
        q = qx_ref[0]
        if tqp > tq:
            q = jnp.concatenate([q, jnp.zeros((tqp - tq, HEAD_DIM), q.dtype)], axis=0)
        lane = lax.broadcasted_iota(jnp.int32, q.shape, 1)
        zero = jnp.zeros_like(q)
        qs = jnp.concatenate([jnp.where(lane < DQK, q, zero), jnp.where(lane >= DQK, q, zero)], axis=0)
        qst_ref[...] = qs.T
        m_ref[...] = jnp.full(m_ref.shape, -jnp.inf, F32)
        acc_ref[...] = jnp.zeros(acc_ref.shape, F32)
        scores(0, s0_ref)

    def mask(jb, s_ref):
        @pl.when(jb >= n_full)
        def _():
            col = lax.broadcasted_iota(jnp.int32, (tk, width), 1)
            qpos = pos_lo + jnp.where(col >= tqp, col - tqp, col)
            kpos = jb * tk + lax.broadcasted_iota(jnp.int32, (tk, width), 0)
            vis = kpos < jnp.minimum((qpos // CHUNK + 1) * CHUNK, n_keys)
            s_ref[:, 0:width] = jnp.where(vis, s_ref[:, 0:width], -jnp.inf)

    def softmax_pv(jb, s_ref):
        st = s_ref[:, 0:width]
        m_old = m_ref[...]
        m_new = jnp.maximum(m_old, jnp.max(st, axis=0, keepdims=True))
        alpha = jnp.exp2(m_old - m_new)
        p = jnp.exp2(st - m_new)
        vt1 = jnp.concatenate([vt_ref[jnp.minimum(jb, nblk - 1)], jnp.ones((SUM_ROWS, tk), BF16)], axis=0)
        acc_ref[...] = alpha * acc_ref[...] + _dot(vt1, p.astype(BF16))
        m_ref[...] = m_new

    @pl.when(qi == 0)
    def _():
        start(q_ref)

    def full_pair(i, carry):
        j = 2 * i
        scores(j + 1, s1_ref)
        softmax_pv(j, s0_ref)
        scores(j + 2, s0_ref)
        softmax_pv(j + 1, s1_ref)
        return carry

    def edge_pair(i, carry):
        j = 2 * i
        mask(j, s0_ref)
        scores(j + 1, s1_ref)
        softmax_pv(j, s0_ref)

        @pl.when(j + 1 < n_all)
        def _():
            mask(j + 1, s1_ref)
            scores(j + 2, s0_ref)
            softmax_pv(j + 1, s1_ref)
        return carry

    lax.fori_loop(0, n_full // 2, full_pair, 0)
    lax.fori_loop(n_full // 2, (n_all + 1) // 2, edge_pair, 0)
    o = acc_ref[0:HEAD_DIM, :] * (1.0 / acc_ref[HEAD_DIM:HEAD_DIM + 1, :])
    if nq > 1:
        start(qn_ref)
    lf = lam_ref[...]
    lam = (jnp.exp(jnp.sum(lf[0:1] * lf[1:2], axis=-1, keepdims=True))
           - jnp.exp(jnp.sum(lf[2:3] * lf[3:4], axis=-1, keepdims=True)) + lam_init)
    od = o[:, :tqp] - lam * o[:, tqp:]
    ms = jnp.mean(od * od, axis=0, keepdims=True)
    on = od * lax.rsqrt(ms + NORM_EPS) * ng_ref[...] * (1.0 - lam_init)
    o_ref[0] = on.T[:tq].astype(BF16)


def _diff_attn(q, k, vt, w, n_keys, q_pos0, tq, tk):
    b, t, _ = q.shape
    tkp = k.shape[1]
    nblk = tkp // tk
    tqp = max(tq, LANES)
    lam_init = 0.8 - 0.6 * math.exp(-0.3 * 0)
    kern = functools.partial(_diff_attn_kernel, tq=tq, tqp=tqp, tk=tk, nblk=nblk, nq=t // tq, n_keys=n_keys,
                             q_pos0=q_pos0, lam_init=lam_init)
    return pl.pallas_call(
        kern,
        grid=(b, HEADS, t // tq),
        in_specs=[
            pl.BlockSpec((1, tq, HEAD_DIM), lambda i, h, j: (i, j, h)),
            pl.BlockSpec((1, tq, HEAD_DIM), lambda i, h, j: (i, jnp.minimum(j + 1, t // tq - 1), h)),
            pl.BlockSpec((1, tkp, HEAD_DIM), lambda i, h, j: (i, 0, h)),
            pl.BlockSpec((nblk, HEAD_DIM, tk), lambda i, h, j: (i, h, 0)),
            pl.BlockSpec((4, DQK), lambda i, h, j: (0, 0)),
            pl.BlockSpec((HEAD_DIM, 1), lambda i, h, j: (0, 0)),
        ],
        out_specs=pl.BlockSpec((1, tq, HEAD_DIM), lambda i, h, j: (i, j, h)),
        out_shape=jax.ShapeDtypeStruct((b, t, KEY_DIM), BF16),
        scratch_shapes=[
            pltpu.VMEM((tk, 2 * tqp + LANES), F32),
            pltpu.VMEM((tk, 2 * tqp + LANES), F32),
            pltpu.VMEM((1, 2 * tqp), F32),
            pltpu.VMEM((HEAD_DIM + SUM_ROWS, 2 * tqp), F32),
            pltpu.VMEM((HEAD_DIM, 2 * tqp), BF16),
        ],
        compiler_params=_params("arbitrary", "arbitrary", "arbitrary"),
        name="diff_attn",
    )(q, q, k, vt, w["diff_lambda"], w["diff_norm_g_col"])


def _kv_cache_kernel(ck_ref, cv_ref, kn_ref, vn_ref, k_ref, vt_ref, *, p_len, t_new, pad_rows):
    k_ref[0, 0:p_len, :] = ck_ref[0].reshape(p_len, KEY_DIM).astype(BF16)
    zeros = jnp.zeros((pad_rows - t_new, KEY_DIM), BF16)
    k_ref[0, p_len:p_len + pad_rows, :] = jnp.concatenate([kn_ref[0], zeros], axis=0)
    vt_ref[0, :, 0:p_len] = cv_ref[0].reshape(p_len, KEY_DIM).T.astype(BF16)
    v_new = jnp.concatenate([vn_ref[0], zeros], axis=0).astype(F32)
    vt_ref[0, :, p_len:p_len + pad_rows] = v_new.T.astype(BF16)


def _kv_cache(cache_k, cache_v, k_new, v_new):
    b, p_len = cache_k.shape[:2]
    t_new = k_new.shape[1]
    assert p_len % LANES == 0, p_len
    pad_rows = -(-t_new // LANES) * LANES
    rows = p_len + pad_rows
    kern = functools.partial(_kv_cache_kernel, p_len=p_len, t_new=t_new, pad_rows=pad_rows)
    return pl.pallas_call(
        kern,
        grid=(b,),
        in_specs=[
            pl.BlockSpec((1, p_len, HEADS, HEAD_DIM), lambda i: (i, 0, 0, 0)),
            pl.BlockSpec((1, p_len, HEADS, HEAD_DIM), lambda i: (i, 0, 0, 0)),
            pl.BlockSpec((1, t_new, KEY_DIM), lambda i: (i, 0, 0)),
            pl.BlockSpec((1, t_new, KEY_DIM), lambda i: (i, 0, 0)),
        ],
        out_specs=(pl.BlockSpec((1, rows, KEY_DIM), lambda i: (i, 0, 0)),
                   pl.BlockSpec((1, KEY_DIM, rows), lambda i: (i, 0, 0))),
        out_shape=(jax.ShapeDtypeStruct((b, rows, KEY_DIM), BF16),
                   jax.ShapeDtypeStruct((b, KEY_DIM, rows), BF16)),
        compiler_params=_params("parallel"),
        name="kv_cache",
    )(cache_k, cache_v, k_new, v_new)


def _post_mix_kernel(x_ref, og_ref, od_ref, mk_ref, mv_ref, wo_ref, g_ref, wq_ref, wmo_ref, o_ref):
    nb, tm, _ = x_ref.shape
    rows = lambda ref: ref[...].reshape(nb * tm, ref.shape[-1])
    x1 = rows(x_ref) + _dot(rows(og_ref), wo_ref[0:KEY_DIM, :]) + _dot(rows(od_ref), wo_ref[KEY_DIM:, :])
    h2 = _rms(x1, g_ref[...]).astype(BF16)
    qm = _dot(h2, wq_ref[...]).astype(BF16)
    oms = []
    for bi in range(nb):
        outs = []
        for hd in range(HEADS):
            sl = slice(hd * HEAD_DIM, (hd + 1) * HEAD_DIM)
            mk = mk_ref[bi, :, sl]
            mv = mv_ref[bi, :, sl]
            s = _dot_nt(qm[bi * tm:(bi + 1) * tm, sl], mk) * (HEAD_DIM ** -0.5)
            p = jnp.exp(s - jnp.max(s, axis=-1, keepdims=True))
            l = jnp.sum(p, axis=-1, keepdims=True)
            outs.append(_dot((p / l).astype(BF16), mv))
        oms.append(jnp.concatenate(outs, axis=1))
    om = jnp.concatenate(oms, axis=0).astype(BF16)
    o_ref[...] = (x1 + _dot(om, wmo_ref[...])).reshape(o_ref.shape)


def _post_mix(x, og, od, mk, mv, w, tm):
    b, t, _ = x.shape
    nb = _tile(b, max(1, ROW_TILE // tm)) if tm == t else 1
    blk = lambda i, j: (i, j, 0)
    fix = lambda i, j: (0, 0)
    mem = lambda i, j: (i, 0, 0)
    return pl.pallas_call(
        _post_mix_kernel,
        grid=(b // nb, t // tm),
        in_specs=[
            pl.BlockSpec((nb, tm, D_MODEL), blk),
            pl.BlockSpec((nb, tm, KEY_DIM), blk),
            pl.BlockSpec((nb, tm, KEY_DIM), blk),
            pl.BlockSpec((nb, N_MEM, KEY_DIM), mem),
            pl.BlockSpec((nb, N_MEM, KEY_DIM), mem),
            pl.BlockSpec((D_MODEL, D_MODEL), fix),
            pl.BlockSpec((1, D_MODEL), fix),
            pl.BlockSpec((D_MODEL, KEY_DIM), fix),
            pl.BlockSpec((KEY_DIM, D_MODEL), fix),
        ],
        out_specs=pl.BlockSpec((nb, tm, D_MODEL), blk),
        out_shape=jax.ShapeDtypeStruct((b, t, D_MODEL), F32),
        compiler_params=_params("parallel", "parallel"),
        name="post_mix",
    )(x, og, od, mk, mv, w["w_out"], w["norm_mem_g"], w["w_mq"], w["w_mo"])


def _mlp_kernel(x_ref, g_ref, wu_ref, wd_ref, fg_ref, o_ref, *, ff_blk):
    x = x_ref[...]
    hf = _rms(x, g_ref[...]).astype(BF16)
    acc = x
    for c0 in range(0, D_FF, ff_blk):
        u = jnp.maximum(_dot(hf, wu_ref[:, c0:c0 + ff_blk]), 0.0)
        acc = acc + _dot((u * u).astype(BF16), wd_ref[c0:c0 + ff_blk, :])
    o_ref[...] = _rms(acc, fg_ref[...])


def _mlp(x2d, w, tm):
    n = x2d.shape[0]
    row = lambda i: (i, 0)
    fix = lambda i: (0, 0)
    return pl.pallas_call(
        functools.partial(_mlp_kernel, ff_blk=D_MODEL),
        grid=(n // tm,),
        in_specs=[
            pl.BlockSpec((tm, D_MODEL), row),
            pl.BlockSpec((1, D_MODEL), fix),
            pl.BlockSpec((D_MODEL, D_FF), fix),
            pl.BlockSpec((D_FF, D_MODEL), fix),
            pl.BlockSpec((1, D_MODEL), fix),
        ],
        out_specs=pl.BlockSpec((tm, D_MODEL), row),
        out_shape=jax.ShapeDtypeStruct((n, D_MODEL), F32),
        compiler_params=_params("parallel"),
        name="mlp",
    )(x2d, w["norm_ffn_g"], w["w_up"], w["w_down"], w["final_norm_g"])


def _mem_kv_kernel(m_ref, g_ref, w_ref, k_ref, v_ref, kb_ref, vb_ref):
    mh = _rms(m_ref[...], g_ref[...]).astype(BF16)
    kv = _dot(mh, w_ref[...])
    kb_ref[...] = kv[:, :KEY_DIM].astype(BF16)
    vb_ref[...] = kv[:, KEY_DIM:].astype(BF16)
    k_ref[...] = kv[:, :KEY_DIM].reshape(k_ref.shape)
    v_ref[...] = kv[:, KEY_DIM:].reshape(v_ref.shape)


def _mem_kv(mem2d, w, tm):
    n = mem2d.shape[0]
    row = lambda i: (i, 0)
    fix = lambda i: (0, 0)
    return pl.pallas_call(
        _mem_kv_kernel,
        grid=(n // tm,),
        in_specs=[
            pl.BlockSpec((tm, D_MODEL), row),
            pl.BlockSpec((1, D_MODEL), fix),
            pl.BlockSpec((D_MODEL, 2 * KEY_DIM), fix),
        ],
        out_specs=(pl.BlockSpec((tm, HEADS, HEAD_DIM), lambda i: (i, 0, 0)),) * 2
        + (pl.BlockSpec((tm, KEY_DIM), row),) * 2,
        out_shape=(jax.ShapeDtypeStruct((n, HEADS, HEAD_DIM), F32),) * 2
        + (jax.ShapeDtypeStruct((n, KEY_DIM), BF16),) * 2,
        compiler_params=_params("parallel"),
        name="mem_kv",
    )(mem2d, w["mem_norm_g"], w["w_mkv"])


def _tile(n, pref):
    t = min(n, pref)
    assert n % t == 0, (n, t)
    return t


def _layer(x, pos0, conv_prev, s0, kv_prev, mem_k, mem_v, w):
    b, t, _ = x.shape
    n = b * t
    c = min(t, CHUNK)
    x2d = x.reshape(n, D_MODEL)
    tm = _tile(n, ROW_TILE)
    tab_rows = t if t % tm == 0 else n
    pos_rows = pos0 + (np.arange(tab_rows) % t)
    qkv, z, gb, qd, kf, kb, vf, vb, vt = _proj_in(x2d, pos_rows, tab_rows, w, tm)
    r3 = lambda a: a.reshape(b, t, -1)
    qkv3 = r3(qkv)
    og, s_new = _gdn(qkv3, r3(z), r3(gb), conv_prev, s0, w, nchunks=_tile(t // c, 8))
    conv_new = qkv3[:, t - (CONV_W - 1):, :]
    kb3 = r3(kb)
    if kv_prev is None:
        assert t % tm == 0, (t, tm)
        n_keys, tk = t, tm
    else:
        n_keys = kv_prev[0].shape[1] + t
        kb3, vt = _kv_cache(kv_prev[0], kv_prev[1], kb3, r3(vb))
        tk = kb3.shape[1]
    od = _diff_attn(r3(qd), kb3, vt, w, n_keys=n_keys, q_pos0=pos0, tq=_tile(t, ROW_TILE), tk=tk)
    x2 = _post_mix(x, og, od, mem_k, mem_v, w, tm=_tile(t, ROW_TILE))
    y = _mlp(x2.reshape(n, D_MODEL), w, tm).reshape(b, t, D_MODEL)
    h4 = lambda a: a.reshape(b, t, HEADS, HEAD_DIM)
    return y, s_new, conv_new, h4(kf), h4(vf)


def kernel(x_prompt, x_sample, mem_prompt, cache_diff_k, cache_diff_v, cache_mem_k, cache_mem_v, state_gdn, state_gdn_conv, norm_mix_g, w_in, gdn_conv_w, gdn_a_log, gdn_dt_bias, gdn_norm_g, diff_lambda, diff_norm_g, w_out, norm_mem_g, mem_norm_g, w_mq, w_mkv, w_mo, norm_ffn_g, w_up, w_down, final_norm_g):
    bp, tp, _ = x_prompt.shape
    bs, ts, _ = x_sample.shape
    p_len = cache_diff_k.shape[2]
    depth = w_in.shape[0]
    assert depth == 1
    l = 0
    wi = w_in[l]
    sp = [CONV_DIM, CONV_DIM + KEY_DIM, CONV_DIM + KEY_DIM + HEADS, CONV_DIM + KEY_DIM + 2 * HEADS]
    w_ab = jnp.concatenate([wi[:, sp[1]:sp[3]], jnp.zeros((D_MODEL, LANES - 2 * HEADS), F32)], axis=1)
    lanes_pad = lambda a: jnp.concatenate([a, jnp.zeros((LANES - a.shape[0],), F32)])[None, :]
    row = lambda a: a.reshape(1, -1)
    w = {
        "norm_mix_g": row(norm_mix_g[l]),
        "w_main": jnp.concatenate([wi[:, :sp[1]], wi[:, sp[3]:]], axis=1).astype(BF16),
        "w_ab": w_ab.astype(BF16),
        "a_log": lanes_pad(gdn_a_log[l]),
        "dt_bias": lanes_pad(gdn_dt_bias[l]),
        "conv_w": gdn_conv_w[l],
        "gdn_norm_g": row(gdn_norm_g[l]),
        "diff_lambda": diff_lambda[l],
        "diff_norm_g_col": diff_norm_g[l].reshape(HEAD_DIM, 1),
        "w_out": w_out[l].astype(BF16),
        "norm_mem_g": row(norm_mem_g[l]),
        "mem_norm_g": row(mem_norm_g[l]),
        "w_mq": w_mq[l].astype(BF16),
        "w_mkv": w_mkv[l].astype(BF16),
        "w_mo": w_mo[l].astype(BF16),
        "norm_ffn_g": row(norm_ffn_g[l]),
        "w_up": w_up[l].astype(BF16),
        "w_down": w_down[l].astype(BF16),
        "final_norm_g": row(final_norm_g),
    }
    n_mem = mem_prompt.shape[1]
    mk, mv, mkb, mvb = _mem_kv(mem_prompt.reshape(bp * n_mem, D_MODEL), w, _tile(bp * n_mem, ROW_TILE))
    mk = mk.reshape(bp, n_mem, HEADS, HEAD_DIM)
    mv = mv.reshape(bp, n_mem, HEADS, HEAD_DIM)
    mkb = mkb.reshape(bp, n_mem, KEY_DIM)
    mvb = mvb.reshape(bp, n_mem, KEY_DIM)

    zeros_conv = jnp.zeros((bp, CONV_W - 1, CONV_DIM), F32)
    zeros_state = jnp.zeros((bp, HEADS, HEAD_DIM, HEAD_DIM), F32)
    yp, sp_, cp, kp, vp = _layer(x_prompt, 0, zeros_conv, zeros_state, None, mkb, mvb, w)
    ys, ss, cs, ks_, vs = _layer(x_sample, p_len, state_gdn_conv[l], state_gdn[l],
                                 (cache_diff_k[l], cache_diff_v[l]),
                                 cache_mem_k[l].reshape(bs, n_mem, KEY_DIM).astype(BF16),
                                 cache_mem_v[l].reshape(bs, n_mem, KEY_DIM).astype(BF16), w)
    return (yp, ys, sp_[None], cp[None], kp[None], vp[None], mk[None], mv[None],
            ss[None], cs[None], ks_[None], vs[None])
```

```python
import functools
import math

import jax
import jax.numpy as jnp
import numpy as np
from jax import lax
from jax.experimental import pallas as pl
from jax.experimental.pallas import tpu as pltpu

D_MODEL = 1024
CHUNK = 64
HEADS = 4
HEAD_DIM = 128
KEY_DIM = HEADS * HEAD_DIM
CONV_DIM = 3 * KEY_DIM
CONV_W = 4
DQK = 64
ROT_DIM = 16
ROPE_THETA = 500000.0
N_MEM = 256
D_FF = 4 * D_MODEL
NORM_EPS = 1e-6
MAIN_COLS = CONV_DIM + 4 * KEY_DIM
LANES = 128
CONV_PAD = 8
SUM_ROWS = 16
VMEM_LIMIT = 56 * 1024 * 1024
ROW_TILE = 512

F32 = jnp.float32
BF16 = jnp.bfloat16
HIGHEST = lax.Precision.HIGHEST


def _dot(a, b, precision=None):
    return jnp.dot(a, b, preferred_element_type=F32, precision=precision)


def _dot_nt(a, b):
    return lax.dot_general(a, b, (((1,), (1,)), ((), ())), preferred_element_type=F32)


def _dot_tn(a, b):
    return lax.dot_general(a, b, (((0,), (0,)), ((), ())), preferred_element_type=F32)


def _rms(x, g):
    return x * lax.rsqrt(jnp.mean(x * x, axis=-1, keepdims=True) + NORM_EPS) * g


def _sigmoid(x):
    return 1.0 / (1.0 + jnp.exp(-x))


def _params(*sem):
    return pltpu.CompilerParams(dimension_semantics=sem, vmem_limit_bytes=VMEM_LIMIT)


def _proj_in_kernel(x_ref, g_ref, wm_ref, wab_ref, alog_ref, dtb_ref, cos_ref, sna_ref, snb_ref,
                    qkv_ref, z_ref, gb_ref, q_ref, kf_ref, kb_ref, vf_ref, vb_ref, vt_ref):
    x = x_ref[...]
    h = _rms(x, g_ref[...]).astype(BF16)
    main = _dot(h, wm_ref[...])
    ab = _dot(h, wab_ref[...])
    qkv_ref[...] = main[:, :CONV_DIM]
    z_ref[...] = main[:, CONV_DIM:CONV_DIM + KEY_DIM]
    xa = ab + dtb_ref[...]
    softplus = jnp.maximum(xa, 0.0) + jnp.log1p(jnp.exp(-jnp.abs(xa)))
    lane = lax.broadcasted_iota(jnp.int32, ab.shape, 1)
    gb_ref[...] = jnp.where(lane < HEADS, -jnp.exp(alog_ref[...]) * softplus, _sigmoid(ab))
    cos, sna, snb = cos_ref[...], sna_ref[...], snb_ref[...]
    q0 = CONV_DIM + KEY_DIM
    krs, vhs = [], []
    for hd in range(HEADS):
        sl = slice(hd * HEAD_DIM, (hd + 1) * HEAD_DIM)
        qh = main[:, q0 + hd * HEAD_DIM:q0 + (hd + 1) * HEAD_DIM]
        kh = main[:, q0 + KEY_DIM + hd * HEAD_DIM:q0 + KEY_DIM + (hd + 1) * HEAD_DIM]
        vh = main[:, q0 + 2 * KEY_DIM + hd * HEAD_DIM:q0 + 2 * KEY_DIM + (hd + 1) * HEAD_DIM]
        qr = qh * cos + pltpu.roll(qh, LANES - ROT_DIM // 2, 1) * sna + pltpu.roll(qh, ROT_DIM // 2, 1) * snb
        kr = kh * cos + pltpu.roll(kh, LANES - ROT_DIM // 2, 1) * sna + pltpu.roll(kh, ROT_DIM // 2, 1) * snb
        q_ref[:, sl] = (qr * (DQK ** -0.5 * math.log2(math.e))).astype(BF16)
        krs.append(kr)
        vhs.append(vh)
        kb_ref[:, sl] = kr.astype(BF16)
        vb_ref[:, sl] = vh.astype(BF16)
        vt_ref[0, sl, :] = vh.T.astype(BF16)
    kf_ref[...] = jnp.concatenate(krs, axis=1).reshape(kf_ref.shape)
    vf_ref[...] = jnp.concatenate(vhs, axis=1).reshape(vf_ref.shape)


def _rope_tables(pos):
    inv = ROPE_THETA ** (-np.arange(0, ROT_DIM, 2, dtype=np.float64) / ROT_DIM)
    ang = pos.astype(np.float64)[:, None] * inv[None, :]
    cos, sin = np.cos(ang), np.sin(ang)
    pad = np.zeros((pos.shape[0], DQK - ROT_DIM))
    cos64 = np.concatenate([cos, cos, pad + 1.0], axis=1)
    sna64 = np.concatenate([-sin, np.zeros_like(sin), pad], axis=1)
    snb64 = np.concatenate([np.zeros_like(sin), sin, pad], axis=1)
    return tuple(jnp.asarray(np.concatenate([a, a], axis=1), F32) for a in (cos64, sna64, snb64))


def _proj_in(x2d, pos_rows, tab_rows, w, tm):
    n = x2d.shape[0]
    nt = tab_rows // tm
    cos, sna, snb = _rope_tables(pos_rows)
    row = lambda i: (i, 0)
    fix = lambda i: (0, 0)
    tab = lambda i: (i % nt, 0)
    out_shape = (
        jax.ShapeDtypeStruct((n, CONV_DIM), F32),
        jax.ShapeDtypeStruct((n, KEY_DIM), F32),
        jax.ShapeDtypeStruct((n, LANES), F32),
        jax.ShapeDtypeStruct((n, KEY_DIM), BF16),
        jax.ShapeDtypeStruct((n, HEADS, HEAD_DIM), F32),
        jax.ShapeDtypeStruct((n, KEY_DIM), BF16),
        jax.ShapeDtypeStruct((n, HEADS, HEAD_DIM), F32),
        jax.ShapeDtypeStruct((n, KEY_DIM), BF16),
        jax.ShapeDtypeStruct((n // tm, KEY_DIM, tm), BF16),
    )

    def spec(sd):
        if len(sd.shape) == 2:
            return pl.BlockSpec((tm, sd.shape[1]), row)
        return pl.BlockSpec((sd.shape[0] * tm // n,) + sd.shape[1:], lambda i: (i, 0, 0))
    out_specs = tuple(spec(sd) for sd in out_shape)
    return pl.pallas_call(
        _proj_in_kernel,
        grid=(n // tm,),
        in_specs=[
            pl.BlockSpec((tm, D_MODEL), row),
            pl.BlockSpec((1, D_MODEL), fix),
            pl.BlockSpec((D_MODEL, MAIN_COLS), fix),
            pl.BlockSpec((D_MODEL, LANES), fix),
            pl.BlockSpec((1, LANES), fix),
            pl.BlockSpec((1, LANES), fix),
            pl.BlockSpec((tm, LANES), tab),
            pl.BlockSpec((tm, LANES), tab),
            pl.BlockSpec((tm, LANES), tab),
        ],
        out_specs=out_specs,
        out_shape=out_shape,
        compiler_params=_params("parallel"),
        name="proj_in",
    )(x2d, w["norm_mix_g"], w["w_main"], w["w_ab"], w["a_log"], w["dt_bias"], cos, sna, snb)


def _gdn_kernel(qkv_ref, z_ref, gb_ref, cprev_ref, s0_ref, cw_ref, ng_ref,
                o_ref, sout_ref, xp_ref, y_ref, s_ref, *, c, nchunks):
    j = pl.program_id(1)
    rows = c * nchunks
    hc = HEADS * c

    @pl.when(j == 0)
    def _():
        xp_ref[0:CONV_PAD, :] = cprev_ref[0]
        s_ref[...] = s0_ref[0]

    xp_ref[CONV_PAD:CONV_PAD + rows, :] = qkv_ref[0]
    cw = cw_ref[...]
    xp = xp_ref[...]
    y = xp[CONV_PAD:] * cw[CONV_W - 1:CONV_W, :]
    for i in range(CONV_W - 1):
        y = y + pltpu.roll(xp, CONV_W - 1 - i, 0)[CONV_PAD:] * cw[i:i + 1, :]
    xp_ref[0:CONV_PAD, :] = xp[rows:rows + CONV_PAD]
    y_ref[...] = y * _sigmoid(y)

    ri = lax.broadcasted_iota(jnp.int32, (hc, hc), 0)
    cj = lax.broadcasted_iota(jnp.int32, (hc, hc), 1)
    same = (ri // c) == (cj // c)
    tril_bd = same & (ri >= cj)
    strict_bd = same & (ri > cj)
    eye_bd = (ri == cj).astype(F32)
    r1 = lax.broadcasted_iota(jnp.int32, (c, c), 0)
    c1 = lax.broadcasted_iota(jnp.int32, (c, c), 1)
    tril_c = (r1 >= c1).astype(F32)
    triu_c = (r1 <= c1).astype(F32)
    ng = ng_ref[...]
    gb_all = gb_ref[0]
    if rows % LANES:
        gb_all = jnp.concatenate([gb_all, jnp.zeros((LANES - rows % LANES, LANES), F32)], axis=0)
    gbt_all = gb_all.T[0:2 * HEADS, :]

    def prep(ci):
        rs = slice(ci * c, (ci + 1) * c)
        gcol = gb_ref[0, rs, :]
        gc_col = _dot(tril_c, gcol, HIGHEST)
        gc_row = _dot(gbt_all[:, rs], triu_c, HIGHEST)
        ks, kbs, qs, vbs, kbgs, qgs, kds, dcol, grow, glast = [], [], [], [], [], [], [], [], [], []
        for hd in range(HEADS):
            qh = y_ref[rs, hd * HEAD_DIM:(hd + 1) * HEAD_DIM]
            kh = y_ref[rs, KEY_DIM + hd * HEAD_DIM:KEY_DIM + (hd + 1) * HEAD_DIM]
            vh = y_ref[rs, 2 * KEY_DIM + hd * HEAD_DIM:2 * KEY_DIM + (hd + 1) * HEAD_DIM]
            qh = qh * lax.rsqrt(jnp.sum(qh * qh, axis=-1, keepdims=True) + NORM_EPS) * (HEAD_DIM ** -0.5)
            kh = kh * lax.rsqrt(jnp.sum(kh * kh, axis=-1, keepdims=True) + NORM_EPS)
            g_h = gc_col[:, hd:hd + 1]
            beta_h = gcol[:, HEADS + hd:HEADS + hd + 1]
            g_last = gc_col[c - 1:c, hd:hd + 1]
            eg = jnp.exp(g_h)
            kb = kh * beta_h
            ks.append(kh)
            kbs.append(kb)
            qs.append(qh)
            vbs.append(vh * beta_h)
            kbgs.append(kb * eg)
            qgs.append((qh * eg).astype(BF16))
            kds.append((kh * jnp.exp(g_last - g_h)).astype(BF16))
            glast.append(jnp.exp(g_last))
            dcol.append(jnp.broadcast_to(g_h, (c, hc)))
            grow.append(gc_row[hd:hd + 1, :])
        k_s = jnp.concatenate(ks, axis=0).astype(BF16)
        q_s = jnp.concatenate(qs, axis=0).astype(BF16)
        dmat = jnp.concatenate(dcol, axis=0) - jnp.concatenate(grow, axis=1)
        decay = jnp.exp(jnp.where(tril_bd, dmat, -jnp.inf))
        a_kk = _dot_nt(jnp.concatenate(kbs, axis=0).astype(BF16), k_s)
        x = jnp.where(strict_bd, -(a_kk * decay), 0.0)
        rhs = jnp.concatenate([jnp.concatenate(vbs, axis=0), jnp.concatenate(kbgs, axis=0)], axis=1)
        a_qk = (_dot_nt(q_s, k_s) * decay).astype(BF16)
        return dict(x=x, rhs=rhs.astype(BF16), a_qk=a_qk, qg=qgs, kd=kds, glast=glast)

    pre = [prep(ci) for ci in range(nchunks)]
    ps = [eye_bd + d["x"] for d in pre]
    xps = [d["x"] for d in pre]
    for _ in range(int(math.log2(c)) - 1):
        xbs = [xp.astype(BF16) for xp in xps]
        xps = [_dot(xb, xb) for xb in xbs]
        ps = [p + _dot(p.astype(BF16), xp.astype(BF16)) for p, xp in zip(ps, xps)]
    uws = [_dot(p.astype(BF16), d["rhs"]) for p, d in zip(ps, pre)]

    state = [s_ref[hd] for hd in range(HEADS)]
    for ci in range(nchunks):
        rs = slice(ci * c, (ci + 1) * c)
        d, uw = pre[ci], uws[ci]
        vnews, qss = [], []
        for hd in range(HEADS):
            hs = slice(hd * c, (hd + 1) * c)
            wq = jnp.concatenate([uw[hs, HEAD_DIM:].astype(BF16), d["qg"][hd]], axis=0)
            ws = _dot(wq, state[hd].astype(BF16))
            v_new = uw[hs, :HEAD_DIM] - ws[:c]
            vnews.append(v_new)
            qss.append(ws[c:])
            state[hd] = state[hd] * d["glast"][hd] + _dot_tn(d["kd"][hd], v_new.astype(BF16))
        o_s = jnp.concatenate(qss, axis=0) + _dot(d["a_qk"], jnp.concatenate(vnews, axis=0).astype(BF16))
        for hd in range(HEADS):
            sl = slice(hd * HEAD_DIM, (hd + 1) * HEAD_DIM)
            zh = z_ref[0, rs, sl]
            o_h = _rms(o_s[hd * c:(hd + 1) * c], ng) * (zh * _sigmoid(zh))
            o_ref[0, rs, sl] = o_h.astype(BF16)
    for hd in range(HEADS):
        s_ref[hd] = state[hd]

    @pl.when(j == pl.num_programs(1) - 1)
    def _():
        sout_ref[0] = s_ref[...]


def _gdn(qkv, z, gb, conv_prev, s0, w, nchunks):
    b, t, _ = qkv.shape
    c = min(t, CHUNK)
    rows = c * nchunks
    cprev = jnp.concatenate([jnp.zeros((b, CONV_PAD - (CONV_W - 1), CONV_DIM), F32), conv_prev], axis=1)
    blk = lambda i, j: (i, j, 0)
    kern = functools.partial(_gdn_kernel, c=c, nchunks=nchunks)
    return pl.pallas_call(
        kern,
        grid=(b, t // rows),
        in_specs=[
            pl.BlockSpec((1, rows, CONV_DIM), blk),
            pl.BlockSpec((1, rows, KEY_DIM), blk),
            pl.BlockSpec((1, rows, LANES), blk),
            pl.BlockSpec((1, CONV_PAD, CONV_DIM), lambda i, j: (i, 0, 0)),
            pl.BlockSpec((1, HEADS, HEAD_DIM, HEAD_DIM), lambda i, j: (i, 0, 0, 0)),
            pl.BlockSpec((CONV_W, CONV_DIM), lambda i, j: (0, 0)),
            pl.BlockSpec((1, HEAD_DIM), lambda i, j: (0, 0)),
        ],
        out_specs=(
            pl.BlockSpec((1, rows, KEY_DIM), blk),
            pl.BlockSpec((1, HEADS, HEAD_DIM, HEAD_DIM), lambda i, j: (i, 0, 0, 0)),
        ),
        out_shape=(
            jax.ShapeDtypeStruct((b, t, KEY_DIM), BF16),
            jax.ShapeDtypeStruct((b, HEADS, HEAD_DIM, HEAD_DIM), F32),
        ),
        scratch_shapes=[
            pltpu.VMEM((rows + CONV_PAD, CONV_DIM), F32),
            pltpu.VMEM((rows, CONV_DIM), F32),
            pltpu.VMEM((HEADS, HEAD_DIM, HEAD_DIM), F32),
        ],
        compiler_params=_params("parallel", "arbitrary"),
        name="gdn",
    )(qkv, z, gb, cprev, s0, w["conv_w"], w["gdn_norm_g"])


def _diff_attn_kernel(q_ref, qn_ref, k_ref, vt_ref, lam_ref, ng_ref, o_ref, s0_ref, s1_ref, m_ref, acc_ref, qst_ref,
                      *, tq, tqp, tk, nblk, nq, hps, n_keys, q_pos0, lam_init):
    qi = pl.program_id(2)
    width = 2 * tqp
    heads = range(hps)
    lanes = lambda hd: slice(hd * HEAD_DIM, (hd + 1) * HEAD_DIM)
    pos_lo = q_pos0 + qi * tq
    k_lo = jnp.minimum((pos_lo // CHUNK + 1) * CHUNK, n_keys)
    k_hi = jnp.minimum(((pos_lo + tq - 1) // CHUNK + 1) * CHUNK, n_keys)
    n_full = k_lo // tk
    n_all = (k_hi + tk - 1) // tk

    def scores(jb, s_ref):
        jl = jnp.minimum(jb, nblk - 1)
        ks = pl.ds(pl.multiple_of(jl * tk, tk), tk)
        for hd in heads:
            s_ref[hd, :, 0:width] = _dot(k_ref[0, ks, lanes(hd)], qst_ref[hd])

    def start(qx_ref):
        for hd in heads:
            q = qx_ref[0, :, lanes(hd)]
            if tqp > tq:
                q = jnp.concatenate([q, jnp.zeros((tqp - tq, HEAD_DIM), q.dtype)], axis=0)
            lane = lax.broadcasted_iota(jnp.int32, q.shape, 1)
            zero = jnp.zeros_like(q)
            qs = jnp.concatenate([jnp.where(lane < DQK, q, zero), jnp.where(lane >= DQK, q, zero)], axis=0)
            qst_ref[hd] = qs.T
        m_ref[...] = jnp.full(m_ref.shape, -jnp.inf, F32)
        acc_ref[...] = jnp.zeros(acc_ref.shape, F32)
        scores(0, s0_ref)

    def mask(jb, s_ref):
        @pl.when(jb >= n_full)
        def _():
            col = lax.broadcasted_iota(jnp.int32, (tk, width), 1)
            qpos = pos_lo + jnp.where(col >= tqp, col - tqp, col)
            kpos = jb * tk + lax.broadcasted_iota(jnp.int32, (tk, width), 0)
            vis = kpos < jnp.minimum((qpos // CHUNK + 1) * CHUNK, n_keys)
            for hd in heads:
                s_ref[hd, :, 0:width] = jnp.where(vis, s_ref[hd, :, 0:width], -jnp.inf)

    def softmax_pv(jb, s_ref):
        ones = jnp.ones((SUM_ROWS, tk), BF16)
        for hd in heads:
            st = s_ref[hd, :, 0:width]
            m_old = m_ref[hd]
            m_new = jnp.maximum(m_old, jnp.max(st, axis=0, keepdims=True))
            alpha = jnp.exp2(m_old - m_new)
            p = jnp.exp2(st - m_new)
            vt1 = jnp.concatenate([vt_ref[jnp.minimum(jb, nblk - 1), lanes(hd), :], ones], axis=0)
            acc_ref[hd] = alpha * acc_ref[hd] + _dot(vt1, p.astype(BF16))
            m_ref[hd] = m_new

    @pl.when(qi == 0)
    def _():
        start(q_ref)

    def full_pair(i, carry):
        j = 2 * i
        scores(j + 1, s1_ref)
        softmax_pv(j, s0_ref)
        scores(j + 2, s0_ref)
        softmax_pv(j + 1, s1_ref)
        return carry

    def edge_pair(i, carry):
        j = 2 * i
        mask(j, s0_ref)
        scores(j + 1, s1_ref)
        softmax_pv(j, s0_ref)

        @pl.when(j + 1 < n_all)
        def _():
            mask(j + 1, s1_ref)
            scores(j + 2, s0_ref)
            softmax_pv(j + 1, s1_ref)
        return carry

    def full_quad(i, carry):
        full_pair(2 * i, carry)
        return full_pair(2 * i + 1, carry)

    lax.fori_loop(0, n_full // 4, full_quad, 0)
    lax.fori_loop(2 * (n_full // 4), n_full // 2, full_pair, 0)
    lax.fori_loop(n_full // 2, (n_all + 1) // 2, edge_pair, 0)
    outs = [acc_ref[hd, 0:HEAD_DIM, :] * (1.0 / acc_ref[hd, HEAD_DIM:HEAD_DIM + 1, :]) for hd in heads]
    if nq > 1:
        start(qn_ref)
    lf = lam_ref[...]
    lam = (jnp.exp(jnp.sum(lf[0:1] * lf[1:2], axis=-1, keepdims=True))
           - jnp.exp(jnp.sum(lf[2:3] * lf[3:4], axis=-1, keepdims=True)) + lam_init)
    for hd in heads:
        od = outs[hd][:, :tqp] - lam * outs[hd][:, tqp:]
        ms = jnp.mean(od * od, axis=0, keepdims=True)
        on = od * lax.rsqrt(ms + NORM_EPS) * ng_ref[...] * (1.0 - lam_init)
        o_ref[0, :, lanes(hd)] = on.T[:tq].astype(BF16)


def _diff_attn(q, k, vt, w, n_keys, q_pos0, tq, tk):
    b, t, _ = q.shape
    tkp = k.shape[1]
    nblk = tkp // tk
    nq = t // tq
    tqp = max(tq, LANES)
    hps = HEADS if nq == 1 else 1
    wide = hps * HEAD_DIM
    lam_init = 0.8 - 0.6 * math.exp(-0.3 * 0)
    kern = functools.partial(_diff_attn_kernel, tq=tq, tqp=tqp, tk=tk, nblk=nblk, nq=nq, hps=hps, n_keys=n_keys,
                             q_pos0=q_pos0, lam_init=lam_init)
    return pl.pallas_call(
        kern,
        grid=(b, HEADS // hps, nq),
        in_specs=[
            pl.BlockSpec((1, tq, wide), lambda i, h, j: (i, j, h)),
            pl.BlockSpec((1, tq, wide), lambda i, h, j: (i, jnp.minimum(j + 1, nq - 1), h)),
            pl.BlockSpec((1, tkp, wide), lambda i, h, j: (i, 0, h)),
            pl.BlockSpec((nblk, wide, tk), lambda i, h, j: (i, h, 0)),
            pl.BlockSpec((4, DQK), lambda i, h, j: (0, 0)),
            pl.BlockSpec((HEAD_DIM, 1), lambda i, h, j: (0, 0)),
        ],
        out_specs=pl.BlockSpec((1, tq, wide), lambda i, h, j: (i, j, h)),
        out_shape=jax.ShapeDtypeStruct((b, t, KEY_DIM), BF16),
        scratch_shapes=[
            pltpu.VMEM((hps, tk, 2 * tqp + LANES), F32),
            pltpu.VMEM((hps, tk, 2 * tqp + LANES), F32),
            pltpu.VMEM((hps, 1, 2 * tqp), F32),
            pltpu.VMEM((hps, HEAD_DIM + SUM_ROWS, 2 * tqp), F32),
            pltpu.VMEM((hps, HEAD_DIM, 2 * tqp), BF16),
        ],
        compiler_params=_params("arbitrary", "arbitrary", "arbitrary"),
        name="diff_attn",
    )(q, q, k, vt, w["diff_lambda"], w["diff_norm_g_col"])


def _kv_cache_kernel(ck_ref, cv_ref, kn_ref, vn_ref, k_ref, vt_ref, *, p_len, t_new, pad_rows):
    k_ref[0, 0:p_len, :] = ck_ref[0].reshape(p_len, KEY_DIM).astype(BF16)
    zeros = jnp.zeros((pad_rows - t_new, KEY_DIM), BF16)
    k_ref[0, p_len:p_len + pad_rows, :] = jnp.concatenate([kn_ref[0], zeros], axis=0)
    vt_ref[0, :, 0:p_len] = cv_ref[0].reshape(p_len, KEY_DIM).T.astype(BF16)
    v_new = jnp.concatenate([vn_ref[0], zeros], axis=0).astype(F32)
    vt_ref[0, :, p_len:p_len + pad_rows] = v_new.T.astype(BF16)


def _kv_cache(cache_k, cache_v, k_new, v_new):
    b, p_len = cache_k.shape[:2]
    t_new = k_new.shape[1]
    assert p_len % LANES == 0, p_len
    pad_rows = -(-t_new // LANES) * LANES
    rows = p_len + pad_rows
    kern = functools.partial(_kv_cache_kernel, p_len=p_len, t_new=t_new, pad_rows=pad_rows)
    return pl.pallas_call(
        kern,
        grid=(b,),
        in_specs=[
            pl.BlockSpec((1, p_len, HEADS, HEAD_DIM), lambda i: (i, 0, 0, 0)),
            pl.BlockSpec((1, p_len, HEADS, HEAD_DIM), lambda i: (i, 0, 0, 0)),
            pl.BlockSpec((1, t_new, KEY_DIM), lambda i: (i, 0, 0)),
            pl.BlockSpec((1, t_new, KEY_DIM), lambda i: (i, 0, 0)),
        ],
        out_specs=(pl.BlockSpec((1, rows, KEY_DIM), lambda i: (i, 0, 0)),
                   pl.BlockSpec((1, KEY_DIM, rows), lambda i: (i, 0, 0))),
        out_shape=(jax.ShapeDtypeStruct((b, rows, KEY_DIM), BF16),
                   jax.ShapeDtypeStruct((b, KEY_DIM, rows), BF16)),
        compiler_params=_params("parallel"),
        name="kv_cache",
    )(cache_k, cache_v, k_new, v_new)


def _post_mix_kernel(x_ref, og_ref, od_ref, mk_ref, mv_ref, wo_ref, g_ref, wq_ref, wmo_ref, o_ref):
    nb, tm, _ = x_ref.shape
    rows = lambda ref: ref[...].reshape(nb * tm, ref.shape[-1])
    x1 = rows(x_ref) + _dot(rows(og_ref), wo_ref[0:KEY_DIM, :]) + _dot(rows(od_ref), wo_ref[KEY_DIM:, :])
    h2 = _rms(x1, g_ref[...]).astype(BF16)
    qm = _dot(h2, wq_ref[...]).astype(BF16)
    oms = []
    for bi in range(nb):
        outs = []
        for hd in range(HEADS):
            sl = slice(hd * HEAD_DIM, (hd + 1) * HEAD_DIM)
            mk = mk_ref[bi, :, sl]
            mv = mv_ref[bi, :, sl]
            s = _dot_nt(qm[bi * tm:(bi + 1) * tm, sl], mk) * (HEAD_DIM ** -0.5)
            p = jnp.exp(s - jnp.max(s, axis=-1, keepdims=True))
            l = jnp.sum(p, axis=-1, keepdims=True)
            outs.append(_dot((p / l).astype(BF16), mv))
        oms.append(jnp.concatenate(outs, axis=1))
    om = jnp.concatenate(oms, axis=0).astype(BF16)
    o_ref[...] = (x1 + _dot(om, wmo_ref[...])).reshape(o_ref.shape)


def _post_mix(x, og, od, mk, mv, w, tm):
    b, t, _ = x.shape
    nb = _tile(b, max(1, ROW_TILE // tm)) if tm == t else 1
    blk = lambda i, j: (i, j, 0)
    fix = lambda i, j: (0, 0)
    mem = lambda i, j: (i, 0, 0)
    return pl.pallas_call(
        _post_mix_kernel,
        grid=(b // nb, t // tm),
        in_specs=[
            pl.BlockSpec((nb, tm, D_MODEL), blk),
            pl.BlockSpec((nb, tm, KEY_DIM), blk),
            pl.BlockSpec((nb, tm, KEY_DIM), blk),
            pl.BlockSpec((nb, N_MEM, KEY_DIM), mem),
            pl.BlockSpec((nb, N_MEM, KEY_DIM), mem),
            pl.BlockSpec((D_MODEL, D_MODEL), fix),
            pl.BlockSpec((1, D_MODEL), fix),
            pl.BlockSpec((D_MODEL, KEY_DIM), fix),
            pl.BlockSpec((KEY_DIM, D_MODEL), fix),
        ],
        out_specs=pl.BlockSpec((nb, tm, D_MODEL), blk),
        out_shape=jax.ShapeDtypeStruct((b, t, D_MODEL), F32),
        compiler_params=_params("parallel", "parallel"),
        name="post_mix",
    )(x, og, od, mk, mv, w["w_out"], w["norm_mem_g"], w["w_mq"], w["w_mo"])


def _mlp_kernel(x_ref, g_ref, wu_ref, wd_ref, fg_ref, o_ref, *, ff_blk):
    x = x_ref[...]
    hf = _rms(x, g_ref[...]).astype(BF16)
    acc = x
    for c0 in range(0, D_FF, ff_blk):
        u = jnp.maximum(_dot(hf, wu_ref[:, c0:c0 + ff_blk]), 0.0)
        acc = acc + _dot((u * u).astype(BF16), wd_ref[c0:c0 + ff_blk, :])
    o_ref[...] = _rms(acc, fg_ref[...])


def _mlp(x2d, w, tm):
    n = x2d.shape[0]
    row = lambda i: (i, 0)
    fix = lambda i: (0, 0)
    return pl.pallas_call(
        functools.partial(_mlp_kernel, ff_blk=D_MODEL),
        grid=(n // tm,),
        in_specs=[
            pl.BlockSpec((tm, D_MODEL), row),
            pl.BlockSpec((1, D_MODEL), fix),
            pl.BlockSpec((D_MODEL, D_FF), fix),
            pl.BlockSpec((D_FF, D_MODEL), fix),
            pl.BlockSpec((1, D_MODEL), fix),
        ],
        out_specs=pl.BlockSpec((tm, D_MODEL), row),
        out_shape=jax.ShapeDtypeStruct((n, D_MODEL), F32),
        compiler_params=_params("parallel"),
        name="mlp",
    )(x2d, w["norm_ffn_g"], w["w_up"], w["w_down"], w["final_norm_g"])


def _mem_kv_kernel(m_ref, g_ref, w_ref, k_ref, v_ref, kb_ref, vb_ref):
    mh = _rms(m_ref[...], g_ref[...]).astype(BF16)
    kv = _dot(mh, w_ref[...])
    kb_ref[...] = kv[:, :KEY_DIM].astype(BF16)
    vb_ref[...] = kv[:, KEY_DIM:].astype(BF16)
    k_ref[...] = kv[:, :KEY_DIM].reshape(k_ref.shape)
    v_ref[...] = kv[:, KEY_DIM:].reshape(v_ref.shape)


def _mem_kv(mem2d, w, tm):
    n = mem2d.shape[0]
    row = lambda i: (i, 0)
    fix = lambda i: (0, 0)
    return pl.pallas_call(
        _mem_kv_kernel,
        grid=(n // tm,),
        in_specs=[
            pl.BlockSpec((tm, D_MODEL), row),
            pl.BlockSpec((1, D_MODEL), fix),
            pl.BlockSpec((D_MODEL, 2 * KEY_DIM), fix),
        ],
        out_specs=(pl.BlockSpec((tm, HEADS, HEAD_DIM), lambda i: (i, 0, 0)),) * 2
        + (pl.BlockSpec((tm, KEY_DIM), row),) * 2,
        out_shape=(jax.ShapeDtypeStruct((n, HEADS, HEAD_DIM), F32),) * 2
        + (jax.ShapeDtypeStruct((n, KEY_DIM), BF16),) * 2,
        compiler_params=_params("parallel"),
        name="mem_kv",
    )(mem2d, w["mem_norm_g"], w["w_mkv"])


def _tile(n, pref):
    t = min(n, pref)
    assert n % t == 0, (n, t)
    return t


def _layer(x, pos0, conv_prev, s0, kv_prev, mem_k, mem_v, w):
    b, t, _ = x.shape
    n = b * t
    c = min(t, CHUNK)
    x2d = x.reshape(n, D_MODEL)
    tm = _tile(n, ROW_TILE)
    tab_rows = t if t % tm == 0 else n
    pos_rows = pos0 + (np.arange(tab_rows) % t)
    qkv, z, gb, qd, kf, kb, vf, vb, vt = _proj_in(x2d, pos_rows, tab_rows, w, tm)
    r3 = lambda a: a.reshape(b, t, -1)
    qkv3 = r3(qkv)
    og, s_new = _gdn(qkv3, r3(z), r3(gb), conv_prev, s0, w, nchunks=_tile(t // c, 8))
    conv_new = qkv3[:, t - (CONV_W - 1):, :]
    kb3 = r3(kb)
    if kv_prev is None:
        assert t % tm == 0, (t, tm)
        n_keys, tk = t, tm
    else:
        n_keys = kv_prev[0].shape[1] + t
        kb3, vt = _kv_cache(kv_prev[0], kv_prev[1], kb3, r3(vb))
        tk = kb3.shape[1]
    od = _diff_attn(r3(qd), kb3, vt, w, n_keys=n_keys, q_pos0=pos0, tq=_tile(t, ROW_TILE), tk=tk)
    x2 = _post_mix(x, og, od, mem_k, mem_v, w, tm=_tile(t, ROW_TILE))
    y = _mlp(x2.reshape(n, D_MODEL), w, tm).reshape(b, t, D_MODEL)
    h4 = lambda a: a.reshape(b, t, HEADS, HEAD_DIM)
    return y, s_new, conv_new, h4(kf), h4(vf)


def kernel(x_prompt, x_sample, mem_prompt, cache_diff_k, cache_diff_v, cache_mem_k, cache_mem_v, state_gdn, state_gdn_conv, norm_mix_g, w_in, gdn_conv_w, gdn_a_log, gdn_dt_bias, gdn_norm_g, diff_lambda, diff_norm_g, w_out, norm_mem_g, mem_norm_g, w_mq, w_mkv, w_mo, norm_ffn_g, w_up, w_down, final_norm_g):
    bp, tp, _ = x_prompt.shape
    bs, ts, _ = x_sample.shape
    p_len = cache_diff_k.shape[2]
    depth = w_in.shape[0]
    assert depth == 1
    l = 0
    wi = w_in[l]
    sp = [CONV_DIM, CONV_DIM + KEY_DIM, CONV_DIM + KEY_DIM + HEADS, CONV_DIM + KEY_DIM + 2 * HEADS]
    w_ab = jnp.concatenate([wi[:, sp[1]:sp[3]], jnp.zeros((D_MODEL, LANES - 2 * HEADS), F32)], axis=1)
    lanes_pad = lambda a: jnp.concatenate([a, jnp.zeros((LANES - a.shape[0],), F32)])[None, :]
    row = lambda a: a.reshape(1, -1)
    w = {
        "norm_mix_g": row(norm_mix_g[l]),
        "w_main": jnp.concatenate([wi[:, :sp[1]], wi[:, sp[3]:]], axis=1).astype(BF16),
        "w_ab": w_ab.astype(BF16),
        "a_log": lanes_pad(gdn_a_log[l]),
        "dt_bias": lanes_pad(gdn_dt_bias[l]),
        "conv_w": gdn_conv_w[l],
        "gdn_norm_g": row(gdn_norm_g[l]),
        "diff_lambda": diff_lambda[l],
        "diff_norm_g_col": diff_norm_g[l].reshape(HEAD_DIM, 1),
        "w_out": w_out[l].astype(BF16),
        "norm_mem_g": row(norm_mem_g[l]),
        "mem_norm_g": row(mem_norm_g[l]),
        "w_mq": w_mq[l].astype(BF16),
        "w_mkv": w_mkv[l].astype(BF16),
        "w_mo": w_mo[l].astype(BF16),
        "norm_ffn_g": row(norm_ffn_g[l]),
        "w_up": w_up[l].astype(BF16),
        "w_down": w_down[l].astype(BF16),
        "final_norm_g": row(final_norm_g),
    }
    n_mem = mem_prompt.shape[1]
    mk, mv, mkb, mvb = _mem_kv(mem_prompt.reshape(bp * n_mem, D_MODEL), w, _tile(bp * n_mem, ROW_TILE))
    mk = mk.reshape(bp, n_mem, HEADS, HEAD_DIM)
    mv = mv.reshape(bp, n_mem, HEADS, HEAD_DIM)
    mkb = mkb.reshape(bp, n_mem, KEY_DIM)
    mvb = mvb.reshape(bp, n_mem, KEY_DIM)

    zeros_conv = jnp.zeros((bp, CONV_W - 1, CONV_DIM), F32)
    zeros_state = jnp.zeros((bp, HEADS, HEAD_DIM, HEAD_DIM), F32)
    yp, sp_, cp, kp, vp = _layer(x_prompt, 0, zeros_conv, zeros_state, None, mkb, mvb, w)
    ys, ss, cs, ks_, vs = _layer(x_sample, p_len, state_gdn_conv[l], state_gdn[l],
                                 (cache_diff_k[l], cache_diff_v[l]),
                                 cache_mem_k[l].reshape(bs, n_mem, KEY_DIM).astype(BF16),
                                 cache_mem_v[l].reshape(bs, n_mem, KEY_DIM).astype(BF16), w)
    return (yp, ys, sp_[None], cp[None], kp[None], vp[None], mk[None], mv[None],
            ss[None], cs[None], ks_[None], vs[None])
```

```python
import functools
import math

import jax
import jax.numpy as jnp
import numpy as np
from jax import lax
from jax.experimental import pallas as pl
from jax.experimental.pallas import tpu as pltpu

D_MODEL = 1024
CHUNK = 64
HEADS = 4
HEAD_DIM = 128
KEY_DIM = HEADS * HEAD_DIM
CONV_DIM = 3 * KEY_DIM
CONV_W = 4
DQK = 64
ROT_DIM = 16
ROPE_THETA = 500000.0
N_MEM = 256
D_FF = 4 * D_MODEL
NORM_EPS = 1e-6
MAIN_COLS = CONV_DIM + 4 * KEY_DIM
LANES = 128
CONV_PAD = 8
SUM_ROWS = 16
VMEM_LIMIT = 56 * 1024 * 1024
ROW_TILE = 512

F32 = jnp.float32
BF16 = jnp.bfloat16
HIGHEST = lax.Precision.HIGHEST


def _dot(a, b, precision=None):
    return jnp.dot(a, b, preferred_element_type=F32, precision=precision)


def _dot_nt(a, b):
    return lax.dot_general(a, b, (((1,), (1,)), ((), ())), preferred_element_type=F32)


def _dot_tn(a, b):
    return lax.dot_general(a, b, (((0,), (0,)), ((), ())), preferred_element_type=F32)


def _rms(x, g):
    return x * lax.rsqrt(jnp.mean(x * x, axis=-1, keepdims=True) + NORM_EPS) * g


def _sigmoid(x):
    return 1.0 / (1.0 + jnp.exp(-x))


def _params(*sem):
    return pltpu.CompilerParams(dimension_semantics=sem, vmem_limit_bytes=VMEM_LIMIT)


def _proj_in_kernel(x_ref, g_ref, wm_ref, wab_ref, alog_ref, dtb_ref, cos_ref, sna_ref, snb_ref,
                    qkv_ref, z_ref, gb_ref, q_ref, kf_ref, kb_ref, vf_ref, vb_ref, vt_ref):
    x = x_ref[...]
    h = _rms(x, g_ref[...]).astype(BF16)
    main = _dot(h, wm_ref[...])
    ab = _dot(h, wab_ref[...])
    qkv_ref[...] = main[:, :CONV_DIM]
    z_ref[...] = main[:, CONV_DIM:CONV_DIM + KEY_DIM]
    xa = ab + dtb_ref[...]
    softplus = jnp.maximum(xa, 0.0) + jnp.log1p(jnp.exp(-jnp.abs(xa)))
    lane = lax.broadcasted_iota(jnp.int32, ab.shape, 1)
    gb_ref[...] = jnp.where(lane < HEADS, -jnp.exp(alog_ref[...]) * softplus, _sigmoid(ab))
    cos, sna, snb = cos_ref[...], sna_ref[...], snb_ref[...]
    q0 = CONV_DIM + KEY_DIM
    krs, vhs = [], []
    for hd in range(HEADS):
        sl = slice(hd * HEAD_DIM, (hd + 1) * HEAD_DIM)
        qh = main[:, q0 + hd * HEAD_DIM:q0 + (hd + 1) * HEAD_DIM]
        kh = main[:, q0 + KEY_DIM + hd * HEAD_DIM:q0 + KEY_DIM + (hd + 1) * HEAD_DIM]
        vh = main[:, q0 + 2 * KEY_DIM + hd * HEAD_DIM:q0 + 2 * KEY_DIM + (hd + 1) * HEAD_DIM]
        qr = qh * cos + pltpu.roll(qh, LANES - ROT_DIM // 2, 1) * sna + pltpu.roll(qh, ROT_DIM // 2, 1) * snb
        kr = kh * cos + pltpu.roll(kh, LANES - ROT_DIM // 2, 1) * sna + pltpu.roll(kh, ROT_DIM // 2, 1) * snb
        q_ref[:, sl] = (qr * (DQK ** -0.5 * math.log2(math.e))).astype(BF16)
        krs.append(kr)
        vhs.append(vh)
        kb_ref[:, sl] = kr.astype(BF16)
        vb_ref[:, sl] = vh.astype(BF16)
        vt_ref[0, sl, :] = vh.T.astype(BF16)
    kf_ref[...] = jnp.concatenate(krs, axis=1).reshape(kf_ref.shape)
    vf_ref[...] = jnp.concatenate(vhs, axis=1).reshape(vf_ref.shape)


def _rope_tables(pos):
    inv = ROPE_THETA ** (-np.arange(0, ROT_DIM, 2, dtype=np.float64) / ROT_DIM)
    ang = pos.astype(np.float64)[:, None] * inv[None, :]
    cos, sin = np.cos(ang), np.sin(ang)
    pad = np.zeros((pos.shape[0], DQK - ROT_DIM))
    cos64 = np.concatenate([cos, cos, pad + 1.0], axis=1)
    sna64 = np.concatenate([-sin, np.zeros_like(sin), pad], axis=1)
    snb64 = np.concatenate([np.zeros_like(sin), sin, pad], axis=1)
    return tuple(jnp.asarray(np.concatenate([a, a], axis=1), F32) for a in (cos64, sna64, snb64))


def _proj_in(x2d, pos_rows, tab_rows, w, tm):
    n = x2d.shape[0]
    nt = tab_rows // tm
    cos, sna, snb = _rope_tables(pos_rows)
    row = lambda i: (i, 0)
    fix = lambda i: (0, 0)
    tab = lambda i: (i % nt, 0)
    out_shape = (
        jax.ShapeDtypeStruct((n, CONV_DIM), F32),
        jax.ShapeDtypeStruct((n, KEY_DIM), F32),
        jax.ShapeDtypeStruct((n, LANES), F32),
        jax.ShapeDtypeStruct((n, KEY_DIM), BF16),
        jax.ShapeDtypeStruct((n, HEADS, HEAD_DIM), F32),
        jax.ShapeDtypeStruct((n, KEY_DIM), BF16),
        jax.ShapeDtypeStruct((n, HEADS, HEAD_DIM), F32),
        jax.ShapeDtypeStruct((n, KEY_DIM), BF16),
        jax.ShapeDtypeStruct((n // tm, KEY_DIM, tm), BF16),
    )

    def spec(sd):
        if len(sd.shape) == 2:
            return pl.BlockSpec((tm, sd.shape[1]), row)
        return pl.BlockSpec((sd.shape[0] * tm // n,) + sd.shape[1:], lambda i: (i, 0, 0))
    out_specs = tuple(spec(sd) for sd in out_shape)
    return pl.pallas_call(
        _proj_in_kernel,
        grid=(n // tm,),
        in_specs=[
            pl.BlockSpec((tm, D_MODEL), row),
            pl.BlockSpec((1, D_MODEL), fix),
            pl.BlockSpec((D_MODEL, MAIN_COLS), fix),
            pl.BlockSpec((D_MODEL, LANES), fix),
            pl.BlockSpec((1, LANES), fix),
            pl.BlockSpec((1, LANES), fix),
            pl.BlockSpec((tm, LANES), tab),
            pl.BlockSpec((tm, LANES), tab),
            pl.BlockSpec((tm, LANES), tab),
        ],
        out_specs=out_specs,
        out_shape=out_shape,
        compiler_params=_params("parallel"),
        name="proj_in",
    )(x2d, w["norm_mix_g"], w["w_main"], w["w_ab"], w["a_log"], w["dt_bias"], cos, sna, snb)


def _gdn_kernel(qkv_ref, z_ref, gb_ref, cprev_ref, s0_ref, cw_ref, ng_ref,
                o_ref, sout_ref, xp_ref, y_ref, s_ref, *, c, nchunks):
    j = pl.program_id(1)
    rows = c * nchunks
    hc = HEADS * c

    @pl.when(j == 0)
    def _():
        xp_ref[0:CONV_PAD, :] = cprev_ref[0]
        s_ref[...] = s0_ref[0]

    xp_ref[CONV_PAD:CONV_PAD + rows, :] = qkv_ref[0]
    cw = cw_ref[...]
    xp = xp_ref[...]
    y = xp[CONV_PAD:] * cw[CONV_W - 1:CONV_W, :]
    for i in range(CONV_W - 1):
        y = y + pltpu.roll(xp, CONV_W - 1 - i, 0)[CONV_PAD:] * cw[i:i + 1, :]
    xp_ref[0:CONV_PAD, :] = xp[rows:rows + CONV_PAD]
    y_ref[...] = y * _sigmoid(y)

    ri = lax.broadcasted_iota(jnp.int32, (hc, hc), 0)
    cj = lax.broadcasted_iota(jnp.int32, (hc, hc), 1)
    same = (ri // c) == (cj // c)
    tril_bd = same & (ri >= cj)
    strict_bd = same & (ri > cj)
    eye_bd = (ri == cj).astype(F32)
    r1 = lax.broadcasted_iota(jnp.int32, (c, c), 0)
    c1 = lax.broadcasted_iota(jnp.int32, (c, c), 1)
    tril_c = (r1 >= c1).astype(F32)
    triu_c = (r1 <= c1).astype(F32)
    ng = ng_ref[...]
    gb_all = gb_ref[0]
    if rows % LANES:
        gb_all = jnp.concatenate([gb_all, jnp.zeros((LANES - rows % LANES, LANES), F32)], axis=0)
    gbt_all = gb_all.T[0:2 * HEADS, :]

    def prep(ci):
        rs = slice(ci * c, (ci + 1) * c)
        gcol = gb_ref[0, rs, :]
        gc_col = _dot(tril_c, gcol, HIGHEST)
        gc_row = _dot(gbt_all[:, rs], triu_c, HIGHEST)
        ks, kbs, qs, vbs, kbgs, qgs, kds, dcol, grow, glast = [], [], [], [], [], [], [], [], [], []
        for hd in range(HEADS):
            qh = y_ref[rs, hd * HEAD_DIM:(hd + 1) * HEAD_DIM]
            kh = y_ref[rs, KEY_DIM + hd * HEAD_DIM:KEY_DIM + (hd + 1) * HEAD_DIM]
            vh = y_ref[rs, 2 * KEY_DIM + hd * HEAD_DIM:2 * KEY_DIM + (hd + 1) * HEAD_DIM]
            qh = qh * lax.rsqrt(jnp.sum(qh * qh, axis=-1, keepdims=True) + NORM_EPS) * (HEAD_DIM ** -0.5)
            kh = kh * lax.rsqrt(jnp.sum(kh * kh, axis=-1, keepdims=True) + NORM_EPS)
            g_h = gc_col[:, hd:hd + 1]
            beta_h = gcol[:, HEADS + hd:HEADS + hd + 1]
            g_last = gc_col[c - 1:c, hd:hd + 1]
            eg = jnp.exp(g_h)
            kb = kh * beta_h
            ks.append(kh)
            kbs.append(kb)
            qs.append(qh)
            vbs.append(vh * beta_h)
            kbgs.append(kb * eg)
            qgs.append((qh * eg).astype(BF16))
            kds.append((kh * jnp.exp(g_last - g_h)).astype(BF16))
            glast.append(jnp.exp(g_last))
            dcol.append(jnp.broadcast_to(g_h, (c, hc)))
            grow.append(gc_row[hd:hd + 1, :])
        k_s = jnp.concatenate(ks, axis=0).astype(BF16)
        q_s = jnp.concatenate(qs, axis=0).astype(BF16)
        dmat = jnp.concatenate(dcol, axis=0) - jnp.concatenate(grow, axis=1)
        decay = jnp.exp(jnp.where(tril_bd, dmat, -jnp.inf))
        a_kk = _dot_nt(jnp.concatenate(kbs, axis=0).astype(BF16), k_s)
        x = jnp.where(strict_bd, -(a_kk * decay), 0.0)
        rhs = jnp.concatenate([jnp.concatenate(vbs, axis=0), jnp.concatenate(kbgs, axis=0)], axis=1)
        a_qk = (_dot_nt(q_s, k_s) * decay).astype(BF16)
        return dict(x=x, rhs=rhs.astype(BF16), a_qk=a_qk, qg=qgs, kd=kds, glast=glast)

    pre = [prep(ci) for ci in range(nchunks)]
    ps = [eye_bd + d["x"] for d in pre]
    xps = [d["x"] for d in pre]
    for _ in range(int(math.log2(c)) - 1):
        xbs = [xp.astype(BF16) for xp in xps]
        xps = [_dot(xb, xb) for xb in xbs]
        ps = [p + _dot(p.astype(BF16), xp.astype(BF16)) for p, xp in zip(ps, xps)]
    uws = [_dot(p.astype(BF16), d["rhs"]) for p, d in zip(ps, pre)]

    state = [s_ref[hd] for hd in range(HEADS)]
    for ci in range(nchunks):
        rs = slice(ci * c, (ci + 1) * c)
        d, uw = pre[ci], uws[ci]
        vnews, qss = [], []
        for hd in range(HEADS):
            hs = slice(hd * c, (hd + 1) * c)
            wq = jnp.concatenate([uw[hs, HEAD_DIM:].astype(BF16), d["qg"][hd]], axis=0)
            ws = _dot(wq, state[hd].astype(BF16))
            v_new = uw[hs, :HEAD_DIM] - ws[:c]
            vnews.append(v_new)
            qss.append(ws[c:])
            state[hd] = state[hd] * d["glast"][hd] + _dot_tn(d["kd"][hd], v_new.astype(BF16))
        o_s = jnp.concatenate(qss, axis=0) + _dot(d["a_qk"], jnp.concatenate(vnews, axis=0).astype(BF16))
        for hd in range(HEADS):
            sl = slice(hd * HEAD_DIM, (hd + 1) * HEAD_DIM)
            zh = z_ref[0, rs, sl]
            o_h = _rms(o_s[hd * c:(hd + 1) * c], ng) * (zh * _sigmoid(zh))
            o_ref[0, rs, sl] = o_h.astype(BF16)
    for hd in range(HEADS):
        s_ref[hd] = state[hd]

    @pl.when(j == pl.num_programs(1) - 1)
    def _():
        sout_ref[0] = s_ref[...]


def _gdn(qkv, z, gb, conv_prev, s0, w, nchunks):
    b, t, _ = qkv.shape
    c = min(t, CHUNK)
    rows = c * nchunks
    cprev = jnp.concatenate([jnp.zeros((b, CONV_PAD - (CONV_W - 1), CONV_DIM), F32), conv_prev], axis=1)
    blk = lambda i, j: (i, j, 0)
    kern = functools.partial(_gdn_kernel, c=c, nchunks=nchunks)
    return pl.pallas_call(
        kern,
        grid=(b, t // rows),
        in_specs=[
            pl.BlockSpec((1, rows, CONV_DIM), blk),
            pl.BlockSpec((1, rows, KEY_DIM), blk),
            pl.BlockSpec((1, rows, LANES), blk),
            pl.BlockSpec((1, CONV_PAD, CONV_DIM), lambda i, j: (i, 0, 0)),
            pl.BlockSpec((1, HEADS, HEAD_DIM, HEAD_DIM), lambda i, j: (i, 0, 0, 0)),
            pl.BlockSpec((CONV_W, CONV_DIM), lambda i, j: (0, 0)),
            pl.BlockSpec((1, HEAD_DIM), lambda i, j: (0, 0)),
        ],
        out_specs=(
            pl.BlockSpec((1, rows, KEY_DIM), blk),
            pl.BlockSpec((1, HEADS, HEAD_DIM, HEAD_DIM), lambda i, j: (i, 0, 0, 0)),
        ),
        out_shape=(
            jax.ShapeDtypeStruct((b, t, KEY_DIM), BF16),
            jax.ShapeDtypeStruct((b, HEADS, HEAD_DIM, HEAD_DIM), F32),
        ),
        scratch_shapes=[
            pltpu.VMEM((rows + CONV_PAD, CONV_DIM), F32),
            pltpu.VMEM((rows, CONV_DIM), F32),
            pltpu.VMEM((HEADS, HEAD_DIM, HEAD_DIM), F32),
        ],
        compiler_params=_params("parallel", "arbitrary"),
        name="gdn",
    )(qkv, z, gb, cprev, s0, w["conv_w"], w["gdn_norm_g"])


def _diff_attn_kernel(q_ref, qn_ref, k_ref, vt_ref, lam_ref, ng_ref, o_ref,
                      sz_ref, s0_ref, s1_ref, m_ref, acc_ref, qst_ref,
                      *, tq, tqp, tk, nblk, nq, hps, n_keys, q_pos0, lam_init):
    qi = pl.program_id(2)
    width = 2 * tqp
    heads = range(hps)
    lanes = lambda hd: slice(hd * HEAD_DIM, (hd + 1) * HEAD_DIM)
    pos_lo = q_pos0 + qi * tq
    k_lo = jnp.minimum((pos_lo // CHUNK + 1) * CHUNK, n_keys)
    k_hi = jnp.minimum(((pos_lo + tq - 1) // CHUNK + 1) * CHUNK, n_keys)
    n_full = k_lo // tk
    last = (k_hi + tk - 1) // tk - 1

    def scores(jb, s_ref):
        jl = jnp.minimum(jb, nblk - 1)
        ks = pl.ds(pl.multiple_of(jl * tk, tk), tk)
        for hd in heads:
            s_ref[hd, :, 0:width] = _dot(k_ref[0, ks, lanes(hd)], qst_ref[hd])

    def start(qx_ref):
        for hd in heads:
            q = qx_ref[0, :, lanes(hd)]
            if tqp > tq:
                q = jnp.concatenate([q, jnp.zeros((tqp - tq, HEAD_DIM), q.dtype)], axis=0)
            lane = lax.broadcasted_iota(jnp.int32, q.shape, 1)
            zero = jnp.zeros_like(q)
            qs = jnp.concatenate([jnp.where(lane < DQK, q, zero), jnp.where(lane >= DQK, q, zero)], axis=0)
            qst_ref[hd] = qs.T
        m_ref[...] = jnp.full(m_ref.shape, -jnp.inf, F32)
        acc_ref[...] = jnp.zeros(acc_ref.shape, F32)
        scores(0, sz_ref)

    def apply_mask(jb, s_ref):
        col = lax.broadcasted_iota(jnp.int32, (tk, width), 1)
        qpos = pos_lo + jnp.where(col >= tqp, col - tqp, col)
        kpos = jb * tk + lax.broadcasted_iota(jnp.int32, (tk, width), 0)
        vis = kpos < jnp.minimum((qpos // CHUNK + 1) * CHUNK, n_keys)
        for hd in heads:
            s_ref[hd, :, 0:width] = jnp.where(vis, s_ref[hd, :, 0:width], -jnp.inf)

    def mask(jb, s_ref):
        @pl.when(jb >= n_full)
        def _():
            apply_mask(jb, s_ref)

    def softmax_pv(jb, s_ref):
        ones = jnp.ones((SUM_ROWS, tk), BF16)
        for hd in heads:
            st = s_ref[hd, :, 0:width]
            m_old = m_ref[hd]
            m_new = jnp.maximum(m_old, jnp.max(st, axis=0, keepdims=True))
            alpha = jnp.exp2(m_old - m_new)
            p = jnp.exp2(st - m_new)
            vt1 = jnp.concatenate([vt_ref[jnp.minimum(jb, nblk - 1), lanes(hd), :], ones], axis=0)
            acc_ref[hd] = alpha * acc_ref[hd] + _dot(vt1, p.astype(BF16))
            m_ref[hd] = m_new

    @pl.when(qi == 0)
    def _():
        start(q_ref)

    @pl.when(last >= 1)
    def _():
        mask(0, sz_ref)
        scores(1, s1_ref)
        softmax_pv(0, sz_ref)

    def full_pair(i, carry):
        j = 1 + 2 * i
        scores(j + 1, s0_ref)
        softmax_pv(j, s1_ref)
        scores(j + 2, s1_ref)
        softmax_pv(j + 1, s0_ref)
        return carry

    def full_quad(i, carry):
        full_pair(2 * i, carry)
        return full_pair(2 * i + 1, carry)

    def full_octo(i, carry):
        full_quad(2 * i, carry)
        return full_quad(2 * i + 1, carry)

    def edge_pair(i, carry):
        j = 1 + 2 * i
        mask(j, s1_ref)
        scores(j + 1, s0_ref)
        softmax_pv(j, s1_ref)

        @pl.when(j + 1 < last)
        def _():
            mask(j + 1, s0_ref)
            scores(j + 2, s1_ref)
            softmax_pv(j + 1, s0_ref)
        return carry

    n_free = jnp.maximum(jnp.minimum(n_full, last) - 1, 0)
    lax.fori_loop(0, n_free // 8, full_octo, 0)
    lax.fori_loop(2 * (n_free // 8), n_free // 4, full_quad, 0)
    lax.fori_loop(2 * (n_free // 4), n_free // 2, full_pair, 0)
    lax.fori_loop(n_free // 2, last // 2, edge_pair, 0)

    def finish(s_ref):
        apply_mask(last, s_ref)
        softmax_pv(last, s_ref)
        outs = [acc_ref[hd, 0:HEAD_DIM, :] * (1.0 / acc_ref[hd, HEAD_DIM:HEAD_DIM + 1, :]) for hd in heads]
        if nq > 1:
            start(qn_ref)
        lf = lam_ref[...]
        lam = (jnp.exp(jnp.sum(lf[0:1] * lf[1:2], axis=-1, keepdims=True))
               - jnp.exp(jnp.sum(lf[2:3] * lf[3:4], axis=-1, keepdims=True)) + lam_init)
        for hd in heads:
            od = outs[hd][:, :tqp] - lam * outs[hd][:, tqp:]
            ms = jnp.mean(od * od, axis=0, keepdims=True)
            on = od * lax.rsqrt(ms + NORM_EPS) * ng_ref[...] * (1.0 - lam_init)
            o_ref[0, :, lanes(hd)] = on.T[:tq].astype(BF16)

    @pl.when(last == 0)
    def _():
        finish(sz_ref)

    @pl.when((last >= 1) & (last % 2 == 1))
    def _():
        finish(s1_ref)

    @pl.when((last >= 1) & (last % 2 == 0))
    def _():
        finish(s0_ref)


def _diff_attn(q, k, vt, w, n_keys, q_pos0, tq, tk):
    b, t, _ = q.shape
    tkp = k.shape[1]
    nblk = tkp // tk
    nq = t // tq
    tqp = max(tq, LANES)
    hps = HEADS if nq == 1 else 1
    wide = hps * HEAD_DIM
    lam_init = 0.8 - 0.6 * math.exp(-0.3 * 0)
    kern = functools.partial(_diff_attn_kernel, tq=tq, tqp=tqp, tk=tk, nblk=nblk, nq=nq, hps=hps, n_keys=n_keys,
                             q_pos0=q_pos0, lam_init=lam_init)
    return pl.pallas_call(
        kern,
        grid=(b, HEADS // hps, nq),
        in_specs=[
            pl.BlockSpec((1, tq, wide), lambda i, h, j: (i, j, h)),
            pl.BlockSpec((1, tq, wide), lambda i, h, j: (i, jnp.minimum(j + 1, nq - 1), h)),
            pl.BlockSpec((1, tkp, wide), lambda i, h, j: (i, 0, h)),
            pl.BlockSpec((nblk, wide, tk), lambda i, h, j: (i, h, 0)),
            pl.BlockSpec((4, DQK), lambda i, h, j: (0, 0)),
            pl.BlockSpec((HEAD_DIM, 1), lambda i, h, j: (0, 0)),
        ],
        out_specs=pl.BlockSpec((1, tq, wide), lambda i, h, j: (i, j, h)),
        out_shape=jax.ShapeDtypeStruct((b, t, KEY_DIM), BF16),
        scratch_shapes=[
            pltpu.VMEM((hps, tk, 2 * tqp + LANES), F32),
            pltpu.VMEM((hps, tk, 2 * tqp + LANES), F32),
            pltpu.VMEM((hps, tk, 2 * tqp + LANES), F32),
            pltpu.VMEM((hps, 1, 2 * tqp), F32),
            pltpu.VMEM((hps, HEAD_DIM + SUM_ROWS, 2 * tqp), F32),
            pltpu.VMEM((hps, HEAD_DIM, 2 * tqp), BF16),
        ],
        compiler_params=_params("arbitrary", "arbitrary", "arbitrary"),
        name="diff_attn",
    )(q, q, k, vt, w["diff_lambda"], w["diff_norm_g_col"])


def _kv_cache_kernel(ck_ref, cv_ref, kn_ref, vn_ref, k_ref, vt_ref, *, p_len, t_new, pad_rows):
    k_ref[0, 0:p_len, :] = ck_ref[0].reshape(p_len, KEY_DIM).astype(BF16)
    zeros = jnp.zeros((pad_rows - t_new, KEY_DIM), BF16)
    k_ref[0, p_len:p_len + pad_rows, :] = jnp.concatenate([kn_ref[0], zeros], axis=0)
    vt_ref[0, :, 0:p_len] = cv_ref[0].reshape(p_len, KEY_DIM).T.astype(BF16)
    v_new = jnp.concatenate([vn_ref[0], zeros], axis=0).astype(F32)
    vt_ref[0, :, p_len:p_len + pad_rows] = v_new.T.astype(BF16)


def _kv_cache(cache_k, cache_v, k_new, v_new):
    b, p_len = cache_k.shape[:2]
    t_new = k_new.shape[1]
    assert p_len % LANES == 0, p_len
    pad_rows = -(-t_new // LANES) * LANES
    rows = p_len + pad_rows
    kern = functools.partial(_kv_cache_kernel, p_len=p_len, t_new=t_new, pad_rows=pad_rows)
    return pl.pallas_call(
        kern,
        grid=(b,),
        in_specs=[
            pl.BlockSpec((1, p_len, HEADS, HEAD_DIM), lambda i: (i, 0, 0, 0)),
            pl.BlockSpec((1, p_len, HEADS, HEAD_DIM), lambda i: (i, 0, 0, 0)),
            pl.BlockSpec((1, t_new, KEY_DIM), lambda i: (i, 0, 0)),
            pl.BlockSpec((1, t_new, KEY_DIM), lambda i: (i, 0, 0)),
        ],
        out_specs=(pl.BlockSpec((1, rows, KEY_DIM), lambda i: (i, 0, 0)),
                   pl.BlockSpec((1, KEY_DIM, rows), lambda i: (i, 0, 0))),
        out_shape=(jax.ShapeDtypeStruct((b, rows, KEY_DIM), BF16),
                   jax.ShapeDtypeStruct((b, KEY_DIM, rows), BF16)),
        compiler_params=_params("parallel"),
        name="kv_cache",
    )(cache_k, cache_v, k_new, v_new)


def _post_mix_kernel(x_ref, og_ref, od_ref, mk_ref, mv_ref, wo_ref, g_ref, wq_ref, wmo_ref, o_ref):
    nb, tm, _ = x_ref.shape
    rows = lambda ref: ref[...].reshape(nb * tm, ref.shape[-1])
    x1 = rows(x_ref) + _dot(rows(og_ref), wo_ref[0:KEY_DIM, :]) + _dot(rows(od_ref), wo_ref[KEY_DIM:, :])
    h2 = _rms(x1, g_ref[...]).astype(BF16)
    qm = _dot(h2, wq_ref[...]).astype(BF16)
    oms = []
    for bi in range(nb):
        outs = []
        for hd in range(HEADS):
            sl = slice(hd * HEAD_DIM, (hd + 1) * HEAD_DIM)
            mk = mk_ref[bi, :, sl]
            mv = mv_ref[bi, :, sl]
            s = _dot_nt(qm[bi * tm:(bi + 1) * tm, sl], mk) * (HEAD_DIM ** -0.5)
            p = jnp.exp(s - jnp.max(s, axis=-1, keepdims=True))
            l = jnp.sum(p, axis=-1, keepdims=True)
            outs.append(_dot((p / l).astype(BF16), mv))
        oms.append(jnp.concatenate(outs, axis=1))
    om = jnp.concatenate(oms, axis=0).astype(BF16)
    o_ref[...] = (x1 + _dot(om, wmo_ref[...])).reshape(o_ref.shape)


def _post_mix(x, og, od, mk, mv, w, tm):
    b, t, _ = x.shape
    nb = _tile(b, max(1, ROW_TILE // tm)) if tm == t else 1
    blk = lambda i, j: (i, j, 0)
    fix = lambda i, j: (0, 0)
    mem = lambda i, j: (i, 0, 0)
    return pl.pallas_call(
        _post_mix_kernel,
        grid=(b // nb, t // tm),
        in_specs=[
            pl.BlockSpec((nb, tm, D_MODEL), blk),
            pl.BlockSpec((nb, tm, KEY_DIM), blk),
            pl.BlockSpec((nb, tm, KEY_DIM), blk),
            pl.BlockSpec((nb, N_MEM, KEY_DIM), mem),
            pl.BlockSpec((nb, N_MEM, KEY_DIM), mem),
            pl.BlockSpec((D_MODEL, D_MODEL), fix),
            pl.BlockSpec((1, D_MODEL), fix),
            pl.BlockSpec((D_MODEL, KEY_DIM), fix),
            pl.BlockSpec((KEY_DIM, D_MODEL), fix),
        ],
        out_specs=pl.BlockSpec((nb, tm, D_MODEL), blk),
        out_shape=jax.ShapeDtypeStruct((b, t, D_MODEL), F32),
        compiler_params=_params("parallel", "parallel"),
        name="post_mix",
    )(x, og, od, mk, mv, w["w_out"], w["norm_mem_g"], w["w_mq"], w["w_mo"])


def _mlp_kernel(x_ref, g_ref, wu_ref, wd_ref, fg_ref, o_ref, *, ff_blk):
    x = x_ref[...]
    hf = _rms(x, g_ref[...]).astype(BF16)
    acc = x
    for c0 in range(0, D_FF, ff_blk):
        u = jnp.maximum(_dot(hf, wu_ref[:, c0:c0 + ff_blk]), 0.0)
        acc = acc + _dot((u * u).astype(BF16), wd_ref[c0:c0 + ff_blk, :])
    o_ref[...] = _rms(acc, fg_ref[...])


def _mlp(x2d, w, tm):
    n = x2d.shape[0]
    row = lambda i: (i, 0)
    fix = lambda i: (0, 0)
    return pl.pallas_call(
        functools.partial(_mlp_kernel, ff_blk=D_MODEL),
        grid=(n // tm,),
        in_specs=[
            pl.BlockSpec((tm, D_MODEL), row),
            pl.BlockSpec((1, D_MODEL), fix),
            pl.BlockSpec((D_MODEL, D_FF), fix),
            pl.BlockSpec((D_FF, D_MODEL), fix),
            pl.BlockSpec((1, D_MODEL), fix),
        ],
        out_specs=pl.BlockSpec((tm, D_MODEL), row),
        out_shape=jax.ShapeDtypeStruct((n, D_MODEL), F32),
        compiler_params=_params("parallel"),
        name="mlp",
    )(x2d, w["norm_ffn_g"], w["w_up"], w["w_down"], w["final_norm_g"])


def _mem_kv_kernel(m_ref, g_ref, w_ref, k_ref, v_ref, kb_ref, vb_ref):
    mh = _rms(m_ref[...], g_ref[...]).astype(BF16)
    kv = _dot(mh, w_ref[...])
    kb_ref[...] = kv[:, :KEY_DIM].astype(BF16)
    vb_ref[...] = kv[:, KEY_DIM:].astype(BF16)
    k_ref[...] = kv[:, :KEY_DIM].reshape(k_ref.shape)
    v_ref[...] = kv[:, KEY_DIM:].reshape(v_ref.shape)


def _mem_kv(mem2d, w, tm):
    n = mem2d.shape[0]
    row = lambda i: (i, 0)
    fix = lambda i: (0, 0)
    return pl.pallas_call(
        _mem_kv_kernel,
        grid=(n // tm,),
        in_specs=[
            pl.BlockSpec((tm, D_MODEL), row),
            pl.BlockSpec((1, D_MODEL), fix),
            pl.BlockSpec((D_MODEL, 2 * KEY_DIM), fix),
        ],
        out_specs=(pl.BlockSpec((tm, HEADS, HEAD_DIM), lambda i: (i, 0, 0)),) * 2
        + (pl.BlockSpec((tm, KEY_DIM), row),) * 2,
        out_shape=(jax.ShapeDtypeStruct((n, HEADS, HEAD_DIM), F32),) * 2
        + (jax.ShapeDtypeStruct((n, KEY_DIM), BF16),) * 2,
        compiler_params=_params("parallel"),
        name="mem_kv",
    )(mem2d, w["mem_norm_g"], w["w_mkv"])


def _tile(n, pref):
    t = min(n, pref)
    assert n % t == 0, (n, t)
    return t


def _layer(x, pos0, conv_prev, s0, kv_prev, mem_k, mem_v, w):
    b, t, _ = x.shape
    n = b * t
    c = min(t, CHUNK)
    x2d = x.reshape(n, D_MODEL)
    tm = _tile(n, ROW_TILE)
    tab_rows = t if t % tm == 0 else n
    pos_rows = pos0 + (np.arange(tab_rows) % t)
    qkv, z, gb, qd, kf, kb, vf, vb, vt = _proj_in(x2d, pos_rows, tab_rows, w, tm)
    r3 = lambda a: a.reshape(b, t, -1)
    qkv3 = r3(qkv)
    og, s_new = _gdn(qkv3, r3(z), r3(gb), conv_prev, s0, w, nchunks=_tile(t // c, 8))
    conv_new = qkv3[:, t - (CONV_W - 1):, :]
    kb3 = r3(kb)
    if kv_prev is None:
        assert t % tm == 0, (t, tm)
        n_keys, tk = t, tm
    else:
        n_keys = kv_prev[0].shape[1] + t
        kb3, vt = _kv_cache(kv_prev[0], kv_prev[1], kb3, r3(vb))
        tk = kb3.shape[1]
    od = _diff_attn(r3(qd), kb3, vt, w, n_keys=n_keys, q_pos0=pos0, tq=_tile(t, ROW_TILE), tk=tk)
    x2 = _post_mix(x, og, od, mem_k, mem_v, w, tm=_tile(t, ROW_TILE))
    y = _mlp(x2.reshape(n, D_MODEL), w, tm).reshape(b, t, D_MODEL)
    h4 = lambda a: a.reshape(b, t, HEADS, HEAD_DIM)
    return y, s_new, conv_new, h4(kf), h4(vf)


def kernel(x_prompt, x_sample, mem_prompt, cache_diff_k, cache_diff_v, cache_mem_k, cache_mem_v, state_gdn, state_gdn_conv, norm_mix_g, w_in, gdn_conv_w, gdn_a_log, gdn_dt_bias, gdn_norm_g, diff_lambda, diff_norm_g, w_out, norm_mem_g, mem_norm_g, w_mq, w_mkv, w_mo, norm_ffn_g, w_up, w_down, final_norm_g):
    bp, tp, _ = x_prompt.shape
    bs, ts, _ = x_sample.shape
    p_len = cache_diff_k.shape[2]
    depth = w_in.shape[0]
    assert depth == 1
    l = 0
    wi = w_in[l]
    sp = [CONV_DIM, CONV_DIM + KEY_DIM, CONV_DIM + KEY_DIM + HEADS, CONV_DIM + KEY_DIM + 2 * HEADS]
    w_ab = jnp.concatenate([wi[:, sp[1]:sp[3]], jnp.zeros((D_MODEL, LANES - 2 * HEADS), F32)], axis=1)
    lanes_pad = lambda a: jnp.concatenate([a, jnp.zeros((LANES - a.shape[0],), F32)])[None, :]
    row = lambda a: a.reshape(1, -1)
    w = {
        "norm_mix_g": row(norm_mix_g[l]),
        "w_main": jnp.concatenate([wi[:, :sp[1]], wi[:, sp[3]:]], axis=1).astype(BF16),
        "w_ab": w_ab.astype(BF16),
        "a_log": lanes_pad(gdn_a_log[l]),
        "dt_bias": lanes_pad(gdn_dt_bias[l]),
        "conv_w": gdn_conv_w[l],
        "gdn_norm_g": row(gdn_norm_g[l]),
        "diff_lambda": diff_lambda[l],
        "diff_norm_g_col": diff_norm_g[l].reshape(HEAD_DIM, 1),
        "w_out": w_out[l].astype(BF16),
        "norm_mem_g": row(norm_mem_g[l]),
        "mem_norm_g": row(mem_norm_g[l]),
        "w_mq": w_mq[l].astype(BF16),
        "w_mkv": w_mkv[l].astype(BF16),
        "w_mo": w_mo[l].astype(BF16),
        "norm_ffn_g": row(norm_ffn_g[l]),
        "w_up": w_up[l].astype(BF16),
        "w_down": w_down[l].astype(BF16),
        "final_norm_g": row(final_norm_g),
    }
    n_mem = mem_prompt.shape[1]
    mk, mv, mkb, mvb = _mem_kv(mem_prompt.reshape(bp * n_mem, D_MODEL), w, _tile(bp * n_mem, ROW_TILE))
    mk = mk.reshape(bp, n_mem, HEADS, HEAD_DIM)
    mv = mv.reshape(bp, n_mem, HEADS, HEAD_DIM)
    mkb = mkb.reshape(bp, n_mem, KEY_DIM)
    mvb = mvb.reshape(bp, n_mem, KEY_DIM)

    zeros_conv = jnp.zeros((bp, CONV_W - 1, CONV_DIM), F32)
    zeros_state = jnp.zeros((bp, HEADS, HEAD_DIM, HEAD_DIM), F32)
    yp, sp_, cp, kp, vp = _layer(x_prompt, 0, zeros_conv, zeros_state, None, mkb, mvb, w)
    ys, ss, cs, ks_, vs = _layer(x_sample, p_len, state_gdn_conv[l], state_gdn[l],
                                 (cache_diff_k[l], cache_diff_v[l]),
                                 cache_mem_k[l].reshape(bs, n_mem, KEY_DIM).astype(BF16),
                                 cache_mem_v[l].reshape(bs, n_mem, KEY_DIM).astype(BF16), w)
    return (yp, ys, sp_[None], cp[None], kp[None], vp[None], mk[None], mv[None],
            ss[None], cs[None], ks_[None], vs[None])
```

```python
import functools
import math

import jax
import jax.numpy as jnp
import numpy as np
from jax import lax
from jax.experimental import pallas as pl
from jax.experimental.pallas import tpu as pltpu

D_MODEL = 1024
CHUNK = 64
HEADS = 4
HEAD_DIM = 128
KEY_DIM = HEADS * HEAD_DIM
CONV_DIM = 3 * KEY_DIM
CONV_W = 4
DQK = 64
ROT_DIM = 16
ROPE_THETA = 500000.0
N_MEM = 256
D_FF = 4 * D_MODEL
NORM_EPS = 1e-6
MAIN_COLS = CONV_DIM + 4 * KEY_DIM
LANES = 128
CONV_PAD = 8
SUM_ROWS = 16
VMEM_LIMIT = 56 * 1024 * 1024
ROW_TILE = 512

F32 = jnp.float32
BF16 = jnp.bfloat16
HIGHEST = lax.Precision.HIGHEST


def _dot(a, b, precision=None):
    return jnp.dot(a, b, preferred_element_type=F32, precision=precision)


def _dot_nt(a, b):
    return lax.dot_general(a, b, (((1,), (1,)), ((), ())), preferred_element_type=F32)


def _dot_tn(a, b):
    return lax.dot_general(a, b, (((0,), (0,)), ((), ())), preferred_element_type=F32)


def _rms(x, g):
    return x * lax.rsqrt(jnp.mean(x * x, axis=-1, keepdims=True) + NORM_EPS) * g


def _sigmoid(x):
    return 1.0 / (1.0 + jnp.exp(-x))


def _params(*sem):
    return pltpu.CompilerParams(dimension_semantics=sem, vmem_limit_bytes=VMEM_LIMIT)


def _proj_in_kernel(x_ref, g_ref, wm_ref, wab_ref, alog_ref, dtb_ref, cos_ref, sna_ref, snb_ref,
                    qkv_ref, z_ref, gb_ref, q_ref, kf_ref, kb_ref, vf_ref, vb_ref, vt_ref):
    x = x_ref[...]
    h = _rms(x, g_ref[...]).astype(BF16)
    main = _dot(h, wm_ref[...])
    ab = _dot(h, wab_ref[...])
    qkv_ref[...] = main[:, :CONV_DIM]
    z_ref[...] = main[:, CONV_DIM:CONV_DIM + KEY_DIM]
    xa = ab + dtb_ref[...]
    softplus = jnp.maximum(xa, 0.0) + jnp.log1p(jnp.exp(-jnp.abs(xa)))
    lane = lax.broadcasted_iota(jnp.int32, ab.shape, 1)
    gb_ref[...] = jnp.where(lane < HEADS, -jnp.exp(alog_ref[...]) * softplus, _sigmoid(ab))
    cos, sna, snb = cos_ref[...], sna_ref[...], snb_ref[...]
    q0 = CONV_DIM + KEY_DIM
    krs, vhs = [], []
    for hd in range(HEADS):
        sl = slice(hd * HEAD_DIM, (hd + 1) * HEAD_DIM)
        qh = main[:, q0 + hd * HEAD_DIM:q0 + (hd + 1) * HEAD_DIM]
        kh = main[:, q0 + KEY_DIM + hd * HEAD_DIM:q0 + KEY_DIM + (hd + 1) * HEAD_DIM]
        vh = main[:, q0 + 2 * KEY_DIM + hd * HEAD_DIM:q0 + 2 * KEY_DIM + (hd + 1) * HEAD_DIM]
        qr = qh * cos + pltpu.roll(qh, LANES - ROT_DIM // 2, 1) * sna + pltpu.roll(qh, ROT_DIM // 2, 1) * snb
        kr = kh * cos + pltpu.roll(kh, LANES - ROT_DIM // 2, 1) * sna + pltpu.roll(kh, ROT_DIM // 2, 1) * snb
        q_ref[:, sl] = (qr * (DQK ** -0.5 * math.log2(math.e))).astype(BF16)
        krs.append(kr)
        vhs.append(vh)
        kb_ref[:, sl] = kr.astype(BF16)
        vb_ref[:, sl] = vh.astype(BF16)
        vt_ref[0, sl, :] = vh.T.astype(BF16)
    kf_ref[...] = jnp.concatenate(krs, axis=1).reshape(kf_ref.shape)
    vf_ref[...] = jnp.concatenate(vhs, axis=1).reshape(vf_ref.shape)


def _rope_tables(pos):
    inv = ROPE_THETA ** (-np.arange(0, ROT_DIM, 2, dtype=np.float64) / ROT_DIM)
    ang = pos.astype(np.float64)[:, None] * inv[None, :]
    cos, sin = np.cos(ang), np.sin(ang)
    pad = np.zeros((pos.shape[0], DQK - ROT_DIM))
    cos64 = np.concatenate([cos, cos, pad + 1.0], axis=1)
    sna64 = np.concatenate([-sin, np.zeros_like(sin), pad], axis=1)
    snb64 = np.concatenate([np.zeros_like(sin), sin, pad], axis=1)
    return tuple(jnp.asarray(np.concatenate([a, a], axis=1), F32) for a in (cos64, sna64, snb64))


def _proj_in(x2d, pos_rows, tab_rows, w, tm):
    n = x2d.shape[0]
    nt = tab_rows // tm
    cos, sna, snb = _rope_tables(pos_rows)
    row = lambda i: (i, 0)
    fix = lambda i: (0, 0)
    tab = lambda i: (i % nt, 0)
    out_shape = (
        jax.ShapeDtypeStruct((n, CONV_DIM), F32),
        jax.ShapeDtypeStruct((n, KEY_DIM), F32),
        jax.ShapeDtypeStruct((n, LANES), F32),
        jax.ShapeDtypeStruct((n, KEY_DIM), BF16),
        jax.ShapeDtypeStruct((n, HEADS, HEAD_DIM), F32),
        jax.ShapeDtypeStruct((n, KEY_DIM), BF16),
        jax.ShapeDtypeStruct((n, HEADS, HEAD_DIM), F32),
        jax.ShapeDtypeStruct((n, KEY_DIM), BF16),
        jax.ShapeDtypeStruct((n // tm, KEY_DIM, tm), BF16),
    )

    def spec(sd):
        if len(sd.shape) == 2:
            return pl.BlockSpec((tm, sd.shape[1]), row)
        return pl.BlockSpec((sd.shape[0] * tm // n,) + sd.shape[1:], lambda i: (i, 0, 0))
    out_specs = tuple(spec(sd) for sd in out_shape)
    return pl.pallas_call(
        _proj_in_kernel,
        grid=(n // tm,),
        in_specs=[
            pl.BlockSpec((tm, D_MODEL), row),
            pl.BlockSpec((1, D_MODEL), fix),
            pl.BlockSpec((D_MODEL, MAIN_COLS), fix),
            pl.BlockSpec((D_MODEL, LANES), fix),
            pl.BlockSpec((1, LANES), fix),
            pl.BlockSpec((1, LANES), fix),
            pl.BlockSpec((tm, LANES), tab),
            pl.BlockSpec((tm, LANES), tab),
            pl.BlockSpec((tm, LANES), tab),
        ],
        out_specs=out_specs,
        out_shape=out_shape,
        compiler_params=_params("parallel"),
        name="proj_in",
    )(x2d, w["norm_mix_g"], w["w_main"], w["w_ab"], w["a_log"], w["dt_bias"], cos, sna, snb)


def _gdn_kernel(qkv_ref, z_ref, gb_ref, cprev_ref, s0_ref, cw_ref, ng_ref,
                o_ref, sout_ref, xp_ref, y_ref, s_ref, *, c, nchunks):
    j = pl.program_id(1)
    rows = c * nchunks
    hc = HEADS * c

    @pl.when(j == 0)
    def _():
        xp_ref[0:CONV_PAD, :] = cprev_ref[0]
        s_ref[...] = s0_ref[0]

    xp_ref[CONV_PAD:CONV_PAD + rows, :] = qkv_ref[0]
    cw = cw_ref[...]
    xp = xp_ref[...]
    y = xp[CONV_PAD:] * cw[CONV_W - 1:CONV_W, :]
    for i in range(CONV_W - 1):
        y = y + pltpu.roll(xp, CONV_W - 1 - i, 0)[CONV_PAD:] * cw[i:i + 1, :]
    xp_ref[0:CONV_PAD, :] = xp[rows:rows + CONV_PAD]
    y_ref[...] = y * _sigmoid(y)

    ri = lax.broadcasted_iota(jnp.int32, (hc, hc), 0)
    cj = lax.broadcasted_iota(jnp.int32, (hc, hc), 1)
    same = (ri // c) == (cj // c)
    tril_bd = same & (ri >= cj)
    strict_bd = same & (ri > cj)
    eye_bd = (ri == cj).astype(F32)
    r1 = lax.broadcasted_iota(jnp.int32, (c, c), 0)
    c1 = lax.broadcasted_iota(jnp.int32, (c, c), 1)
    tril_c = (r1 >= c1).astype(F32)
    triu_c = (r1 <= c1).astype(F32)
    ng = ng_ref[...]
    gb_all = gb_ref[0]
    if rows % LANES:
        gb_all = jnp.concatenate([gb_all, jnp.zeros((LANES - rows % LANES, LANES), F32)], axis=0)
    gbt_all = gb_all.T[0:2 * HEADS, :]

    def prep(ci):
        rs = slice(ci * c, (ci + 1) * c)
        gcol = gb_ref[0, rs, :]
        gc_col = _dot(tril_c, gcol, HIGHEST)
        gc_row = _dot(gbt_all[:, rs], triu_c, HIGHEST)
        ks, kbs, qs, vbs, kbgs, qgs, kds, dcol, grow, glast = [], [], [], [], [], [], [], [], [], []
        for hd in range(HEADS):
            qh = y_ref[rs, hd * HEAD_DIM:(hd + 1) * HEAD_DIM]
            kh = y_ref[rs, KEY_DIM + hd * HEAD_DIM:KEY_DIM + (hd + 1) * HEAD_DIM]
            vh = y_ref[rs, 2 * KEY_DIM + hd * HEAD_DIM:2 * KEY_DIM + (hd + 1) * HEAD_DIM]
            qh = qh * lax.rsqrt(jnp.sum(qh * qh, axis=-1, keepdims=True) + NORM_EPS) * (HEAD_DIM ** -0.5)
            kh = kh * lax.rsqrt(jnp.sum(kh * kh, axis=-1, keepdims=True) + NORM_EPS)
            g_h = gc_col[:, hd:hd + 1]
            beta_h = gcol[:, HEADS + hd:HEADS + hd + 1]
            g_last = gc_col[c - 1:c, hd:hd + 1]
            eg = jnp.exp(g_h)
            kb = kh * beta_h
            ks.append(kh)
            kbs.append(kb)
            qs.append(qh)
            vbs.append(vh * beta_h)
            kbgs.append(kb * eg)
            qgs.append((qh * eg).astype(BF16))
            kds.append((kh * jnp.exp(g_last - g_h)).astype(BF16))
            glast.append(jnp.exp(g_last))
            dcol.append(jnp.broadcast_to(g_h, (c, hc)))
            grow.append(gc_row[hd:hd + 1, :])
        k_s = jnp.concatenate(ks, axis=0).astype(BF16)
        q_s = jnp.concatenate(qs, axis=0).astype(BF16)
        dmat = jnp.concatenate(dcol, axis=0) - jnp.concatenate(grow, axis=1)
        decay = jnp.exp(jnp.where(tril_bd, dmat, -jnp.inf))
        a_kk = _dot_nt(jnp.concatenate(kbs, axis=0).astype(BF16), k_s)
        x = jnp.where(strict_bd, -(a_kk * decay), 0.0)
        rhs = jnp.concatenate([jnp.concatenate(vbs, axis=0), jnp.concatenate(kbgs, axis=0)], axis=1)
        a_qk = (_dot_nt(q_s, k_s) * decay).astype(BF16)
        return dict(x=x, rhs=rhs.astype(BF16), a_qk=a_qk, qg=qgs, kd=kds, glast=glast)

    pre = [prep(ci) for ci in range(nchunks)]
    ps = [eye_bd + d["x"] for d in pre]
    xps = [d["x"] for d in pre]
    for _ in range(int(math.log2(c)) - 1):
        xbs = [xp.astype(BF16) for xp in xps]
        xps = [_dot(xb, xb) for xb in xbs]
        ps = [p + _dot(p.astype(BF16), xp.astype(BF16)) for p, xp in zip(ps, xps)]
    uws = [_dot(p.astype(BF16), d["rhs"]) for p, d in zip(ps, pre)]

    state = [s_ref[hd] for hd in range(HEADS)]
    for ci in range(nchunks):
        rs = slice(ci * c, (ci + 1) * c)
        d, uw = pre[ci], uws[ci]
        vnews, qss = [], []
        for hd in range(HEADS):
            hs = slice(hd * c, (hd + 1) * c)
            wq = jnp.concatenate([uw[hs, HEAD_DIM:].astype(BF16), d["qg"][hd]], axis=0)
            ws = _dot(wq, state[hd].astype(BF16))
            v_new = uw[hs, :HEAD_DIM] - ws[:c]
            vnews.append(v_new)
            qss.append(ws[c:])
            state[hd] = state[hd] * d["glast"][hd] + _dot_tn(d["kd"][hd], v_new.astype(BF16))
        o_s = jnp.concatenate(qss, axis=0) + _dot(d["a_qk"], jnp.concatenate(vnews, axis=0).astype(BF16))
        for hd in range(HEADS):
            sl = slice(hd * HEAD_DIM, (hd + 1) * HEAD_DIM)
            zh = z_ref[0, rs, sl]
            o_h = _rms(o_s[hd * c:(hd + 1) * c], ng) * (zh * _sigmoid(zh))
            o_ref[0, rs, sl] = o_h.astype(BF16)
    for hd in range(HEADS):
        s_ref[hd] = state[hd]

    @pl.when(j == pl.num_programs(1) - 1)
    def _():
        sout_ref[0] = s_ref[...]


def _gdn(qkv, z, gb, conv_prev, s0, w, nchunks):
    b, t, _ = qkv.shape
    c = min(t, CHUNK)
    rows = c * nchunks
    cprev = jnp.concatenate([jnp.zeros((b, CONV_PAD - (CONV_W - 1), CONV_DIM), F32), conv_prev], axis=1)
    blk = lambda i, j: (i, j, 0)
    kern = functools.partial(_gdn_kernel, c=c, nchunks=nchunks)
    return pl.pallas_call(
        kern,
        grid=(b, t // rows),
        in_specs=[
            pl.BlockSpec((1, rows, CONV_DIM), blk),
            pl.BlockSpec((1, rows, KEY_DIM), blk),
            pl.BlockSpec((1, rows, LANES), blk),
            pl.BlockSpec((1, CONV_PAD, CONV_DIM), lambda i, j: (i, 0, 0)),
            pl.BlockSpec((1, HEADS, HEAD_DIM, HEAD_DIM), lambda i, j: (i, 0, 0, 0)),
            pl.BlockSpec((CONV_W, CONV_DIM), lambda i, j: (0, 0)),
            pl.BlockSpec((1, HEAD_DIM), lambda i, j: (0, 0)),
        ],
        out_specs=(
            pl.BlockSpec((1, rows, KEY_DIM), blk),
            pl.BlockSpec((1, HEADS, HEAD_DIM, HEAD_DIM), lambda i, j: (i, 0, 0, 0)),
        ),
        out_shape=(
            jax.ShapeDtypeStruct((b, t, KEY_DIM), BF16),
            jax.ShapeDtypeStruct((b, HEADS, HEAD_DIM, HEAD_DIM), F32),
        ),
        scratch_shapes=[
            pltpu.VMEM((rows + CONV_PAD, CONV_DIM), F32),
            pltpu.VMEM((rows, CONV_DIM), F32),
            pltpu.VMEM((HEADS, HEAD_DIM, HEAD_DIM), F32),
        ],
        compiler_params=_params("parallel", "arbitrary"),
        name="gdn",
    )(qkv, z, gb, cprev, s0, w["conv_w"], w["gdn_norm_g"])


def _diff_attn_kernel(q_ref, qn_ref, k_ref, vt_ref, lam_ref, ng_ref, o_ref,
                      sz_ref, s0_ref, s1_ref, m_ref, acc_ref, qst_ref,
                      *, tq, tqp, tk, nblk, nq, hps, ahead, n_keys, q_pos0, lam_init):
    qi = pl.program_id(2)
    width = 2 * tqp
    heads = range(hps)
    lanes = lambda hd: slice(hd * HEAD_DIM, (hd + 1) * HEAD_DIM)
    pos_lo = q_pos0 + qi * tq
    k_lo = jnp.minimum((pos_lo // CHUNK + 1) * CHUNK, n_keys)
    k_hi = jnp.minimum(((pos_lo + tq - 1) // CHUNK + 1) * CHUNK, n_keys)
    n_full = k_lo // tk
    last = (k_hi + tk - 1) // tk - 1

    def scores(jb, s_ref):
        jl = jnp.minimum(jb, nblk - 1)
        ks = pl.ds(pl.multiple_of(jl * tk, tk), tk)
        for hd in heads:
            s_ref[hd, :, 0:width] = _dot(k_ref[0, ks, lanes(hd)], qst_ref[hd])

    def start(qx_ref):
        for hd in heads:
            q = qx_ref[0, :, lanes(hd)]
            if tqp > tq:
                q = jnp.concatenate([q, jnp.zeros((tqp - tq, HEAD_DIM), q.dtype)], axis=0)
            lane = lax.broadcasted_iota(jnp.int32, q.shape, 1)
            zero = jnp.zeros_like(q)
            qs = jnp.concatenate([jnp.where(lane < DQK, q, zero), jnp.where(lane >= DQK, q, zero)], axis=0)
            qst_ref[hd] = qs.T
        m_ref[...] = jnp.full(m_ref.shape, -jnp.inf, F32)
        acc_ref[...] = jnp.zeros(acc_ref.shape, F32)
        scores(0, sz_ref)

    def apply_mask(jb, s_ref):
        col = lax.broadcasted_iota(jnp.int32, (tk, width), 1)
        qpos = pos_lo + jnp.where(col >= tqp, col - tqp, col)
        kpos = jb * tk + lax.broadcasted_iota(jnp.int32, (tk, width), 0)
        vis = kpos < jnp.minimum((qpos // CHUNK + 1) * CHUNK, n_keys)
        for hd in heads:
            s_ref[hd, :, 0:width] = jnp.where(vis, s_ref[hd, :, 0:width], -jnp.inf)

    def mask(jb, s_ref):
        @pl.when(jb >= n_full)
        def _():
            apply_mask(jb, s_ref)

    def softmax_pv(jb, s_ref):
        ones = jnp.ones((SUM_ROWS, tk), BF16)
        for hd in heads:
            st = s_ref[hd, :, 0:width]
            m_old = m_ref[hd]
            m_new = jnp.maximum(m_old, jnp.max(st, axis=0, keepdims=True))
            alpha = jnp.exp2(m_old - m_new)
            p = jnp.exp2(st - m_new)
            vt1 = jnp.concatenate([vt_ref[jnp.minimum(jb, nblk - 1), lanes(hd), :], ones], axis=0)
            acc_ref[hd] = alpha * acc_ref[hd] + _dot(vt1, p.astype(BF16))
            m_ref[hd] = m_new

    def first_block():
        mask(0, sz_ref)
        scores(1, s1_ref)
        softmax_pv(0, sz_ref)

    @pl.when(qi == 0)
    def _():
        start(q_ref)

    @pl.when((last >= 1) & ((qi == 0) if ahead else True))
    def _():
        first_block()

    def full_pair(i, carry):
        j = 1 + 2 * i
        scores(j + 1, s0_ref)
        softmax_pv(j, s1_ref)
        scores(j + 2, s1_ref)
        softmax_pv(j + 1, s0_ref)
        return carry

    def full_quad(i, carry):
        full_pair(2 * i, carry)
        return full_pair(2 * i + 1, carry)

    def full_octo(i, carry):
        full_quad(2 * i, carry)
        return full_quad(2 * i + 1, carry)

    def edge_pair(i, carry):
        j = 1 + 2 * i
        mask(j, s1_ref)
        scores(j + 1, s0_ref)
        softmax_pv(j, s1_ref)

        @pl.when(j + 1 < last)
        def _():
            mask(j + 1, s0_ref)
            scores(j + 2, s1_ref)
            softmax_pv(j + 1, s0_ref)
        return carry

    n_free = jnp.maximum(jnp.minimum(n_full, last) - 1, 0)
    lax.fori_loop(0, n_free // 8, full_octo, 0)
    lax.fori_loop(2 * (n_free // 8), n_free // 4, full_quad, 0)
    lax.fori_loop(2 * (n_free // 4), n_free // 2, full_pair, 0)
    lax.fori_loop(n_free // 2, last // 2, edge_pair, 0)

    def finish(s_ref, more):
        apply_mask(last, s_ref)
        softmax_pv(last, s_ref)
        outs = [acc_ref[hd, 0:HEAD_DIM, :] * (1.0 / acc_ref[hd, HEAD_DIM:HEAD_DIM + 1, :]) for hd in heads]
        if more:
            start(qn_ref)
            if ahead:
                scores(1, s1_ref)
                softmax_pv(0, sz_ref)
        lf = lam_ref[...]
        lam = (jnp.exp(jnp.sum(lf[0:1] * lf[1:2], axis=-1, keepdims=True))
               - jnp.exp(jnp.sum(lf[2:3] * lf[3:4], axis=-1, keepdims=True)) + lam_init)
        for hd in heads:
            od = outs[hd][:, :tqp] - lam * outs[hd][:, tqp:]
            ms = jnp.mean(od * od, axis=0, keepdims=True)
            on = od * lax.rsqrt(ms + NORM_EPS) * ng_ref[...] * (1.0 - lam_init)
            o_ref[0, :, lanes(hd)] = on.T[:tq].astype(BF16)

    for more in ((True, False) if nq > 1 else (False,)):
        step_ok = (qi < nq - 1) if more else (qi == nq - 1)

        @pl.when(step_ok & (last == 0))
        def _():
            finish(sz_ref, more)

        @pl.when(step_ok & (last >= 1) & (last % 2 == 1))
        def _():
            finish(s1_ref, more)

        @pl.when(step_ok & (last >= 1) & (last % 2 == 0))
        def _():
            finish(s0_ref, more)


def _diff_attn(q, k, vt, w, n_keys, q_pos0, tq, tk):
    b, t, _ = q.shape
    tkp = k.shape[1]
    nblk = tkp // tk
    nq = t // tq
    tqp = max(tq, LANES)
    hps = HEADS if nq == 1 else 1
    wide = hps * HEAD_DIM
    lam_init = 0.8 - 0.6 * math.exp(-0.3 * 0)

    def bounds(qi):
        k_lo = min(((q_pos0 + qi * tq) // CHUNK + 1) * CHUNK, n_keys)
        k_hi = min(((q_pos0 + qi * tq + tq - 1) // CHUNK + 1) * CHUNK, n_keys)
        return k_lo // tk, -(-k_hi // tk) - 1
    ahead = nq > 1 and all(min(bounds(qi)) >= 1 for qi in range(1, nq))
    kern = functools.partial(_diff_attn_kernel, tq=tq, tqp=tqp, tk=tk, nblk=nblk, nq=nq, hps=hps, ahead=ahead,
                             n_keys=n_keys, q_pos0=q_pos0, lam_init=lam_init)
    return pl.pallas_call(
        kern,
        grid=(b, HEADS // hps, nq),
        in_specs=[
            pl.BlockSpec((1, tq, wide), lambda i, h, j: (i, j, h)),
            pl.BlockSpec((1, tq, wide), lambda i, h, j: (i, jnp.minimum(j + 1, nq - 1), h)),
            pl.BlockSpec((1, tkp, wide), lambda i, h, j: (i, 0, h)),
            pl.BlockSpec((nblk, wide, tk), lambda i, h, j: (i, h, 0)),
            pl.BlockSpec((4, DQK), lambda i, h, j: (0, 0)),
            pl.BlockSpec((HEAD_DIM, 1), lambda i, h, j: (0, 0)),
        ],
        out_specs=pl.BlockSpec((1, tq, wide), lambda i, h, j: (i, j, h)),
        out_shape=jax.ShapeDtypeStruct((b, t, KEY_DIM), BF16),
        scratch_shapes=[
            pltpu.VMEM((hps, tk, 2 * tqp + LANES), F32),
            pltpu.VMEM((hps, tk, 2 * tqp + LANES), F32),
            pltpu.VMEM((hps, tk, 2 * tqp + LANES), F32),
            pltpu.VMEM((hps, 1, 2 * tqp), F32),
            pltpu.VMEM((hps, HEAD_DIM + SUM_ROWS, 2 * tqp), F32),
            pltpu.VMEM((hps, HEAD_DIM, 2 * tqp), BF16),
        ],
        compiler_params=_params("arbitrary", "arbitrary", "arbitrary"),
        name="diff_attn",
    )(q, q, k, vt, w["diff_lambda"], w["diff_norm_g_col"])


def _kv_cache_kernel(ck_ref, cv_ref, kn_ref, vn_ref, k_ref, vt_ref, *, p_len, t_new, pad_rows):
    k_ref[0, 0:p_len, :] = ck_ref[0].reshape(p_len, KEY_DIM).astype(BF16)
    zeros = jnp.zeros((pad_rows - t_new, KEY_DIM), BF16)
    k_ref[0, p_len:p_len + pad_rows, :] = jnp.concatenate([kn_ref[0], zeros], axis=0)
    vt_ref[0, :, 0:p_len] = cv_ref[0].reshape(p_len, KEY_DIM).T.astype(BF16)
    v_new = jnp.concatenate([vn_ref[0], zeros], axis=0).astype(F32)
    vt_ref[0, :, p_len:p_len + pad_rows] = v_new.T.astype(BF16)


def _kv_cache(cache_k, cache_v, k_new, v_new):
    b, p_len = cache_k.shape[:2]
    t_new = k_new.shape[1]
    assert p_len % LANES == 0, p_len
    pad_rows = -(-t_new // LANES) * LANES
    rows = p_len + pad_rows
    kern = functools.partial(_kv_cache_kernel, p_len=p_len, t_new=t_new, pad_rows=pad_rows)
    return pl.pallas_call(
        kern,
        grid=(b,),
        in_specs=[
            pl.BlockSpec((1, p_len, HEADS, HEAD_DIM), lambda i: (i, 0, 0, 0)),
            pl.BlockSpec((1, p_len, HEADS, HEAD_DIM), lambda i: (i, 0, 0, 0)),
            pl.BlockSpec((1, t_new, KEY_DIM), lambda i: (i, 0, 0)),
            pl.BlockSpec((1, t_new, KEY_DIM), lambda i: (i, 0, 0)),
        ],
        out_specs=(pl.BlockSpec((1, rows, KEY_DIM), lambda i: (i, 0, 0)),
                   pl.BlockSpec((1, KEY_DIM, rows), lambda i: (i, 0, 0))),
        out_shape=(jax.ShapeDtypeStruct((b, rows, KEY_DIM), BF16),
                   jax.ShapeDtypeStruct((b, KEY_DIM, rows), BF16)),
        compiler_params=_params("parallel"),
        name="kv_cache",
    )(cache_k, cache_v, k_new, v_new)


def _post_mix_kernel(x_ref, og_ref, od_ref, mk_ref, mv_ref, wo_ref, g_ref, wq_ref, wmo_ref, o_ref):
    nb, tm, _ = x_ref.shape
    rows = lambda ref: ref[...].reshape(nb * tm, ref.shape[-1])
    x1 = rows(x_ref) + _dot(rows(og_ref), wo_ref[0:KEY_DIM, :]) + _dot(rows(od_ref), wo_ref[KEY_DIM:, :])
    h2 = _rms(x1, g_ref[...]).astype(BF16)
    qm = _dot(h2, wq_ref[...]).astype(BF16)
    oms = []
    for bi in range(nb):
        outs = []
        for hd in range(HEADS):
            sl = slice(hd * HEAD_DIM, (hd + 1) * HEAD_DIM)
            mk = mk_ref[bi, :, sl]
            mv = mv_ref[bi, :, sl]
            s = _dot_nt(qm[bi * tm:(bi + 1) * tm, sl], mk) * (HEAD_DIM ** -0.5)
            p = jnp.exp(s - jnp.max(s, axis=-1, keepdims=True))
            l = jnp.sum(p, axis=-1, keepdims=True)
            outs.append(_dot((p / l).astype(BF16), mv))
        oms.append(jnp.concatenate(outs, axis=1))
    om = jnp.concatenate(oms, axis=0).astype(BF16)
    o_ref[...] = (x1 + _dot(om, wmo_ref[...])).reshape(o_ref.shape)


def _post_mix(x, og, od, mk, mv, w, tm):
    b, t, _ = x.shape
    nb = _tile(b, max(1, ROW_TILE // tm)) if tm == t else 1
    blk = lambda i, j: (i, j, 0)
    fix = lambda i, j: (0, 0)
    mem = lambda i, j: (i, 0, 0)
    return pl.pallas_call(
        _post_mix_kernel,
        grid=(b // nb, t // tm),
        in_specs=[
            pl.BlockSpec((nb, tm, D_MODEL), blk),
            pl.BlockSpec((nb, tm, KEY_DIM), blk),
            pl.BlockSpec((nb, tm, KEY_DIM), blk),
            pl.BlockSpec((nb, N_MEM, KEY_DIM), mem),
            pl.BlockSpec((nb, N_MEM, KEY_DIM), mem),
            pl.BlockSpec((D_MODEL, D_MODEL), fix),
            pl.BlockSpec((1, D_MODEL), fix),
            pl.BlockSpec((D_MODEL, KEY_DIM), fix),
            pl.BlockSpec((KEY_DIM, D_MODEL), fix),
        ],
        out_specs=pl.BlockSpec((nb, tm, D_MODEL), blk),
        out_shape=jax.ShapeDtypeStruct((b, t, D_MODEL), F32),
        compiler_params=_params("parallel", "parallel"),
        name="post_mix",
    )(x, og, od, mk, mv, w["w_out"], w["norm_mem_g"], w["w_mq"], w["w_mo"])


def _mlp_kernel(x_ref, g_ref, wu_ref, wd_ref, fg_ref, o_ref, *, ff_blk):
    x = x_ref[...]
    hf = _rms(x, g_ref[...]).astype(BF16)
    acc = x
    for c0 in range(0, D_FF, ff_blk):
        u = jnp.maximum(_dot(hf, wu_ref[:, c0:c0 + ff_blk]), 0.0)
        acc = acc + _dot((u * u).astype(BF16), wd_ref[c0:c0 + ff_blk, :])
    o_ref[...] = _rms(acc, fg_ref[...])


def _mlp(x2d, w, tm):
    n = x2d.shape[0]
    row = lambda i: (i, 0)
    fix = lambda i: (0, 0)
    return pl.pallas_call(
        functools.partial(_mlp_kernel, ff_blk=D_MODEL),
        grid=(n // tm,),
        in_specs=[
            pl.BlockSpec((tm, D_MODEL), row),
            pl.BlockSpec((1, D_MODEL), fix),
            pl.BlockSpec((D_MODEL, D_FF), fix),
            pl.BlockSpec((D_FF, D_MODEL), fix),
            pl.BlockSpec((1, D_MODEL), fix),
        ],
        out_specs=pl.BlockSpec((tm, D_MODEL), row),
        out_shape=jax.ShapeDtypeStruct((n, D_MODEL), F32),
        compiler_params=_params("parallel"),
        name="mlp",
    )(x2d, w["norm_ffn_g"], w["w_up"], w["w_down"], w["final_norm_g"])


def _mem_kv_kernel(m_ref, g_ref, w_ref, k_ref, v_ref, kb_ref, vb_ref):
    mh = _rms(m_ref[...], g_ref[...]).astype(BF16)
    kv = _dot(mh, w_ref[...])
    kb_ref[...] = kv[:, :KEY_DIM].astype(BF16)
    vb_ref[...] = kv[:, KEY_DIM:].astype(BF16)
    k_ref[...] = kv[:, :KEY_DIM].reshape(k_ref.shape)
    v_ref[...] = kv[:, KEY_DIM:].reshape(v_ref.shape)


def _mem_kv(mem2d, w, tm):
    n = mem2d.shape[0]
    row = lambda i: (i, 0)
    fix = lambda i: (0, 0)
    return pl.pallas_call(
        _mem_kv_kernel,
        grid=(n // tm,),
        in_specs=[
            pl.BlockSpec((tm, D_MODEL), row),
            pl.BlockSpec((1, D_MODEL), fix),
            pl.BlockSpec((D_MODEL, 2 * KEY_DIM), fix),
        ],
        out_specs=(pl.BlockSpec((tm, HEADS, HEAD_DIM), lambda i: (i, 0, 0)),) * 2
        + (pl.BlockSpec((tm, KEY_DIM), row),) * 2,
        out_shape=(jax.ShapeDtypeStruct((n, HEADS, HEAD_DIM), F32),) * 2
        + (jax.ShapeDtypeStruct((n, KEY_DIM), BF16),) * 2,
        compiler_params=_params("parallel"),
        name="mem_kv",
    )(mem2d, w["mem_norm_g"], w["w_mkv"])


def _tile(n, pref):
    t = min(n, pref)
    assert n % t == 0, (n, t)
    return t


def _layer(x, pos0, conv_prev, s0, kv_prev, mem_k, mem_v, w):
    b, t, _ = x.shape
    n = b * t
    c = min(t, CHUNK)
    x2d = x.reshape(n, D_MODEL)
    tm = _tile(n, ROW_TILE)
    tab_rows = t if t % tm == 0 else n
    pos_rows = pos0 + (np.arange(tab_rows) % t)
    qkv, z, gb, qd, kf, kb, vf, vb, vt = _proj_in(x2d, pos_rows, tab_rows, w, tm)
    r3 = lambda a: a.reshape(b, t, -1)
    qkv3 = r3(qkv)
    og, s_new = _gdn(qkv3, r3(z), r3(gb), conv_prev, s0, w, nchunks=_tile(t // c, 8))
    conv_new = qkv3[:, t - (CONV_W - 1):, :]
    kb3 = r3(kb)
    if kv_prev is None:
        assert t % tm == 0, (t, tm)
        n_keys, tk = t, tm
    else:
        n_keys = kv_prev[0].shape[1] + t
        kb3, vt = _kv_cache(kv_prev[0], kv_prev[1], kb3, r3(vb))
        tk = kb3.shape[1]
    od = _diff_attn(r3(qd), kb3, vt, w, n_keys=n_keys, q_pos0=pos0, tq=_tile(t, ROW_TILE), tk=tk)
    x2 = _post_mix(x, og, od, mem_k, mem_v, w, tm=_tile(t, ROW_TILE))
    y = _mlp(x2.reshape(n, D_MODEL), w, tm).reshape(b, t, D_MODEL)
    h4 = lambda a: a.reshape(b, t, HEADS, HEAD_DIM)
    return y, s_new, conv_new, h4(kf), h4(vf)


def kernel(x_prompt, x_sample, mem_prompt, cache_diff_k, cache_diff_v, cache_mem_k, cache_mem_v, state_gdn, state_gdn_conv, norm_mix_g, w_in, gdn_conv_w, gdn_a_log, gdn_dt_bias, gdn_norm_g, diff_lambda, diff_norm_g, w_out, norm_mem_g, mem_norm_g, w_mq, w_mkv, w_mo, norm_ffn_g, w_up, w_down, final_norm_g):
    bp, tp, _ = x_prompt.shape
    bs, ts, _ = x_sample.shape
    p_len = cache_diff_k.shape[2]
    depth = w_in.shape[0]
    assert depth == 1
    l = 0
    wi = w_in[l]
    sp = [CONV_DIM, CONV_DIM + KEY_DIM, CONV_DIM + KEY_DIM + HEADS, CONV_DIM + KEY_DIM + 2 * HEADS]
    w_ab = jnp.concatenate([wi[:, sp[1]:sp[3]], jnp.zeros((D_MODEL, LANES - 2 * HEADS), F32)], axis=1)
    lanes_pad = lambda a: jnp.concatenate([a, jnp.zeros((LANES - a.shape[0],), F32)])[None, :]
    row = lambda a: a.reshape(1, -1)
    w = {
        "norm_mix_g": row(norm_mix_g[l]),
        "w_main": jnp.concatenate([wi[:, :sp[1]], wi[:, sp[3]:]], axis=1).astype(BF16),
        "w_ab": w_ab.astype(BF16),
        "a_log": lanes_pad(gdn_a_log[l]),
        "dt_bias": lanes_pad(gdn_dt_bias[l]),
        "conv_w": gdn_conv_w[l],
        "gdn_norm_g": row(gdn_norm_g[l]),
        "diff_lambda": diff_lambda[l],
        "diff_norm_g_col": diff_norm_g[l].reshape(HEAD_DIM, 1),
        "w_out": w_out[l].astype(BF16),
        "norm_mem_g": row(norm_mem_g[l]),
        "mem_norm_g": row(mem_norm_g[l]),
        "w_mq": w_mq[l].astype(BF16),
        "w_mkv": w_mkv[l].astype(BF16),
        "w_mo": w_mo[l].astype(BF16),
        "norm_ffn_g": row(norm_ffn_g[l]),
        "w_up": w_up[l].astype(BF16),
        "w_down": w_down[l].astype(BF16),
        "final_norm_g": row(final_norm_g),
    }
    n_mem = mem_prompt.shape[1]
    mk, mv, mkb, mvb = _mem_kv(mem_prompt.reshape(bp * n_mem, D_MODEL), w, _tile(bp * n_mem, ROW_TILE))
    mk = mk.reshape(bp, n_mem, HEADS, HEAD_DIM)
    mv = mv.reshape(bp, n_mem, HEADS, HEAD_DIM)
    mkb = mkb.reshape(bp, n_mem, KEY_DIM)
    mvb = mvb.reshape(bp, n_mem, KEY_DIM)

    zeros_conv = jnp.zeros((bp, CONV_W - 1, CONV_DIM), F32)
    zeros_state = jnp.zeros((bp, HEADS, HEAD_DIM, HEAD_DIM), F32)
    yp, sp_, cp, kp, vp = _layer(x_prompt, 0, zeros_conv, zeros_state, None, mkb, mvb, w)
    ys, ss, cs, ks_, vs = _layer(x_sample, p_len, state_gdn_conv[l], state_gdn[l],
                                 (cache_diff_k[l], cache_diff_v[l]),
                                 cache_mem_k[l].reshape(bs, n_mem, KEY_DIM).astype(BF16),
                                 cache_mem_v[l].reshape(bs, n_mem, KEY_DIM).astype(BF16), w)
    return (yp, ys, sp_[None], cp[None], kp[None], vp[None], mk[None], mv[None],
            ss[None], cs[None], ks_[None], vs[None])
```

```python
import functools
import math

import jax
import jax.numpy as jnp
import numpy as np
from jax import lax
from jax.experimental import pallas as pl
from jax.experimental.pallas import tpu as pltpu

D_MODEL = 1024
CHUNK = 64
HEADS = 4
HEAD_DIM = 128
KEY_DIM = HEADS * HEAD_DIM
CONV_DIM = 3 * KEY_DIM
CONV_W = 4
DQK = 64
ROT_DIM = 16
ROPE_THETA = 500000.0
N_MEM = 256
D_FF = 4 * D_MODEL
NORM_EPS = 1e-6
MAIN_COLS = CONV_DIM + 4 * KEY_DIM
LANES = 128
CONV_PAD = 8
SUM_ROWS = 16
VMEM_LIMIT = 56 * 1024 * 1024
ROW_TILE = 512
GDN_CHUNKS = 8

F32 = jnp.float32
BF16 = jnp.bfloat16
HIGHEST = lax.Precision.HIGHEST


def _dot(a, b, precision=None):
    return jnp.dot(a, b, preferred_element_type=F32, precision=precision)


def _dot_nt(a, b):
    return lax.dot_general(a, b, (((1,), (1,)), ((), ())), preferred_element_type=F32)


def _dot_tn(a, b):
    return lax.dot_general(a, b, (((0,), (0,)), ((), ())), preferred_element_type=F32)


def _rms(x, g):
    return x * lax.rsqrt(jnp.mean(x * x, axis=-1, keepdims=True) + NORM_EPS) * g


def _sigmoid(x):
    return 1.0 / (1.0 + jnp.exp(-x))


def _params(*sem):
    return pltpu.CompilerParams(dimension_semantics=sem, vmem_limit_bytes=VMEM_LIMIT)


def _proj_in_kernel(x_ref, g_ref, wm_ref, wab_ref, alog_ref, dtb_ref, cos_ref, sna_ref, snb_ref,
                    qkv_ref, z_ref, gb_ref, q_ref, kf_ref, kb_ref, vf_ref, vb_ref, vt_ref):
    x = x_ref[...]
    h = _rms(x, g_ref[...]).astype(BF16)
    q0 = CONV_DIM + KEY_DIM
    att = _dot(h, wm_ref[:, q0:])
    main = _dot(h, wm_ref[:, :q0])
    ab = _dot(h, wab_ref[...])
    qkv_ref[...] = main[:, :CONV_DIM]
    z_ref[...] = main[:, CONV_DIM:CONV_DIM + KEY_DIM]
    xa = ab + dtb_ref[...]
    softplus = jnp.maximum(xa, 0.0) + jnp.log1p(jnp.exp(-jnp.abs(xa)))
    lane = lax.broadcasted_iota(jnp.int32, ab.shape, 1)
    gb_ref[...] = jnp.where(lane < HEADS, -jnp.exp(alog_ref[...]) * softplus, _sigmoid(ab))
    cos, sna, snb = cos_ref[...], sna_ref[...], snb_ref[...]
    krs, vhs = [], []
    for hd in range(HEADS):
        sl = slice(hd * HEAD_DIM, (hd + 1) * HEAD_DIM)
        qh = att[:, hd * HEAD_DIM:(hd + 1) * HEAD_DIM]
        kh = att[:, KEY_DIM + hd * HEAD_DIM:KEY_DIM + (hd + 1) * HEAD_DIM]
        vh = att[:, 2 * KEY_DIM + hd * HEAD_DIM:2 * KEY_DIM + (hd + 1) * HEAD_DIM]
        qr = qh * cos + pltpu.roll(qh, LANES - ROT_DIM // 2, 1) * sna + pltpu.roll(qh, ROT_DIM // 2, 1) * snb
        kr = kh * cos + pltpu.roll(kh, LANES - ROT_DIM // 2, 1) * sna + pltpu.roll(kh, ROT_DIM // 2, 1) * snb
        q_ref[:, sl] = (qr * (DQK ** -0.5 * math.log2(math.e))).astype(BF16)
        krs.append(kr)
        vhs.append(vh)
        kb_ref[:, sl] = kr.astype(BF16)
        vb_ref[:, sl] = vh.astype(BF16)
        vt_ref[0, sl, :] = vh.T.astype(BF16)
    kf_ref[...] = jnp.concatenate(krs, axis=1).reshape(kf_ref.shape)
    vf_ref[...] = jnp.concatenate(vhs, axis=1).reshape(vf_ref.shape)


def _rope_tables(pos):
    inv = ROPE_THETA ** (-np.arange(0, ROT_DIM, 2, dtype=np.float64) / ROT_DIM)
    ang = pos.astype(np.float64)[:, None] * inv[None, :]
    cos, sin = np.cos(ang), np.sin(ang)
    pad = np.zeros((pos.shape[0], DQK - ROT_DIM))
    cos64 = np.concatenate([cos, cos, pad + 1.0], axis=1)
    sna64 = np.concatenate([-sin, np.zeros_like(sin), pad], axis=1)
    snb64 = np.concatenate([np.zeros_like(sin), sin, pad], axis=1)
    return tuple(jnp.asarray(np.concatenate([a, a], axis=1), F32) for a in (cos64, sna64, snb64))


def _proj_in(x2d, pos_rows, tab_rows, w, tm):
    n = x2d.shape[0]
    nt = tab_rows // tm
    cos, sna, snb = _rope_tables(pos_rows)
    row = lambda i: (i, 0)
    fix = lambda i: (0, 0)
    tab = lambda i: (i % nt, 0)
    out_shape = (
        jax.ShapeDtypeStruct((n, CONV_DIM), F32),
        jax.ShapeDtypeStruct((n, KEY_DIM), F32),
        jax.ShapeDtypeStruct((n, LANES), F32),
        jax.ShapeDtypeStruct((n, KEY_DIM), BF16),
        jax.ShapeDtypeStruct((n, HEADS, HEAD_DIM), F32),
        jax.ShapeDtypeStruct((n, KEY_DIM), BF16),
        jax.ShapeDtypeStruct((n, HEADS, HEAD_DIM), F32),
        jax.ShapeDtypeStruct((n, KEY_DIM), BF16),
        jax.ShapeDtypeStruct((n // tm, KEY_DIM, tm), BF16),
    )

    def spec(sd):
        if len(sd.shape) == 2:
            return pl.BlockSpec((tm, sd.shape[1]), row)
        return pl.BlockSpec((sd.shape[0] * tm // n,) + sd.shape[1:], lambda i: (i, 0, 0))
    out_specs = tuple(spec(sd) for sd in out_shape)
    return pl.pallas_call(
        _proj_in_kernel,
        grid=(n // tm,),
        in_specs=[
            pl.BlockSpec((tm, D_MODEL), row),
            pl.BlockSpec((1, D_MODEL), fix),
            pl.BlockSpec((D_MODEL, MAIN_COLS), fix),
            pl.BlockSpec((D_MODEL, LANES), fix),
            pl.BlockSpec((1, LANES), fix),
            pl.BlockSpec((1, LANES), fix),
            pl.BlockSpec((tm, LANES), tab),
            pl.BlockSpec((tm, LANES), tab),
            pl.BlockSpec((tm, LANES), tab),
        ],
        out_specs=out_specs,
        out_shape=out_shape,
        compiler_params=_params("parallel"),
        name="proj_in",
    )(x2d, w["norm_mix_g"], w["w_main"], w["w_ab"], w["a_log"], w["dt_bias"], cos, sna, snb)


def _gdn_kernel(qkv_ref, z_ref, gb_ref, cprev_ref, s0_ref, cw_ref, ng_ref,
                o_ref, sout_ref, xp_ref, y_ref, s_ref, *, c, nchunks):
    j = pl.program_id(1)
    rows = c * nchunks
    hc = HEADS * c

    @pl.when(j == 0)
    def _():
        xp_ref[0:CONV_PAD, :] = cprev_ref[0]
        s_ref[...] = s0_ref[0]

    xp_ref[CONV_PAD:CONV_PAD + rows, :] = qkv_ref[0]
    cw = cw_ref[...]
    xp = xp_ref[...]
    y = xp[CONV_PAD:] * cw[CONV_W - 1:CONV_W, :]
    for i in range(CONV_W - 1):
        y = y + pltpu.roll(xp, CONV_W - 1 - i, 0)[CONV_PAD:] * cw[i:i + 1, :]
    xp_ref[0:CONV_PAD, :] = xp[rows:rows + CONV_PAD]
    y_ref[...] = y * _sigmoid(y)

    ri = lax.broadcasted_iota(jnp.int32, (hc, hc), 0)
    cj = lax.broadcasted_iota(jnp.int32, (hc, hc), 1)
    same = (ri // c) == (cj // c)
    tril_bd = same & (ri >= cj)
    strict_bd = same & (ri > cj)
    eye_bd = (ri == cj).astype(F32)
    r1 = lax.broadcasted_iota(jnp.int32, (c, c), 0)
    c1 = lax.broadcasted_iota(jnp.int32, (c, c), 1)
    tril_c = (r1 >= c1).astype(F32)
    triu_c = (r1 <= c1).astype(F32)
    ng = ng_ref[...]
    gb_all = gb_ref[0]
    if rows % LANES:
        gb_all = jnp.concatenate([gb_all, jnp.zeros((LANES - rows % LANES, LANES), F32)], axis=0)
    gbt_all = gb_all.T[0:2 * HEADS, :]

    def prep(ci):
        rs = slice(ci * c, (ci + 1) * c)
        gcol = gb_ref[0, rs, :]
        gc_col = _dot(tril_c, gcol, HIGHEST)
        gc_row = _dot(gbt_all[:, rs], triu_c, HIGHEST)
        ks, kbs, qs, vbs, kbgs, qgs, kds, dcol, grow, glast = [], [], [], [], [], [], [], [], [], []
        for hd in range(HEADS):
            qh = y_ref[rs, hd * HEAD_DIM:(hd + 1) * HEAD_DIM]
            kh = y_ref[rs, KEY_DIM + hd * HEAD_DIM:KEY_DIM + (hd + 1) * HEAD_DIM]
            vh = y_ref[rs, 2 * KEY_DIM + hd * HEAD_DIM:2 * KEY_DIM + (hd + 1) * HEAD_DIM]
            qh = qh * lax.rsqrt(jnp.sum(qh * qh, axis=-1, keepdims=True) + NORM_EPS) * (HEAD_DIM ** -0.5)
            kh = kh * lax.rsqrt(jnp.sum(kh * kh, axis=-1, keepdims=True) + NORM_EPS)
            g_h = gc_col[:, hd:hd + 1]
            beta_h = gcol[:, HEADS + hd:HEADS + hd + 1]
            g_last = gc_col[c - 1:c, hd:hd + 1]
            eg = jnp.exp(g_h)
            kb = kh * beta_h
            ks.append(kh)
            kbs.append(kb)
            qs.append(qh)
            vbs.append(vh * beta_h)
            kbgs.append(kb * eg)
            qgs.append((qh * eg).astype(BF16))
            kds.append((kh * jnp.exp(g_last - g_h)).astype(BF16))
            glast.append(jnp.exp(g_last))
            dcol.append(jnp.broadcast_to(g_h, (c, hc)))
            grow.append(gc_row[hd:hd + 1, :])
        k_s = jnp.concatenate(ks, axis=0).astype(BF16)
        q_s = jnp.concatenate(qs, axis=0).astype(BF16)
        dmat = jnp.concatenate(dcol, axis=0) - jnp.concatenate(grow, axis=1)
        decay = jnp.exp(jnp.where(tril_bd, dmat, -jnp.inf))
        a_kk = _dot_nt(jnp.concatenate(kbs, axis=0).astype(BF16), k_s)
        x = jnp.where(strict_bd, -(a_kk * decay), 0.0)
        rhs = jnp.concatenate([jnp.concatenate(vbs, axis=0), jnp.concatenate(kbgs, axis=0)], axis=1)
        a_qk = (_dot_nt(q_s, k_s) * decay).astype(BF16)
        return dict(x=x, rhs=rhs.astype(BF16), a_qk=a_qk, qg=qgs, kd=kds, glast=glast)

    pre = [prep(ci) for ci in range(nchunks)]
    ps = [eye_bd + d["x"] for d in pre]
    xps = [d["x"] for d in pre]
    for _ in range(int(math.log2(c)) - 1):
        xbs = [xp.astype(BF16) for xp in xps]
        xps = [_dot(xb, xb) for xb in xbs]
        ps = [p + _dot(p.astype(BF16), xp.astype(BF16)) for p, xp in zip(ps, xps)]
    uws = [_dot(p.astype(BF16), d["rhs"]) for p, d in zip(ps, pre)]

    state = [s_ref[hd] for hd in range(HEADS)]
    for ci in range(nchunks):
        rs = slice(ci * c, (ci + 1) * c)
        d, uw = pre[ci], uws[ci]
        vnews, qss = [], []
        for hd in range(HEADS):
            hs = slice(hd * c, (hd + 1) * c)
            wq = jnp.concatenate([uw[hs, HEAD_DIM:].astype(BF16), d["qg"][hd]], axis=0)
            ws = _dot(wq, state[hd].astype(BF16))
            v_new = uw[hs, :HEAD_DIM] - ws[:c]
            vnews.append(v_new)
            qss.append(ws[c:])
            state[hd] = state[hd] * d["glast"][hd] + _dot_tn(d["kd"][hd], v_new.astype(BF16))
        o_s = jnp.concatenate(qss, axis=0) + _dot(d["a_qk"], jnp.concatenate(vnews, axis=0).astype(BF16))
        for hd in range(HEADS):
            sl = slice(hd * HEAD_DIM, (hd + 1) * HEAD_DIM)
            zh = z_ref[0, rs, sl]
            o_h = _rms(o_s[hd * c:(hd + 1) * c], ng) * (zh * _sigmoid(zh))
            o_ref[0, rs, sl] = o_h.astype(BF16)
    for hd in range(HEADS):
        s_ref[hd] = state[hd]

    @pl.when(j == pl.num_programs(1) - 1)
    def _():
        sout_ref[0] = s_ref[...]


def _gdn(qkv, z, gb, conv_prev, s0, w, nchunks):
    b, t, _ = qkv.shape
    c = min(t, CHUNK)
    rows = c * nchunks
    cprev = jnp.concatenate([jnp.zeros((b, CONV_PAD - (CONV_W - 1), CONV_DIM), F32), conv_prev], axis=1)
    blk = lambda i, j: (i, j, 0)
    kern = functools.partial(_gdn_kernel, c=c, nchunks=nchunks)
    return pl.pallas_call(
        kern,
        grid=(b, t // rows),
        in_specs=[
            pl.BlockSpec((1, rows, CONV_DIM), blk),
            pl.BlockSpec((1, rows, KEY_DIM), blk),
            pl.BlockSpec((1, rows, LANES), blk),
            pl.BlockSpec((1, CONV_PAD, CONV_DIM), lambda i, j: (i, 0, 0)),
            pl.BlockSpec((1, HEADS, HEAD_DIM, HEAD_DIM), lambda i, j: (i, 0, 0, 0)),
            pl.BlockSpec((CONV_W, CONV_DIM), lambda i, j: (0, 0)),
            pl.BlockSpec((1, HEAD_DIM), lambda i, j: (0, 0)),
        ],
        out_specs=(
            pl.BlockSpec((1, rows, KEY_DIM), blk),
            pl.BlockSpec((1, HEADS, HEAD_DIM, HEAD_DIM), lambda i, j: (i, 0, 0, 0)),
        ),
        out_shape=(
            jax.ShapeDtypeStruct((b, t, KEY_DIM), BF16),
            jax.ShapeDtypeStruct((b, HEADS, HEAD_DIM, HEAD_DIM), F32),
        ),
        scratch_shapes=[
            pltpu.VMEM((rows + CONV_PAD, CONV_DIM), F32),
            pltpu.VMEM((rows, CONV_DIM), F32),
            pltpu.VMEM((HEADS, HEAD_DIM, HEAD_DIM), F32),
        ],
        compiler_params=_params("parallel", "arbitrary"),
        name="gdn",
    )(qkv, z, gb, cprev, s0, w["conv_w"], w["gdn_norm_g"])


def _diff_attn_kernel(q_ref, qn_ref, k_ref, vt_ref, lam_ref, ng_ref, o_ref,
                      sz_ref, s0_ref, s1_ref, m_ref, acc_ref, qst_ref,
                      *, tq, tqp, tk, nblk, nq, hps, ahead, n_keys, q_pos0, lam_init):
    qi = pl.program_id(2)
    width = 2 * tqp
    heads = range(hps)
    lanes = lambda hd: slice(hd * HEAD_DIM, (hd + 1) * HEAD_DIM)
    pos_lo = q_pos0 + qi * tq
    k_lo = jnp.minimum((pos_lo // CHUNK + 1) * CHUNK, n_keys)
    k_hi = jnp.minimum(((pos_lo + tq - 1) // CHUNK + 1) * CHUNK, n_keys)
    n_full = k_lo // tk
    last = (k_hi + tk - 1) // tk - 1

    def scores(jb, s_ref):
        jl = jnp.minimum(jb, nblk - 1)
        ks = pl.ds(pl.multiple_of(jl * tk, tk), tk)
        for hd in heads:
            s_ref[hd, :, 0:width] = _dot(k_ref[0, ks, lanes(hd)], qst_ref[hd])

    def start(qx_ref):
        for hd in heads:
            q = qx_ref[0, :, lanes(hd)]
            if tqp > tq:
                q = jnp.concatenate([q, jnp.zeros((tqp - tq, HEAD_DIM), q.dtype)], axis=0)
            lane = lax.broadcasted_iota(jnp.int32, q.shape, 1)
            zero = jnp.zeros_like(q)
            qs = jnp.concatenate([jnp.where(lane < DQK, q, zero), jnp.where(lane >= DQK, q, zero)], axis=0)
            qst_ref[hd] = qs.T
        m_ref[...] = jnp.full(m_ref.shape, -jnp.inf, F32)
        acc_ref[...] = jnp.zeros(acc_ref.shape, F32)
        scores(0, sz_ref)

    def apply_mask(jb, s_ref):
        col = lax.broadcasted_iota(jnp.int32, (tk, width), 1)
        qpos = pos_lo + jnp.where(col >= tqp, col - tqp, col)
        kpos = jb * tk + lax.broadcasted_iota(jnp.int32, (tk, width), 0)
        vis = kpos < jnp.minimum((qpos // CHUNK + 1) * CHUNK, n_keys)
        for hd in heads:
            s_ref[hd, :, 0:width] = jnp.where(vis, s_ref[hd, :, 0:width], -jnp.inf)

    def mask(jb, s_ref):
        @pl.when(jb >= n_full)
        def _():
            apply_mask(jb, s_ref)

    def softmax_pv(jb, s_ref):
        ones = jnp.ones((SUM_ROWS, tk), BF16)
        for hd in heads:
            st = s_ref[hd, :, 0:width]
            m_old = m_ref[hd]
            m_new = jnp.maximum(m_old, jnp.max(st, axis=0, keepdims=True))
            alpha = jnp.exp2(m_old - m_new)
            p = jnp.exp2(st - m_new)
            vt1 = jnp.concatenate([vt_ref[jnp.minimum(jb, nblk - 1), lanes(hd), :], ones], axis=0)
            acc_ref[hd] = alpha * acc_ref[hd] + _dot(vt1, p.astype(BF16))
            m_ref[hd] = m_new

    def first_block():
        mask(0, sz_ref)
        scores(1, s1_ref)
        softmax_pv(0, sz_ref)

    @pl.when(qi == 0)
    def _():
        start(q_ref)

    @pl.when((last >= 1) & ((qi == 0) if ahead else True))
    def _():
        first_block()

    def full_pair(i, carry):
        j = 1 + 2 * i
        scores(j + 1, s0_ref)
        softmax_pv(j, s1_ref)
        scores(j + 2, s1_ref)
        softmax_pv(j + 1, s0_ref)
        return carry

    def full_quad(i, carry):
        full_pair(2 * i, carry)
        return full_pair(2 * i + 1, carry)

    def full_octo(i, carry):
        full_quad(2 * i, carry)
        return full_quad(2 * i + 1, carry)

    def edge_pair(i, carry):
        j = 1 + 2 * i
        mask(j, s1_ref)
        scores(j + 1, s0_ref)
        softmax_pv(j, s1_ref)

        @pl.when(j + 1 < last)
        def _():
            mask(j + 1, s0_ref)
            scores(j + 2, s1_ref)
            softmax_pv(j + 1, s0_ref)
        return carry

    n_free = jnp.maximum(jnp.minimum(n_full, last) - 1, 0)
    lax.fori_loop(0, n_free // 8, full_octo, 0)
    lax.fori_loop(2 * (n_free // 8), n_free // 4, full_quad, 0)
    lax.fori_loop(2 * (n_free // 4), n_free // 2, full_pair, 0)
    lax.fori_loop(n_free // 2, last // 2, edge_pair, 0)

    def finish(s_ref, more):
        apply_mask(last, s_ref)
        softmax_pv(last, s_ref)
        outs = [acc_ref[hd, 0:HEAD_DIM, :] * (1.0 / acc_ref[hd, HEAD_DIM:HEAD_DIM + 1, :]) for hd in heads]
        if more:
            start(qn_ref)
            if ahead:
                scores(1, s1_ref)
                softmax_pv(0, sz_ref)
        lf = lam_ref[...]
        lam = (jnp.exp(jnp.sum(lf[0:1] * lf[1:2], axis=-1, keepdims=True))
               - jnp.exp(jnp.sum(lf[2:3] * lf[3:4], axis=-1, keepdims=True)) + lam_init)
        for hd in heads:
            od = outs[hd][:, :tqp] - lam * outs[hd][:, tqp:]
            ms = jnp.mean(od * od, axis=0, keepdims=True)
            on = od * lax.rsqrt(ms + NORM_EPS) * ng_ref[...] * (1.0 - lam_init)
            o_ref[0, :, lanes(hd)] = on.T[:tq].astype(BF16)

    for more in ((True, False) if nq > 1 else (False,)):
        step_ok = (qi < nq - 1) if more else (qi == nq - 1)

        @pl.when(step_ok & (last == 0))
        def _():
            finish(sz_ref, more)

        @pl.when(step_ok & (last >= 1) & (last % 2 == 1))
        def _():
            finish(s1_ref, more)

        @pl.when(step_ok & (last >= 1) & (last % 2 == 0))
        def _():
            finish(s0_ref, more)


def _diff_attn(q, k, vt, w, n_keys, q_pos0, tq, tk):
    b, t, _ = q.shape
    tkp = k.shape[1]
    nblk = tkp // tk
    nq = t // tq
    tqp = max(tq, LANES)
    hps = HEADS if nq == 1 else 1
    wide = hps * HEAD_DIM
    lam_init = 0.8 - 0.6 * math.exp(-0.3 * 0)

    def bounds(qi):
        k_lo = min(((q_pos0 + qi * tq) // CHUNK + 1) * CHUNK, n_keys)
        k_hi = min(((q_pos0 + qi * tq + tq - 1) // CHUNK + 1) * CHUNK, n_keys)
        return k_lo // tk, -(-k_hi // tk) - 1
    ahead = nq > 1 and all(min(bounds(qi)) >= 1 for qi in range(1, nq))
    kern = functools.partial(_diff_attn_kernel, tq=tq, tqp=tqp, tk=tk, nblk=nblk, nq=nq, hps=hps, ahead=ahead,
                             n_keys=n_keys, q_pos0=q_pos0, lam_init=lam_init)
    return pl.pallas_call(
        kern,
        grid=(b, HEADS // hps, nq),
        in_specs=[
            pl.BlockSpec((1, tq, wide), lambda i, h, j: (i, j, h)),
            pl.BlockSpec((1, tq, wide), lambda i, h, j: (i, jnp.minimum(j + 1, nq - 1), h)),
            pl.BlockSpec((1, tkp, wide), lambda i, h, j: (i, 0, h)),
            pl.BlockSpec((nblk, wide, tk), lambda i, h, j: (i, h, 0)),
            pl.BlockSpec((4, DQK), lambda i, h, j: (0, 0)),
            pl.BlockSpec((HEAD_DIM, 1), lambda i, h, j: (0, 0)),
        ],
        out_specs=pl.BlockSpec((1, tq, wide), lambda i, h, j: (i, j, h)),
        out_shape=jax.ShapeDtypeStruct((b, t, KEY_DIM), BF16),
        scratch_shapes=[
            pltpu.VMEM((hps, tk, 2 * tqp + LANES), F32),
            pltpu.VMEM((hps, tk, 2 * tqp + LANES), F32),
            pltpu.VMEM((hps, tk, 2 * tqp + LANES), F32),
            pltpu.VMEM((hps, 1, 2 * tqp), F32),
            pltpu.VMEM((hps, HEAD_DIM + SUM_ROWS, 2 * tqp), F32),
            pltpu.VMEM((hps, HEAD_DIM, 2 * tqp), BF16),
        ],
        compiler_params=_params("arbitrary", "arbitrary", "arbitrary"),
        name="diff_attn",
    )(q, q, k, vt, w["diff_lambda"], w["diff_norm_g_col"])


def _kv_cache_kernel(ck_ref, cv_ref, kn_ref, vn_ref, k_ref, vt_ref, *, p_len, t_new, pad_rows):
    k_ref[0, 0:p_len, :] = ck_ref[0].reshape(p_len, KEY_DIM).astype(BF16)
    zeros = jnp.zeros((pad_rows - t_new, KEY_DIM), BF16)
    k_ref[0, p_len:p_len + pad_rows, :] = jnp.concatenate([kn_ref[0], zeros], axis=0)
    vt_ref[0, :, 0:p_len] = cv_ref[0].reshape(p_len, KEY_DIM).T.astype(BF16)
    v_new = jnp.concatenate([vn_ref[0], zeros], axis=0).astype(F32)
    vt_ref[0, :, p_len:p_len + pad_rows] = v_new.T.astype(BF16)


def _kv_cache(cache_k, cache_v, k_new, v_new):
    b, p_len = cache_k.shape[:2]
    t_new = k_new.shape[1]
    assert p_len % LANES == 0, p_len
    pad_rows = -(-t_new // LANES) * LANES
    rows = p_len + pad_rows
    kern = functools.partial(_kv_cache_kernel, p_len=p_len, t_new=t_new, pad_rows=pad_rows)
    return pl.pallas_call(
        kern,
        grid=(b,),
        in_specs=[
            pl.BlockSpec((1, p_len, HEADS, HEAD_DIM), lambda i: (i, 0, 0, 0)),
            pl.BlockSpec((1, p_len, HEADS, HEAD_DIM), lambda i: (i, 0, 0, 0)),
            pl.BlockSpec((1, t_new, KEY_DIM), lambda i: (i, 0, 0)),
            pl.BlockSpec((1, t_new, KEY_DIM), lambda i: (i, 0, 0)),
        ],
        out_specs=(pl.BlockSpec((1, rows, KEY_DIM), lambda i: (i, 0, 0)),
                   pl.BlockSpec((1, KEY_DIM, rows), lambda i: (i, 0, 0))),
        out_shape=(jax.ShapeDtypeStruct((b, rows, KEY_DIM), BF16),
                   jax.ShapeDtypeStruct((b, KEY_DIM, rows), BF16)),
        compiler_params=_params("parallel"),
        name="kv_cache",
    )(cache_k, cache_v, k_new, v_new)


def _post_mix_kernel(x_ref, og_ref, od_ref, mk_ref, mv_ref, wo_ref, g_ref, wq_ref, wmo_ref, o_ref):
    nb, tm, _ = x_ref.shape
    rows = lambda ref: ref[...].reshape(nb * tm, ref.shape[-1])
    x1 = rows(x_ref) + _dot(rows(og_ref), wo_ref[0:KEY_DIM, :]) + _dot(rows(od_ref), wo_ref[KEY_DIM:, :])
    h2 = _rms(x1, g_ref[...]).astype(BF16)
    qm = _dot(h2, wq_ref[...]).astype(BF16)
    oms = []
    for bi in range(nb):
        outs = []
        for hd in range(HEADS):
            sl = slice(hd * HEAD_DIM, (hd + 1) * HEAD_DIM)
            mk = mk_ref[bi, :, sl]
            mv = mv_ref[bi, :, sl]
            s = _dot_nt(qm[bi * tm:(bi + 1) * tm, sl], mk) * (HEAD_DIM ** -0.5)
            p = jnp.exp(s - jnp.max(s, axis=-1, keepdims=True))
            l = jnp.sum(p, axis=-1, keepdims=True)
            outs.append(_dot((p / l).astype(BF16), mv))
        oms.append(jnp.concatenate(outs, axis=1))
    om = jnp.concatenate(oms, axis=0).astype(BF16)
    o_ref[...] = (x1 + _dot(om, wmo_ref[...])).reshape(o_ref.shape)


def _post_mix(x, og, od, mk, mv, w, tm):
    b, t, _ = x.shape
    nb = _tile(b, max(1, ROW_TILE // tm)) if tm == t else 1
    blk = lambda i, j: (i, j, 0)
    fix = lambda i, j: (0, 0)
    mem = lambda i, j: (i, 0, 0)
    return pl.pallas_call(
        _post_mix_kernel,
        grid=(b // nb, t // tm),
        in_specs=[
            pl.BlockSpec((nb, tm, D_MODEL), blk),
            pl.BlockSpec((nb, tm, KEY_DIM), blk),
            pl.BlockSpec((nb, tm, KEY_DIM), blk),
            pl.BlockSpec((nb, N_MEM, KEY_DIM), mem),
            pl.BlockSpec((nb, N_MEM, KEY_DIM), mem),
            pl.BlockSpec((D_MODEL, D_MODEL), fix),
            pl.BlockSpec((1, D_MODEL), fix),
            pl.BlockSpec((D_MODEL, KEY_DIM), fix),
            pl.BlockSpec((KEY_DIM, D_MODEL), fix),
        ],
        out_specs=pl.BlockSpec((nb, tm, D_MODEL), blk),
        out_shape=jax.ShapeDtypeStruct((b, t, D_MODEL), F32),
        compiler_params=_params("parallel", "parallel"),
        name="post_mix",
    )(x, og, od, mk, mv, w["w_out"], w["norm_mem_g"], w["w_mq"], w["w_mo"])


def _mlp_kernel(x_ref, g_ref, wu_ref, wd_ref, fg_ref, o_ref, *, ff_blk):
    x = x_ref[...]
    hf = _rms(x, g_ref[...]).astype(BF16)
    acc = x
    for c0 in range(0, D_FF, ff_blk):
        u = jnp.maximum(_dot(hf, wu_ref[:, c0:c0 + ff_blk]), 0.0)
        acc = acc + _dot((u * u).astype(BF16), wd_ref[c0:c0 + ff_blk, :])
    o_ref[...] = _rms(acc, fg_ref[...])


def _mlp(x2d, w, tm):
    n = x2d.shape[0]
    row = lambda i: (i, 0)
    fix = lambda i: (0, 0)
    return pl.pallas_call(
        functools.partial(_mlp_kernel, ff_blk=D_MODEL),
        grid=(n // tm,),
        in_specs=[
            pl.BlockSpec((tm, D_MODEL), row),
            pl.BlockSpec((1, D_MODEL), fix),
            pl.BlockSpec((D_MODEL, D_FF), fix, pipeline_mode=pl.Buffered(1)),
            pl.BlockSpec((D_FF, D_MODEL), fix, pipeline_mode=pl.Buffered(1)),
            pl.BlockSpec((1, D_MODEL), fix),
        ],
        out_specs=pl.BlockSpec((tm, D_MODEL), row),
        out_shape=jax.ShapeDtypeStruct((n, D_MODEL), F32),
        compiler_params=_params("parallel"),
        name="mlp",
    )(x2d, w["norm_ffn_g"], w["w_up"], w["w_down"], w["final_norm_g"])


def _mem_kv_kernel(m_ref, g_ref, w_ref, k_ref, v_ref, kb_ref, vb_ref):
    mh = _rms(m_ref[...], g_ref[...]).astype(BF16)
    kv = _dot(mh, w_ref[...])
    kb_ref[...] = kv[:, :KEY_DIM].astype(BF16)
    vb_ref[...] = kv[:, KEY_DIM:].astype(BF16)
    k_ref[...] = kv[:, :KEY_DIM].reshape(k_ref.shape)
    v_ref[...] = kv[:, KEY_DIM:].reshape(v_ref.shape)


def _mem_kv(mem2d, w, tm):
    n = mem2d.shape[0]
    row = lambda i: (i, 0)
    fix = lambda i: (0, 0)
    return pl.pallas_call(
        _mem_kv_kernel,
        grid=(n // tm,),
        in_specs=[
            pl.BlockSpec((tm, D_MODEL), row),
            pl.BlockSpec((1, D_MODEL), fix),
            pl.BlockSpec((D_MODEL, 2 * KEY_DIM), fix),
        ],
        out_specs=(pl.BlockSpec((tm, HEADS, HEAD_DIM), lambda i: (i, 0, 0)),) * 2
        + (pl.BlockSpec((tm, KEY_DIM), row),) * 2,
        out_shape=(jax.ShapeDtypeStruct((n, HEADS, HEAD_DIM), F32),) * 2
        + (jax.ShapeDtypeStruct((n, KEY_DIM), BF16),) * 2,
        compiler_params=_params("parallel"),
        name="mem_kv",
    )(mem2d, w["mem_norm_g"], w["w_mkv"])


def _tile(n, pref):
    t = min(n, pref)
    assert n % t == 0, (n, t)
    return t


def _layer(x, pos0, conv_prev, s0, kv_prev, mem_k, mem_v, w):
    b, t, _ = x.shape
    n = b * t
    c = min(t, CHUNK)
    x2d = x.reshape(n, D_MODEL)
    tm = _tile(n, ROW_TILE)
    tab_rows = t if t % tm == 0 else n
    pos_rows = pos0 + (np.arange(tab_rows) % t)
    qkv, z, gb, qd, kf, kb, vf, vb, vt = _proj_in(x2d, pos_rows, tab_rows, w, tm)
    r3 = lambda a: a.reshape(b, t, -1)
    qkv3 = r3(qkv)
    og, s_new = _gdn(qkv3, r3(z), r3(gb), conv_prev, s0, w, nchunks=_tile(t // c, GDN_CHUNKS))
    conv_new = qkv3[:, t - (CONV_W - 1):, :]
    kb3 = r3(kb)
    if kv_prev is None:
        assert t % tm == 0, (t, tm)
        n_keys, tk = t, tm
    else:
        n_keys = kv_prev[0].shape[1] + t
        kb3, vt = _kv_cache(kv_prev[0], kv_prev[1], kb3, r3(vb))
        tk = kb3.shape[1]
    od = _diff_attn(r3(qd), kb3, vt, w, n_keys=n_keys, q_pos0=pos0, tq=_tile(t, ROW_TILE), tk=tk)
    x2 = _post_mix(x, og, od, mem_k, mem_v, w, tm=_tile(t, ROW_TILE))
    y = _mlp(x2.reshape(n, D_MODEL), w, _tile(n, 2 * ROW_TILE)).reshape(b, t, D_MODEL)
    h4 = lambda a: a.reshape(b, t, HEADS, HEAD_DIM)
    return y, s_new, conv_new, h4(kf), h4(vf)


def kernel(x_prompt, x_sample, mem_prompt, cache_diff_k, cache_diff_v, cache_mem_k, cache_mem_v, state_gdn, state_gdn_conv, norm_mix_g, w_in, gdn_conv_w, gdn_a_log, gdn_dt_bias, gdn_norm_g, diff_lambda, diff_norm_g, w_out, norm_mem_g, mem_norm_g, w_mq, w_mkv, w_mo, norm_ffn_g, w_up, w_down, final_norm_g):
    bp, tp, _ = x_prompt.shape
    bs, ts, _ = x_sample.shape
    p_len = cache_diff_k.shape[2]
    depth = w_in.shape[0]
    assert depth == 1
    l = 0
    wi = w_in[l]
    sp = [CONV_DIM, CONV_DIM + KEY_DIM, CONV_DIM + KEY_DIM + HEADS, CONV_DIM + KEY_DIM + 2 * HEADS]
    w_ab = jnp.concatenate([wi[:, sp[1]:sp[3]], jnp.zeros((D_MODEL, LANES - 2 * HEADS), F32)], axis=1)
    lanes_pad = lambda a: jnp.concatenate([a, jnp.zeros((LANES - a.shape[0],), F32)])[None, :]
    row = lambda a: a.reshape(1, -1)
    w = {
        "norm_mix_g": row(norm_mix_g[l]),
        "w_main": jnp.concatenate([wi[:, :sp[1]], wi[:, sp[3]:]], axis=1).astype(BF16),
        "w_ab": w_ab.astype(BF16),
        "a_log": lanes_pad(gdn_a_log[l]),
        "dt_bias": lanes_pad(gdn_dt_bias[l]),
        "conv_w": gdn_conv_w[l],
        "gdn_norm_g": row(gdn_norm_g[l]),
        "diff_lambda": diff_lambda[l],
        "diff_norm_g_col": diff_norm_g[l].reshape(HEAD_DIM, 1),
        "w_out": w_out[l].astype(BF16),
        "norm_mem_g": row(norm_mem_g[l]),
        "mem_norm_g": row(mem_norm_g[l]),
        "w_mq": w_mq[l].astype(BF16),
        "w_mkv": w_mkv[l].astype(BF16),
        "w_mo": w_mo[l].astype(BF16),
        "norm_ffn_g": row(norm_ffn_g[l]),
        "w_up": w_up[l].astype(BF16),
        "w_down": w_down[l].astype(BF16),
        "final_norm_g": row(final_norm_g),
    }
    n_mem = mem_prompt.shape[1]
    mk, mv, mkb, mvb = _mem_kv(mem_prompt.reshape(bp * n_mem, D_MODEL), w, _tile(bp * n_mem, ROW_TILE))
    mk = mk.reshape(bp, n_mem, HEADS, HEAD_DIM)
    mv = mv.reshape(bp, n_mem, HEADS, HEAD_DIM)
    mkb = mkb.reshape(bp, n_mem, KEY_DIM)
    mvb = mvb.reshape(bp, n_mem, KEY_DIM)

    zeros_conv = jnp.zeros((bp, CONV_W - 1, CONV_DIM), F32)
    zeros_state = jnp.zeros((bp, HEADS, HEAD_DIM, HEAD_DIM), F32)
    yp, sp_, cp, kp, vp = _layer(x_prompt, 0, zeros_conv, zeros_state, None, mkb, mvb, w)
    ys, ss, cs, ks_, vs = _layer(x_sample, p_len, state_gdn_conv[l], state_gdn[l],
                                 (cache_diff_k[l], cache_diff_v[l]),
                                 cache_mem_k[l].reshape(bs, n_mem, KEY_DIM).astype(BF16),
                                 cache_mem_v[l].reshape(bs, n_mem, KEY_DIM).astype(BF16), w)
    return (yp, ys, sp_[None], cp[None], kp[None], vp[None], mk[None], mv[None],
            ss[None], cs[None], ks_[None], vs[None])
```

```python
import functools
import math

import jax
import jax.numpy as jnp
import numpy as np
from jax import lax
from jax.experimental import pallas as pl
from jax.experimental.pallas import tpu as pltpu

D_MODEL = 1024
CHUNK = 64
HEADS = 4
HEAD_DIM = 128
KEY_DIM = HEADS * HEAD_DIM
CONV_DIM = 3 * KEY_DIM
CONV_W = 4
DQK = 64
ROT_DIM = 16
ROPE_THETA = 500000.0
N_MEM = 256
D_FF = 4 * D_MODEL
NORM_EPS = 1e-6
MAIN_COLS = CONV_DIM + 4 * KEY_DIM
LANES = 128
CONV_PAD = 8
SUM_ROWS = 16
VMEM_LIMIT = 56 * 1024 * 1024
ROW_TILE = 512
GDN_CHUNKS = 8

F32 = jnp.float32
BF16 = jnp.bfloat16
HIGHEST = lax.Precision.HIGHEST


def _dot(a, b, precision=None):
    return jnp.dot(a, b, preferred_element_type=F32, precision=precision)


def _dot_nt(a, b):
    return lax.dot_general(a, b, (((1,), (1,)), ((), ())), preferred_element_type=F32)


def _dot_tn(a, b):
    return lax.dot_general(a, b, (((0,), (0,)), ((), ())), preferred_element_type=F32)


def _rms(x, g):
    return x * lax.rsqrt(jnp.mean(x * x, axis=-1, keepdims=True) + NORM_EPS) * g


def _sigmoid(x):
    return 1.0 / (1.0 + jnp.exp(-x))


def _params(*sem):
    return pltpu.CompilerParams(dimension_semantics=sem, vmem_limit_bytes=VMEM_LIMIT)


def _proj_in_kernel(x_ref, g_ref, wm_ref, wab_ref, alog_ref, dtb_ref, cos_ref, sna_ref, snb_ref,
                    qkv_ref, z_ref, gb_ref, q_ref, kf_ref, kb_ref, vf_ref, vb_ref, vt_ref):
    x = x_ref[...]
    h = _rms(x, g_ref[...]).astype(BF16)
    q0 = CONV_DIM + KEY_DIM
    att = _dot(h, wm_ref[:, q0:])
    main = _dot(h, wm_ref[:, :q0])
    ab = _dot(h, wab_ref[...])
    qkv_ref[...] = main[:, :CONV_DIM]
    z_ref[...] = main[:, CONV_DIM:CONV_DIM + KEY_DIM]
    xa = ab + dtb_ref[...]
    softplus = jnp.maximum(xa, 0.0) + jnp.log1p(jnp.exp(-jnp.abs(xa)))
    lane = lax.broadcasted_iota(jnp.int32, ab.shape, 1)
    gb_ref[...] = jnp.where(lane < HEADS, -jnp.exp(alog_ref[...]) * softplus, _sigmoid(ab))
    cos, sna, snb = cos_ref[...], sna_ref[...], snb_ref[...]
    krs, vhs = [], []
    for hd in range(HEADS):
        sl = slice(hd * HEAD_DIM, (hd + 1) * HEAD_DIM)
        qh = att[:, hd * HEAD_DIM:(hd + 1) * HEAD_DIM]
        kh = att[:, KEY_DIM + hd * HEAD_DIM:KEY_DIM + (hd + 1) * HEAD_DIM]
        vh = att[:, 2 * KEY_DIM + hd * HEAD_DIM:2 * KEY_DIM + (hd + 1) * HEAD_DIM]
        qr = qh * cos + pltpu.roll(qh, LANES - ROT_DIM // 2, 1) * sna + pltpu.roll(qh, ROT_DIM // 2, 1) * snb
        kr = kh * cos + pltpu.roll(kh, LANES - ROT_DIM // 2, 1) * sna + pltpu.roll(kh, ROT_DIM // 2, 1) * snb
        q_ref[:, sl] = (qr * (DQK ** -0.5 * math.log2(math.e))).astype(BF16)
        krs.append(kr)
        vhs.append(vh)
        kb_ref[:, sl] = kr.astype(BF16)
        vb_ref[:, sl] = vh.astype(BF16)
        vt_ref[0, sl, :] = vh.T.astype(BF16)
    kf_ref[...] = jnp.concatenate(krs, axis=1).reshape(kf_ref.shape)
    vf_ref[...] = jnp.concatenate(vhs, axis=1).reshape(vf_ref.shape)


def _rope_tables(pos):
    inv = ROPE_THETA ** (-np.arange(0, ROT_DIM, 2, dtype=np.float64) / ROT_DIM)
    ang = pos.astype(np.float64)[:, None] * inv[None, :]
    cos, sin = np.cos(ang), np.sin(ang)
    pad = np.zeros((pos.shape[0], DQK - ROT_DIM))
    cos64 = np.concatenate([cos, cos, pad + 1.0], axis=1)
    sna64 = np.concatenate([-sin, np.zeros_like(sin), pad], axis=1)
    snb64 = np.concatenate([np.zeros_like(sin), sin, pad], axis=1)
    return tuple(jnp.asarray(np.concatenate([a, a], axis=1), F32) for a in (cos64, sna64, snb64))


def _proj_in(x2d, pos_rows, tab_rows, w, tm):
    n = x2d.shape[0]
    nt = tab_rows // tm
    cos, sna, snb = _rope_tables(pos_rows)
    row = lambda i: (i, 0)
    fix = lambda i: (0, 0)
    tab = lambda i: (i % nt, 0)
    out_shape = (
        jax.ShapeDtypeStruct((n, CONV_DIM), F32),
        jax.ShapeDtypeStruct((n, KEY_DIM), F32),
        jax.ShapeDtypeStruct((n, LANES), F32),
        jax.ShapeDtypeStruct((n, KEY_DIM), BF16),
        jax.ShapeDtypeStruct((n, HEADS, HEAD_DIM), F32),
        jax.ShapeDtypeStruct((n, KEY_DIM), BF16),
        jax.ShapeDtypeStruct((n, HEADS, HEAD_DIM), F32),
        jax.ShapeDtypeStruct((n, KEY_DIM), BF16),
        jax.ShapeDtypeStruct((n // tm, KEY_DIM, tm), BF16),
    )

    def spec(sd):
        if len(sd.shape) == 2:
            return pl.BlockSpec((tm, sd.shape[1]), row)
        return pl.BlockSpec((sd.shape[0] * tm // n,) + sd.shape[1:], lambda i: (i, 0, 0))
    out_specs = tuple(spec(sd) for sd in out_shape)
    return pl.pallas_call(
        _proj_in_kernel,
        grid=(n // tm,),
        in_specs=[
            pl.BlockSpec((tm, D_MODEL), row),
            pl.BlockSpec((1, D_MODEL), fix),
            pl.BlockSpec((D_MODEL, MAIN_COLS), fix),
            pl.BlockSpec((D_MODEL, LANES), fix),
            pl.BlockSpec((1, LANES), fix),
            pl.BlockSpec((1, LANES), fix),
            pl.BlockSpec((tm, LANES), tab),
            pl.BlockSpec((tm, LANES), tab),
            pl.BlockSpec((tm, LANES), tab),
        ],
        out_specs=out_specs,
        out_shape=out_shape,
        compiler_params=_params("parallel"),
        name="proj_in",
    )(x2d, w["norm_mix_g"], w["w_main"], w["w_ab"], w["a_log"], w["dt_bias"], cos, sna, snb)


def _gdn_kernel(qkv_ref, z_ref, gb_ref, cprev_ref, s0_ref, cw_ref, ng_ref,
                o_ref, sout_ref, xp_ref, y_ref, s_ref, *, c, nchunks):
    j = pl.program_id(1)
    rows = c * nchunks
    hc = HEADS * c

    @pl.when(j == 0)
    def _():
        xp_ref[0:CONV_PAD, :] = cprev_ref[0]
        s_ref[...] = s0_ref[0]

    xp_ref[CONV_PAD:CONV_PAD + rows, :] = qkv_ref[0]
    cw = cw_ref[...]
    xp = xp_ref[...]
    y = xp[CONV_PAD:] * cw[CONV_W - 1:CONV_W, :]
    for i in range(CONV_W - 1):
        y = y + pltpu.roll(xp, CONV_W - 1 - i, 0)[CONV_PAD:] * cw[i:i + 1, :]
    xp_ref[0:CONV_PAD, :] = xp[rows:rows + CONV_PAD]
    y_ref[...] = y * _sigmoid(y)

    ri = lax.broadcasted_iota(jnp.int32, (hc, hc), 0)
    cj = lax.broadcasted_iota(jnp.int32, (hc, hc), 1)
    same = (ri // c) == (cj // c)
    tril_bd = same & (ri >= cj)
    strict_bd = same & (ri > cj)
    eye_bd = (ri == cj).astype(F32)
    r1 = lax.broadcasted_iota(jnp.int32, (c, c), 0)
    c1 = lax.broadcasted_iota(jnp.int32, (c, c), 1)
    tril_c = (r1 >= c1).astype(F32)
    triu_c = (r1 <= c1).astype(F32)
    ng = ng_ref[...]
    gb_all = gb_ref[0]
    if rows % LANES:
        gb_all = jnp.concatenate([gb_all, jnp.zeros((LANES - rows % LANES, LANES), F32)], axis=0)
    gbt_all = gb_all.T[0:2 * HEADS, :]

    def prep(ci):
        rs = slice(ci * c, (ci + 1) * c)
        gcol = gb_ref[0, rs, :]
        gc_col = _dot(tril_c, gcol, HIGHEST)
        gc_row = _dot(gbt_all[:, rs], triu_c, HIGHEST)
        ks, kbs, qs, vbs, kbgs, qgs, kds, dcol, grow, glast = [], [], [], [], [], [], [], [], [], []
        for hd in range(HEADS):
            qh = y_ref[rs, hd * HEAD_DIM:(hd + 1) * HEAD_DIM]
            kh = y_ref[rs, KEY_DIM + hd * HEAD_DIM:KEY_DIM + (hd + 1) * HEAD_DIM]
            vh = y_ref[rs, 2 * KEY_DIM + hd * HEAD_DIM:2 * KEY_DIM + (hd + 1) * HEAD_DIM]
            qh = qh * lax.rsqrt(jnp.sum(qh * qh, axis=-1, keepdims=True) + NORM_EPS) * (HEAD_DIM ** -0.5)
            kh = kh * lax.rsqrt(jnp.sum(kh * kh, axis=-1, keepdims=True) + NORM_EPS)
            g_h = gc_col[:, hd:hd + 1]
            beta_h = gcol[:, HEADS + hd:HEADS + hd + 1]
            g_last = gc_col[c - 1:c, hd:hd + 1]
            eg = jnp.exp(g_h)
            kb = kh * beta_h
            ks.append(kh)
            kbs.append(kb)
            qs.append(qh)
            vbs.append(vh * beta_h)
            kbgs.append(kb * eg)
            qgs.append((qh * eg).astype(BF16))
            kds.append((kh * jnp.exp(g_last - g_h)).astype(BF16))
            glast.append(jnp.exp(g_last))
            dcol.append(jnp.broadcast_to(g_h, (c, hc)))
            grow.append(gc_row[hd:hd + 1, :])
        k_s = jnp.concatenate(ks, axis=0).astype(BF16)
        q_s = jnp.concatenate(qs, axis=0).astype(BF16)
        dmat = jnp.concatenate(dcol, axis=0) - jnp.concatenate(grow, axis=1)
        decay = jnp.exp(jnp.where(tril_bd, dmat, -jnp.inf))
        a_kk = _dot_nt(jnp.concatenate(kbs, axis=0).astype(BF16), k_s)
        x = jnp.where(strict_bd, -(a_kk * decay), 0.0)
        rhs = jnp.concatenate([jnp.concatenate(vbs, axis=0), jnp.concatenate(kbgs, axis=0)], axis=1)
        a_qk = (_dot_nt(q_s, k_s) * decay).astype(BF16)
        return dict(x=x, rhs=rhs.astype(BF16), a_qk=a_qk, qg=qgs, kd=kds, glast=glast)

    pre = [prep(ci) for ci in range(nchunks)]
    ps = [eye_bd + d["x"] for d in pre]
    xps = [d["x"] for d in pre]
    for _ in range(int(math.log2(c)) - 1):
        xbs = [xp.astype(BF16) for xp in xps]
        xps = [_dot(xb, xb) for xb in xbs]
        ps = [p + _dot(p.astype(BF16), xp.astype(BF16)) for p, xp in zip(ps, xps)]
    uws = [_dot(p.astype(BF16), d["rhs"]) for p, d in zip(ps, pre)]

    state = [s_ref[hd] for hd in range(HEADS)]
    for ci in range(nchunks):
        rs = slice(ci * c, (ci + 1) * c)
        d, uw = pre[ci], uws[ci]
        vnews, qss = [], []
        for hd in range(HEADS):
            hs = slice(hd * c, (hd + 1) * c)
            wq = jnp.concatenate([uw[hs, HEAD_DIM:].astype(BF16), d["qg"][hd]], axis=0)
            ws = _dot(wq, state[hd].astype(BF16))
            v_new = uw[hs, :HEAD_DIM] - ws[:c]
            vnews.append(v_new)
            qss.append(ws[c:])
            state[hd] = state[hd] * d["glast"][hd] + _dot_tn(d["kd"][hd], v_new.astype(BF16))
        o_s = jnp.concatenate(qss, axis=0) + _dot(d["a_qk"], jnp.concatenate(vnews, axis=0).astype(BF16))
        for hd in range(HEADS):
            sl = slice(hd * HEAD_DIM, (hd + 1) * HEAD_DIM)
            zh = z_ref[0, rs, sl]
            o_h = _rms(o_s[hd * c:(hd + 1) * c], ng) * (zh * _sigmoid(zh))
            o_ref[0, rs, sl] = o_h.astype(BF16)
    for hd in range(HEADS):
        s_ref[hd] = state[hd]

    @pl.when(j == pl.num_programs(1) - 1)
    def _():
        sout_ref[0] = s_ref[...]


def _gdn(qkv, z, gb, conv_prev, s0, w, nchunks):
    b, t, _ = qkv.shape
    c = min(t, CHUNK)
    rows = c * nchunks
    cprev = jnp.concatenate([jnp.zeros((b, CONV_PAD - (CONV_W - 1), CONV_DIM), F32), conv_prev], axis=1)
    blk = lambda i, j: (i, j, 0)
    kern = functools.partial(_gdn_kernel, c=c, nchunks=nchunks)
    return pl.pallas_call(
        kern,
        grid=(b, t // rows),
        in_specs=[
            pl.BlockSpec((1, rows, CONV_DIM), blk),
            pl.BlockSpec((1, rows, KEY_DIM), blk),
            pl.BlockSpec((1, rows, LANES), blk),
            pl.BlockSpec((1, CONV_PAD, CONV_DIM), lambda i, j: (i, 0, 0)),
            pl.BlockSpec((1, HEADS, HEAD_DIM, HEAD_DIM), lambda i, j: (i, 0, 0, 0)),
            pl.BlockSpec((CONV_W, CONV_DIM), lambda i, j: (0, 0)),
            pl.BlockSpec((1, HEAD_DIM), lambda i, j: (0, 0)),
        ],
        out_specs=(
            pl.BlockSpec((1, rows, KEY_DIM), blk),
            pl.BlockSpec((1, HEADS, HEAD_DIM, HEAD_DIM), lambda i, j: (i, 0, 0, 0)),
        ),
        out_shape=(
            jax.ShapeDtypeStruct((b, t, KEY_DIM), BF16),
            jax.ShapeDtypeStruct((b, HEADS, HEAD_DIM, HEAD_DIM), F32),
        ),
        scratch_shapes=[
            pltpu.VMEM((rows + CONV_PAD, CONV_DIM), F32),
            pltpu.VMEM((rows, CONV_DIM), F32),
            pltpu.VMEM((HEADS, HEAD_DIM, HEAD_DIM), F32),
        ],
        compiler_params=_params("parallel", "arbitrary"),
        name="gdn",
    )(qkv, z, gb, cprev, s0, w["conv_w"], w["gdn_norm_g"])


def _diff_attn_kernel(q_ref, qn_ref, k_ref, vt_ref, lam_ref, ng_ref, o_ref,
                      sz_ref, s0_ref, s1_ref, m_ref, acc_ref, qst_ref,
                      *, tq, tqp, tk, nblk, nq, hps, ahead, n_keys, q_pos0, lam_init):
    qi = pl.program_id(2)
    width = 2 * tqp
    heads = range(hps)
    lanes = lambda hd: slice(hd * HEAD_DIM, (hd + 1) * HEAD_DIM)
    pos_lo = q_pos0 + qi * tq
    k_lo = jnp.minimum((pos_lo // CHUNK + 1) * CHUNK, n_keys)
    k_hi = jnp.minimum(((pos_lo + tq - 1) // CHUNK + 1) * CHUNK, n_keys)
    n_full = k_lo // tk
    last = (k_hi + tk - 1) // tk - 1

    def scores(jb, s_ref):
        jl = jnp.minimum(jb, nblk - 1)
        ks = pl.ds(pl.multiple_of(jl * tk, tk), tk)
        for hd in heads:
            s_ref[hd, :, 0:width] = _dot(k_ref[0, ks, lanes(hd)], qst_ref[hd])

    def start(qx_ref):
        for hd in heads:
            q = qx_ref[0, :, lanes(hd)]
            if tqp > tq:
                q = jnp.concatenate([q, jnp.zeros((tqp - tq, HEAD_DIM), q.dtype)], axis=0)
            lane = lax.broadcasted_iota(jnp.int32, q.shape, 1)
            zero = jnp.zeros_like(q)
            qs = jnp.concatenate([jnp.where(lane < DQK, q, zero), jnp.where(lane >= DQK, q, zero)], axis=0)
            qst_ref[hd] = qs.T
        m_ref[...] = jnp.full(m_ref.shape, -jnp.inf, F32)
        acc_ref[...] = jnp.zeros(acc_ref.shape, F32)
        scores(0, sz_ref)

    def apply_mask(jb, s_ref):
        col = lax.broadcasted_iota(jnp.int32, (tk, width), 1)
        qpos = pos_lo + jnp.where(col >= tqp, col - tqp, col)
        kpos = jb * tk + lax.broadcasted_iota(jnp.int32, (tk, width), 0)
        vis = kpos < jnp.minimum((qpos // CHUNK + 1) * CHUNK, n_keys)
        for hd in heads:
            s_ref[hd, :, 0:width] = jnp.where(vis, s_ref[hd, :, 0:width], -jnp.inf)

    def mask(jb, s_ref):
        @pl.when(jb >= n_full)
        def _():
            apply_mask(jb, s_ref)

    def softmax_pv(jb, s_ref):
        ones = jnp.ones((SUM_ROWS, tk), BF16)
        for hd in heads:
            st = s_ref[hd, :, 0:width]
            m_old = m_ref[hd]
            m_new = jnp.maximum(m_old, jnp.max(st, axis=0, keepdims=True))
            alpha = jnp.exp2(m_old - m_new)
            p = jnp.exp2(st - m_new)
            vt1 = jnp.concatenate([vt_ref[jnp.minimum(jb, nblk - 1), lanes(hd), :], ones], axis=0)
            acc_ref[hd] = alpha * acc_ref[hd] + _dot(vt1, p.astype(BF16))
            m_ref[hd] = m_new

    def first_block():
        mask(0, sz_ref)
        scores(1, s1_ref)
        softmax_pv(0, sz_ref)

    @pl.when(qi == 0)
    def _():
        start(q_ref)

    @pl.when((last >= 1) & ((qi == 0) if ahead else True))
    def _():
        first_block()

    def full_pair(i, carry):
        j = 1 + 2 * i
        scores(j + 1, s0_ref)
        softmax_pv(j, s1_ref)
        scores(j + 2, s1_ref)
        softmax_pv(j + 1, s0_ref)
        return carry

    def full_quad(i, carry):
        full_pair(2 * i, carry)
        return full_pair(2 * i + 1, carry)

    def full_octo(i, carry):
        full_quad(2 * i, carry)
        return full_quad(2 * i + 1, carry)

    def edge_pair(i, carry):
        j = 1 + 2 * i
        mask(j, s1_ref)
        scores(j + 1, s0_ref)
        softmax_pv(j, s1_ref)

        @pl.when(j + 1 < last)
        def _():
            mask(j + 1, s0_ref)
            scores(j + 2, s1_ref)
            softmax_pv(j + 1, s0_ref)
        return carry

    n_free = jnp.maximum(jnp.minimum(n_full, last) - 1, 0)
    lax.fori_loop(0, n_free // 8, full_octo, 0)
    lax.fori_loop(2 * (n_free // 8), n_free // 4, full_quad, 0)
    lax.fori_loop(2 * (n_free // 4), n_free // 2, full_pair, 0)
    lax.fori_loop(n_free // 2, last // 2, edge_pair, 0)

    def finish(s_ref, more):
        apply_mask(last, s_ref)
        softmax_pv(last, s_ref)
        outs = [acc_ref[hd, 0:HEAD_DIM, :] * (1.0 / acc_ref[hd, HEAD_DIM:HEAD_DIM + 1, :]) for hd in heads]
        if more:
            start(qn_ref)
            if ahead:
                scores(1, s1_ref)
                softmax_pv(0, sz_ref)
        lf = lam_ref[...]
        lam = (jnp.exp(jnp.sum(lf[0:1] * lf[1:2], axis=-1, keepdims=True))
               - jnp.exp(jnp.sum(lf[2:3] * lf[3:4], axis=-1, keepdims=True)) + lam_init)
        for hd in heads:
            od = outs[hd][:, :tqp] - lam * outs[hd][:, tqp:]
            ms = jnp.mean(od * od, axis=0, keepdims=True)
            on = od * lax.rsqrt(ms + NORM_EPS) * ng_ref[...] * (1.0 - lam_init)
            o_ref[0, :, lanes(hd)] = on.T[:tq].astype(BF16)

    for more in ((True, False) if nq > 1 else (False,)):
        step_ok = (qi < nq - 1) if more else (qi == nq - 1)

        @pl.when(step_ok & (last == 0))
        def _():
            finish(sz_ref, more)

        @pl.when(step_ok & (last >= 1) & (last % 2 == 1))
        def _():
            finish(s1_ref, more)

        @pl.when(step_ok & (last >= 1) & (last % 2 == 0))
        def _():
            finish(s0_ref, more)


def _diff_attn(q, k, vt, w, n_keys, q_pos0, tq, tk):
    b, t, _ = q.shape
    tkp = k.shape[1]
    nblk = tkp // tk
    nq = t // tq
    tqp = max(tq, LANES)
    hps = HEADS if nq == 1 else 1
    wide = hps * HEAD_DIM
    lam_init = 0.8 - 0.6 * math.exp(-0.3 * 0)

    def bounds(qi):
        k_lo = min(((q_pos0 + qi * tq) // CHUNK + 1) * CHUNK, n_keys)
        k_hi = min(((q_pos0 + qi * tq + tq - 1) // CHUNK + 1) * CHUNK, n_keys)
        return k_lo // tk, -(-k_hi // tk) - 1
    ahead = nq > 1 and all(min(bounds(qi)) >= 1 for qi in range(1, nq))
    kern = functools.partial(_diff_attn_kernel, tq=tq, tqp=tqp, tk=tk, nblk=nblk, nq=nq, hps=hps, ahead=ahead,
                             n_keys=n_keys, q_pos0=q_pos0, lam_init=lam_init)
    return pl.pallas_call(
        kern,
        grid=(b, HEADS // hps, nq),
        in_specs=[
            pl.BlockSpec((1, tq, wide), lambda i, h, j: (i, j, h)),
            pl.BlockSpec((1, tq, wide), lambda i, h, j: (i, jnp.minimum(j + 1, nq - 1), h)),
            pl.BlockSpec((1, tkp, wide), lambda i, h, j: (i, 0, h)),
            pl.BlockSpec((nblk, wide, tk), lambda i, h, j: (i, h, 0)),
            pl.BlockSpec((4, DQK), lambda i, h, j: (0, 0)),
            pl.BlockSpec((HEAD_DIM, 1), lambda i, h, j: (0, 0)),
        ],
        out_specs=pl.BlockSpec((1, tq, wide), lambda i, h, j: (i, j, h)),
        out_shape=jax.ShapeDtypeStruct((b, t, KEY_DIM), BF16),
        scratch_shapes=[
            pltpu.VMEM((hps, tk, 2 * tqp + LANES), F32),
            pltpu.VMEM((hps, tk, 2 * tqp + LANES), F32),
            pltpu.VMEM((hps, tk, 2 * tqp + LANES), F32),
            pltpu.VMEM((hps, 1, 2 * tqp), F32),
            pltpu.VMEM((hps, HEAD_DIM + SUM_ROWS, 2 * tqp), F32),
            pltpu.VMEM((hps, HEAD_DIM, 2 * tqp), BF16),
        ],
        compiler_params=_params("arbitrary", "arbitrary", "arbitrary"),
        name="diff_attn",
    )(q, q, k, vt, w["diff_lambda"], w["diff_norm_g_col"])


def _kv_cache_kernel(ck_ref, cv_ref, kn_ref, vn_ref, k_ref, vt_ref, *, p_len, t_new, pad_rows):
    k_ref[0, 0:p_len, :] = ck_ref[0].reshape(p_len, KEY_DIM).astype(BF16)
    zeros = jnp.zeros((pad_rows - t_new, KEY_DIM), BF16)
    k_ref[0, p_len:p_len + pad_rows, :] = jnp.concatenate([kn_ref[0], zeros], axis=0)
    vt_ref[0, :, 0:p_len] = cv_ref[0].reshape(p_len, KEY_DIM).T.astype(BF16)
    v_new = jnp.concatenate([vn_ref[0], zeros], axis=0).astype(F32)
    vt_ref[0, :, p_len:p_len + pad_rows] = v_new.T.astype(BF16)


def _kv_cache(cache_k, cache_v, k_new, v_new):
    b, p_len = cache_k.shape[:2]
    t_new = k_new.shape[1]
    assert p_len % LANES == 0, p_len
    pad_rows = -(-t_new // LANES) * LANES
    rows = p_len + pad_rows
    kern = functools.partial(_kv_cache_kernel, p_len=p_len, t_new=t_new, pad_rows=pad_rows)
    return pl.pallas_call(
        kern,
        grid=(b,),
        in_specs=[
            pl.BlockSpec((1, p_len, HEADS, HEAD_DIM), lambda i: (i, 0, 0, 0)),
            pl.BlockSpec((1, p_len, HEADS, HEAD_DIM), lambda i: (i, 0, 0, 0)),
            pl.BlockSpec((1, t_new, KEY_DIM), lambda i: (i, 0, 0)),
            pl.BlockSpec((1, t_new, KEY_DIM), lambda i: (i, 0, 0)),
        ],
        out_specs=(pl.BlockSpec((1, rows, KEY_DIM), lambda i: (i, 0, 0)),
                   pl.BlockSpec((1, KEY_DIM, rows), lambda i: (i, 0, 0))),
        out_shape=(jax.ShapeDtypeStruct((b, rows, KEY_DIM), BF16),
                   jax.ShapeDtypeStruct((b, KEY_DIM, rows), BF16)),
        compiler_params=_params("parallel"),
        name="kv_cache",
    )(cache_k, cache_v, k_new, v_new)


def _post_mix_kernel(x_ref, og_ref, od_ref, mk_ref, mv_ref, wo_ref, g_ref, wq_ref, wmo_ref, o_ref):
    nb, tm, _ = x_ref.shape
    rows = lambda ref: ref[...].reshape(nb * tm, ref.shape[-1])
    x1 = rows(x_ref) + _dot(rows(og_ref), wo_ref[0:KEY_DIM, :]) + _dot(rows(od_ref), wo_ref[KEY_DIM:, :])
    h2 = _rms(x1, g_ref[...]).astype(BF16)
    qm = _dot(h2, wq_ref[...]).astype(BF16)
    oms = []
    for bi in range(nb):
        outs = []
        for hd in range(HEADS):
            sl = slice(hd * HEAD_DIM, (hd + 1) * HEAD_DIM)
            mk = mk_ref[bi, :, sl]
            mv = mv_ref[bi, :, sl]
            s = _dot_nt(qm[bi * tm:(bi + 1) * tm, sl], mk) * (HEAD_DIM ** -0.5)
            p = jnp.exp(s - jnp.max(s, axis=-1, keepdims=True))
            l = jnp.sum(p, axis=-1, keepdims=True)
            outs.append(_dot((p / l).astype(BF16), mv))
        oms.append(jnp.concatenate(outs, axis=1))
    om = jnp.concatenate(oms, axis=0).astype(BF16)
    o_ref[...] = (x1 + _dot(om, wmo_ref[...])).reshape(o_ref.shape)


def _post_mix(x, og, od, mk, mv, w, tm):
    b, t, _ = x.shape
    nb = _tile(b, max(1, ROW_TILE // tm)) if tm == t else 1
    blk = lambda i, j: (i, j, 0)
    fix = lambda i, j: (0, 0)
    mem = lambda i, j: (i, 0, 0)
    return pl.pallas_call(
        _post_mix_kernel,
        grid=(b // nb, t // tm),
        in_specs=[
            pl.BlockSpec((nb, tm, D_MODEL), blk),
            pl.BlockSpec((nb, tm, KEY_DIM), blk),
            pl.BlockSpec((nb, tm, KEY_DIM), blk),
            pl.BlockSpec((nb, N_MEM, KEY_DIM), mem),
            pl.BlockSpec((nb, N_MEM, KEY_DIM), mem),
            pl.BlockSpec((D_MODEL, D_MODEL), fix, pipeline_mode=pl.Buffered(1)),
            pl.BlockSpec((1, D_MODEL), fix),
            pl.BlockSpec((D_MODEL, KEY_DIM), fix, pipeline_mode=pl.Buffered(1)),
            pl.BlockSpec((KEY_DIM, D_MODEL), fix, pipeline_mode=pl.Buffered(1)),
        ],
        out_specs=pl.BlockSpec((nb, tm, D_MODEL), blk),
        out_shape=jax.ShapeDtypeStruct((b, t, D_MODEL), F32),
        compiler_params=_params("parallel", "parallel"),
        name="post_mix",
    )(x, og, od, mk, mv, w["w_out"], w["norm_mem_g"], w["w_mq"], w["w_mo"])


def _mlp_kernel(x_ref, g_ref, wu_ref, wd_ref, fg_ref, o_ref, *, ff_blk):
    x = x_ref[...]
    hf = _rms(x, g_ref[...]).astype(BF16)
    acc = x
    for c0 in range(0, D_FF, ff_blk):
        u = jnp.maximum(_dot(hf, wu_ref[:, c0:c0 + ff_blk]), 0.0)
        acc = acc + _dot((u * u).astype(BF16), wd_ref[c0:c0 + ff_blk, :])
    o_ref[...] = _rms(acc, fg_ref[...])


def _mlp(x2d, w, tm):
    n = x2d.shape[0]
    row = lambda i: (i, 0)
    fix = lambda i: (0, 0)
    return pl.pallas_call(
        functools.partial(_mlp_kernel, ff_blk=D_MODEL),
        grid=(n // tm,),
        in_specs=[
            pl.BlockSpec((tm, D_MODEL), row),
            pl.BlockSpec((1, D_MODEL), fix),
            pl.BlockSpec((D_MODEL, D_FF), fix, pipeline_mode=pl.Buffered(1)),
            pl.BlockSpec((D_FF, D_MODEL), fix, pipeline_mode=pl.Buffered(1)),
            pl.BlockSpec((1, D_MODEL), fix),
        ],
        out_specs=pl.BlockSpec((tm, D_MODEL), row),
        out_shape=jax.ShapeDtypeStruct((n, D_MODEL), F32),
        compiler_params=_params("parallel"),
        name="mlp",
    )(x2d, w["norm_ffn_g"], w["w_up"], w["w_down"], w["final_norm_g"])


def _mem_kv_kernel(m_ref, g_ref, w_ref, k_ref, v_ref, kb_ref, vb_ref):
    mh = _rms(m_ref[...], g_ref[...]).astype(BF16)
    kv = _dot(mh, w_ref[...])
    kb_ref[...] = kv[:, :KEY_DIM].astype(BF16)
    vb_ref[...] = kv[:, KEY_DIM:].astype(BF16)
    k_ref[...] = kv[:, :KEY_DIM].reshape(k_ref.shape)
    v_ref[...] = kv[:, KEY_DIM:].reshape(v_ref.shape)


def _mem_kv(mem2d, w, tm):
    n = mem2d.shape[0]
    row = lambda i: (i, 0)
    fix = lambda i: (0, 0)
    return pl.pallas_call(
        _mem_kv_kernel,
        grid=(n // tm,),
        in_specs=[
            pl.BlockSpec((tm, D_MODEL), row),
            pl.BlockSpec((1, D_MODEL), fix),
            pl.BlockSpec((D_MODEL, 2 * KEY_DIM), fix),
        ],
        out_specs=(pl.BlockSpec((tm, HEADS, HEAD_DIM), lambda i: (i, 0, 0)),) * 2
        + (pl.BlockSpec((tm, KEY_DIM), row),) * 2,
        out_shape=(jax.ShapeDtypeStruct((n, HEADS, HEAD_DIM), F32),) * 2
        + (jax.ShapeDtypeStruct((n, KEY_DIM), BF16),) * 2,
        compiler_params=_params("parallel"),
        name="mem_kv",
    )(mem2d, w["mem_norm_g"], w["w_mkv"])


def _tile(n, pref):
    t = min(n, pref)
    assert n % t == 0, (n, t)
    return t


def _layer(x, pos0, conv_prev, s0, kv_prev, mem_k, mem_v, w):
    b, t, _ = x.shape
    n = b * t
    c = min(t, CHUNK)
    x2d = x.reshape(n, D_MODEL)
    tm = _tile(n, ROW_TILE)
    tab_rows = t if t % tm == 0 else n
    pos_rows = pos0 + (np.arange(tab_rows) % t)
    qkv, z, gb, qd, kf, kb, vf, vb, vt = _proj_in(x2d, pos_rows, tab_rows, w, tm)
    r3 = lambda a: a.reshape(b, t, -1)
    qkv3 = r3(qkv)
    og, s_new = _gdn(qkv3, r3(z), r3(gb), conv_prev, s0, w, nchunks=_tile(t // c, GDN_CHUNKS))
    conv_new = qkv3[:, t - (CONV_W - 1):, :]
    kb3 = r3(kb)
    if kv_prev is None:
        assert t % tm == 0, (t, tm)
        n_keys, tk = t, tm
    else:
        n_keys = kv_prev[0].shape[1] + t
        kb3, vt = _kv_cache(kv_prev[0], kv_prev[1], kb3, r3(vb))
        tk = kb3.shape[1]
    od = _diff_attn(r3(qd), kb3, vt, w, n_keys=n_keys, q_pos0=pos0, tq=_tile(t, ROW_TILE), tk=tk)
    x2 = _post_mix(x, og, od, mem_k, mem_v, w, tm=_tile(t, 2 * ROW_TILE))
    y = _mlp(x2.reshape(n, D_MODEL), w, _tile(n, 2 * ROW_TILE)).reshape(b, t, D_MODEL)
    h4 = lambda a: a.reshape(b, t, HEADS, HEAD_DIM)
    return y, s_new, conv_new, h4(kf), h4(vf)


def kernel(x_prompt, x_sample, mem_prompt, cache_diff_k, cache_diff_v, cache_mem_k, cache_mem_v, state_gdn, state_gdn_conv, norm_mix_g, w_in, gdn_conv_w, gdn_a_log, gdn_dt_bias, gdn_norm_g, diff_lambda, diff_norm_g, w_out, norm_mem_g, mem_norm_g, w_mq, w_mkv, w_mo, norm_ffn_g, w_up, w_down, final_norm_g):
    bp, tp, _ = x_prompt.shape
    bs, ts, _ = x_sample.shape
    p_len = cache_diff_k.shape[2]
    depth = w_in.shape[0]
    assert depth == 1
    l = 0
    wi = w_in[l]
    sp = [CONV_DIM, CONV_DIM + KEY_DIM, CONV_DIM + KEY_DIM + HEADS, CONV_DIM + KEY_DIM + 2 * HEADS]
    w_ab = jnp.concatenate([wi[:, sp[1]:sp[3]], jnp.zeros((D_MODEL, LANES - 2 * HEADS), F32)], axis=1)
    lanes_pad = lambda a: jnp.concatenate([a, jnp.zeros((LANES - a.shape[0],), F32)])[None, :]
    row = lambda a: a.reshape(1, -1)
    w = {
        "norm_mix_g": row(norm_mix_g[l]),
        "w_main": jnp.concatenate([wi[:, :sp[1]], wi[:, sp[3]:]], axis=1).astype(BF16),
        "w_ab": w_ab.astype(BF16),
        "a_log": lanes_pad(gdn_a_log[l]),
        "dt_bias": lanes_pad(gdn_dt_bias[l]),
        "conv_w": gdn_conv_w[l],
        "gdn_norm_g": row(gdn_norm_g[l]),
        "diff_lambda": diff_lambda[l],
        "diff_norm_g_col": diff_norm_g[l].reshape(HEAD_DIM, 1),
        "w_out": w_out[l].astype(BF16),
        "norm_mem_g": row(norm_mem_g[l]),
        "mem_norm_g": row(mem_norm_g[l]),
        "w_mq": w_mq[l].astype(BF16),
        "w_mkv": w_mkv[l].astype(BF16),
        "w_mo": w_mo[l].astype(BF16),
        "norm_ffn_g": row(norm_ffn_g[l]),
        "w_up": w_up[l].astype(BF16),
        "w_down": w_down[l].astype(BF16),
        "final_norm_g": row(final_norm_g),
    }
    n_mem = mem_prompt.shape[1]
    mk, mv, mkb, mvb = _mem_kv(mem_prompt.reshape(bp * n_mem, D_MODEL), w, _tile(bp * n_mem, ROW_TILE))
    mk = mk.reshape(bp, n_mem, HEADS, HEAD_DIM)
    mv = mv.reshape(bp, n_mem, HEADS, HEAD_DIM)
    mkb = mkb.reshape(bp, n_mem, KEY_DIM)
    mvb = mvb.reshape(bp, n_mem, KEY_DIM)

    zeros_conv = jnp.zeros((bp, CONV_W - 1, CONV_DIM), F32)
    zeros_state = jnp.zeros((bp, HEADS, HEAD_DIM, HEAD_DIM), F32)
    yp, sp_, cp, kp, vp = _layer(x_prompt, 0, zeros_conv, zeros_state, None, mkb, mvb, w)
    ys, ss, cs, ks_, vs = _layer(x_sample, p_len, state_gdn_conv[l], state_gdn[l],
                                 (cache_diff_k[l], cache_diff_v[l]),
                                 cache_mem_k[l].reshape(bs, n_mem, KEY_DIM).astype(BF16),
                                 cache_mem_v[l].reshape(bs, n_mem, KEY_DIM).astype(BF16), w)
    return (yp, ys, sp_[None], cp[None], kp[None], vp[None], mk[None], mv[None],
            ss[None], cs[None], ks_[None], vs[None])
```

```python
import functools
import math

import jax
import jax.numpy as jnp
import numpy as np
from jax import lax
from jax.experimental import pallas as pl
from jax.experimental.pallas import tpu as pltpu

D_MODEL = 1024
CHUNK = 64
HEADS = 4
HEAD_DIM = 128
KEY_DIM = HEADS * HEAD_DIM
CONV_DIM = 3 * KEY_DIM
CONV_W = 4
DQK = 64
ROT_DIM = 16
ROPE_THETA = 500000.0
N_MEM = 256
D_FF = 4 * D_MODEL
NORM_EPS = 1e-6
MAIN_COLS = CONV_DIM + 4 * KEY_DIM
LANES = 128
CONV_PAD = 8
SUM_ROWS = 16
VMEM_LIMIT = 56 * 1024 * 1024
ROW_TILE = 512
GDN_CHUNKS = 16

F32 = jnp.float32
BF16 = jnp.bfloat16
HIGHEST = lax.Precision.HIGHEST


def _dot(a, b, precision=None):
    return jnp.dot(a, b, preferred_element_type=F32, precision=precision)


def _dot_nt(a, b):
    return lax.dot_general(a, b, (((1,), (1,)), ((), ())), preferred_element_type=F32)


def _dot_tn(a, b):
    return lax.dot_general(a, b, (((0,), (0,)), ((), ())), preferred_element_type=F32)


def _rms(x, g):
    return x * lax.rsqrt(jnp.mean(x * x, axis=-1, keepdims=True) + NORM_EPS) * g


def _sigmoid(x):
    return 1.0 / (1.0 + jnp.exp(-x))


def _params(*sem):
    return pltpu.CompilerParams(dimension_semantics=sem, vmem_limit_bytes=VMEM_LIMIT)


def _proj_in_kernel(x_ref, g_ref, wm_ref, wab_ref, alog_ref, dtb_ref, cos_ref, sna_ref, snb_ref,
                    qkv_ref, z_ref, gb_ref, q_ref, kf_ref, kb_ref, vf_ref, vb_ref, vt_ref):
    x = x_ref[...]
    h = _rms(x, g_ref[...]).astype(BF16)
    q0 = CONV_DIM + KEY_DIM
    att = _dot(h, wm_ref[:, q0:])
    main = _dot(h, wm_ref[:, :q0])
    ab = _dot(h, wab_ref[...])
    qkv_ref[...] = main[:, :CONV_DIM]
    z_ref[...] = main[:, CONV_DIM:CONV_DIM + KEY_DIM]
    xa = ab + dtb_ref[...]
    softplus = jnp.maximum(xa, 0.0) + jnp.log1p(jnp.exp(-jnp.abs(xa)))
    lane = lax.broadcasted_iota(jnp.int32, ab.shape, 1)
    gb_ref[...] = jnp.where(lane < HEADS, -jnp.exp(alog_ref[...]) * softplus, _sigmoid(ab))
    cos, sna, snb = cos_ref[...], sna_ref[...], snb_ref[...]
    krs, vhs = [], []
    for hd in range(HEADS):
        sl = slice(hd * HEAD_DIM, (hd + 1) * HEAD_DIM)
        qh = att[:, hd * HEAD_DIM:(hd + 1) * HEAD_DIM]
        kh = att[:, KEY_DIM + hd * HEAD_DIM:KEY_DIM + (hd + 1) * HEAD_DIM]
        vh = att[:, 2 * KEY_DIM + hd * HEAD_DIM:2 * KEY_DIM + (hd + 1) * HEAD_DIM]
        qr = qh * cos + pltpu.roll(qh, LANES - ROT_DIM // 2, 1) * sna + pltpu.roll(qh, ROT_DIM // 2, 1) * snb
        kr = kh * cos + pltpu.roll(kh, LANES - ROT_DIM // 2, 1) * sna + pltpu.roll(kh, ROT_DIM // 2, 1) * snb
        q_ref[:, sl] = (qr * (DQK ** -0.5 * math.log2(math.e))).astype(BF16)
        krs.append(kr)
        vhs.append(vh)
        kb_ref[:, sl] = kr.astype(BF16)
        vb_ref[:, sl] = vh.astype(BF16)
        vt_ref[0, sl, :] = vh.T.astype(BF16)
    kf_ref[...] = jnp.concatenate(krs, axis=1).reshape(kf_ref.shape)
    vf_ref[...] = jnp.concatenate(vhs, axis=1).reshape(vf_ref.shape)


def _rope_tables(pos):
    inv = ROPE_THETA ** (-np.arange(0, ROT_DIM, 2, dtype=np.float64) / ROT_DIM)
    ang = pos.astype(np.float64)[:, None] * inv[None, :]
    cos, sin = np.cos(ang), np.sin(ang)
    pad = np.zeros((pos.shape[0], DQK - ROT_DIM))
    cos64 = np.concatenate([cos, cos, pad + 1.0], axis=1)
    sna64 = np.concatenate([-sin, np.zeros_like(sin), pad], axis=1)
    snb64 = np.concatenate([np.zeros_like(sin), sin, pad], axis=1)
    return tuple(jnp.asarray(np.concatenate([a, a], axis=1), F32) for a in (cos64, sna64, snb64))


def _proj_in(x2d, pos_rows, tab_rows, w, tm):
    n = x2d.shape[0]
    nt = tab_rows // tm
    cos, sna, snb = _rope_tables(pos_rows)
    row = lambda i: (i, 0)
    fix = lambda i: (0, 0)
    tab = lambda i: (i % nt, 0)
    out_shape = (
        jax.ShapeDtypeStruct((n, CONV_DIM), F32),
        jax.ShapeDtypeStruct((n, KEY_DIM), F32),
        jax.ShapeDtypeStruct((n, LANES), F32),
        jax.ShapeDtypeStruct((n, KEY_DIM), BF16),
        jax.ShapeDtypeStruct((n, HEADS, HEAD_DIM), F32),
        jax.ShapeDtypeStruct((n, KEY_DIM), BF16),
        jax.ShapeDtypeStruct((n, HEADS, HEAD_DIM), F32),
        jax.ShapeDtypeStruct((n, KEY_DIM), BF16),
        jax.ShapeDtypeStruct((n // tm, KEY_DIM, tm), BF16),
    )

    def spec(sd):
        if len(sd.shape) == 2:
            return pl.BlockSpec((tm, sd.shape[1]), row)
        return pl.BlockSpec((sd.shape[0] * tm // n,) + sd.shape[1:], lambda i: (i, 0, 0))
    out_specs = tuple(spec(sd) for sd in out_shape)
    return pl.pallas_call(
        _proj_in_kernel,
        grid=(n // tm,),
        in_specs=[
            pl.BlockSpec((tm, D_MODEL), row),
            pl.BlockSpec((1, D_MODEL), fix),
            pl.BlockSpec((D_MODEL, MAIN_COLS), fix),
            pl.BlockSpec((D_MODEL, LANES), fix),
            pl.BlockSpec((1, LANES), fix),
            pl.BlockSpec((1, LANES), fix),
            pl.BlockSpec((tm, LANES), tab),
            pl.BlockSpec((tm, LANES), tab),
            pl.BlockSpec((tm, LANES), tab),
        ],
        out_specs=out_specs,
        out_shape=out_shape,
        compiler_params=_params("parallel"),
        name="proj_in",
    )(x2d, w["norm_mix_g"], w["w_main"], w["w_ab"], w["a_log"], w["dt_bias"], cos, sna, snb)


def _gdn_kernel(qkv_ref, z_ref, gb_ref, cprev_ref, s0_ref, cw_ref, ng_ref,
                o_ref, sout_ref, xp_ref, y_ref, s_ref, *, c, nchunks):
    j = pl.program_id(1)
    rows = c * nchunks
    hc = HEADS * c

    @pl.when(j == 0)
    def _():
        xp_ref[0:CONV_PAD, :] = cprev_ref[0]
        s_ref[...] = s0_ref[0]

    xp_ref[CONV_PAD:CONV_PAD + rows, :] = qkv_ref[0]
    cw = cw_ref[...]
    xp = xp_ref[...]
    y = xp[CONV_PAD:] * cw[CONV_W - 1:CONV_W, :]
    for i in range(CONV_W - 1):
        y = y + pltpu.roll(xp, CONV_W - 1 - i, 0)[CONV_PAD:] * cw[i:i + 1, :]
    xp_ref[0:CONV_PAD, :] = xp[rows:rows + CONV_PAD]
    y_ref[...] = y * _sigmoid(y)

    ri = lax.broadcasted_iota(jnp.int32, (hc, hc), 0)
    cj = lax.broadcasted_iota(jnp.int32, (hc, hc), 1)
    same = (ri // c) == (cj // c)
    tril_bd = same & (ri >= cj)
    strict_bd = same & (ri > cj)
    eye_bd = (ri == cj).astype(F32)
    r1 = lax.broadcasted_iota(jnp.int32, (c, c), 0)
    c1 = lax.broadcasted_iota(jnp.int32, (c, c), 1)
    tril_c = (r1 >= c1).astype(F32)
    triu_c = (r1 <= c1).astype(F32)
    ng = ng_ref[...]
    gb_all = gb_ref[0]
    if rows % LANES:
        gb_all = jnp.concatenate([gb_all, jnp.zeros((LANES - rows % LANES, LANES), F32)], axis=0)
    gbt_all = gb_all.T[0:2 * HEADS, :]

    def prep(ci):
        rs = slice(ci * c, (ci + 1) * c)
        gcol = gb_ref[0, rs, :]
        gc_col = _dot(tril_c, gcol, HIGHEST)
        gc_row = _dot(gbt_all[:, rs], triu_c, HIGHEST)
        ks, kbs, qs, vbs, kbgs, qgs, kds, dcol, grow, glast = [], [], [], [], [], [], [], [], [], []
        for hd in range(HEADS):
            qh = y_ref[rs, hd * HEAD_DIM:(hd + 1) * HEAD_DIM]
            kh = y_ref[rs, KEY_DIM + hd * HEAD_DIM:KEY_DIM + (hd + 1) * HEAD_DIM]
            vh = y_ref[rs, 2 * KEY_DIM + hd * HEAD_DIM:2 * KEY_DIM + (hd + 1) * HEAD_DIM]
            qh = qh * lax.rsqrt(jnp.sum(qh * qh, axis=-1, keepdims=True) + NORM_EPS) * (HEAD_DIM ** -0.5)
            kh = kh * lax.rsqrt(jnp.sum(kh * kh, axis=-1, keepdims=True) + NORM_EPS)
            g_h = gc_col[:, hd:hd + 1]
            beta_h = gcol[:, HEADS + hd:HEADS + hd + 1]
            g_last = gc_col[c - 1:c, hd:hd + 1]
            eg = jnp.exp(g_h)
            kb = kh * beta_h
            ks.append(kh)
            kbs.append(kb)
            qs.append(qh)
            vbs.append(vh * beta_h)
            kbgs.append(kb * eg)
            qgs.append((qh * eg).astype(BF16))
            kds.append((kh * jnp.exp(g_last - g_h)).astype(BF16))
            glast.append(jnp.exp(g_last))
            dcol.append(jnp.broadcast_to(g_h, (c, hc)))
            grow.append(gc_row[hd:hd + 1, :])
        k_s = jnp.concatenate(ks, axis=0).astype(BF16)
        q_s = jnp.concatenate(qs, axis=0).astype(BF16)
        dmat = jnp.concatenate(dcol, axis=0) - jnp.concatenate(grow, axis=1)
        decay = jnp.exp(jnp.where(tril_bd, dmat, -jnp.inf))
        a_kk = _dot_nt(jnp.concatenate(kbs, axis=0).astype(BF16), k_s)
        x = jnp.where(strict_bd, -(a_kk * decay), 0.0)
        rhs = jnp.concatenate([jnp.concatenate(vbs, axis=0), jnp.concatenate(kbgs, axis=0)], axis=1)
        a_qk = (_dot_nt(q_s, k_s) * decay).astype(BF16)
        return dict(x=x, rhs=rhs.astype(BF16), a_qk=a_qk, qg=qgs, kd=kds, glast=glast)

    pre = [prep(ci) for ci in range(nchunks)]
    ps = [eye_bd + d["x"] for d in pre]
    xps = [d["x"] for d in pre]
    for _ in range(int(math.log2(c)) - 1):
        xbs = [xp.astype(BF16) for xp in xps]
        xps = [_dot(xb, xb) for xb in xbs]
        ps = [p + _dot(p.astype(BF16), xp.astype(BF16)) for p, xp in zip(ps, xps)]
    uws = [_dot(p.astype(BF16), d["rhs"]) for p, d in zip(ps, pre)]

    state = [s_ref[hd] for hd in range(HEADS)]
    for ci in range(nchunks):
        rs = slice(ci * c, (ci + 1) * c)
        d, uw = pre[ci], uws[ci]
        vnews, qss = [], []
        for hd in range(HEADS):
            hs = slice(hd * c, (hd + 1) * c)
            wq = jnp.concatenate([uw[hs, HEAD_DIM:].astype(BF16), d["qg"][hd]], axis=0)
            ws = _dot(wq, state[hd].astype(BF16))
            v_new = uw[hs, :HEAD_DIM] - ws[:c]
            vnews.append(v_new)
            qss.append(ws[c:])
            state[hd] = state[hd] * d["glast"][hd] + _dot_tn(d["kd"][hd], v_new.astype(BF16))
        o_s = jnp.concatenate(qss, axis=0) + _dot(d["a_qk"], jnp.concatenate(vnews, axis=0).astype(BF16))
        for hd in range(HEADS):
            sl = slice(hd * HEAD_DIM, (hd + 1) * HEAD_DIM)
            zh = z_ref[0, rs, sl]
            o_h = _rms(o_s[hd * c:(hd + 1) * c], ng) * (zh * _sigmoid(zh))
            o_ref[0, rs, sl] = o_h.astype(BF16)
    for hd in range(HEADS):
        s_ref[hd] = state[hd]

    @pl.when(j == pl.num_programs(1) - 1)
    def _():
        sout_ref[0] = s_ref[...]


def _gdn(qkv, z, gb, conv_prev, s0, w, nchunks):
    b, t, _ = qkv.shape
    c = min(t, CHUNK)
    rows = c * nchunks
    cprev = jnp.concatenate([jnp.zeros((b, CONV_PAD - (CONV_W - 1), CONV_DIM), F32), conv_prev], axis=1)
    blk = lambda i, j: (i, j, 0)
    kern = functools.partial(_gdn_kernel, c=c, nchunks=nchunks)
    return pl.pallas_call(
        kern,
        grid=(b, t // rows),
        in_specs=[
            pl.BlockSpec((1, rows, CONV_DIM), blk),
            pl.BlockSpec((1, rows, KEY_DIM), blk),
            pl.BlockSpec((1, rows, LANES), blk),
            pl.BlockSpec((1, CONV_PAD, CONV_DIM), lambda i, j: (i, 0, 0)),
            pl.BlockSpec((1, HEADS, HEAD_DIM, HEAD_DIM), lambda i, j: (i, 0, 0, 0)),
            pl.BlockSpec((CONV_W, CONV_DIM), lambda i, j: (0, 0)),
            pl.BlockSpec((1, HEAD_DIM), lambda i, j: (0, 0)),
        ],
        out_specs=(
            pl.BlockSpec((1, rows, KEY_DIM), blk),
            pl.BlockSpec((1, HEADS, HEAD_DIM, HEAD_DIM), lambda i, j: (i, 0, 0, 0)),
        ),
        out_shape=(
            jax.ShapeDtypeStruct((b, t, KEY_DIM), BF16),
            jax.ShapeDtypeStruct((b, HEADS, HEAD_DIM, HEAD_DIM), F32),
        ),
        scratch_shapes=[
            pltpu.VMEM((rows + CONV_PAD, CONV_DIM), F32),
            pltpu.VMEM((rows, CONV_DIM), F32),
            pltpu.VMEM((HEADS, HEAD_DIM, HEAD_DIM), F32),
        ],
        compiler_params=_params("parallel", "arbitrary"),
        name="gdn",
    )(qkv, z, gb, cprev, s0, w["conv_w"], w["gdn_norm_g"])


def _diff_attn_kernel(q_ref, qn_ref, k_ref, vt_ref, lam_ref, ng_ref, o_ref,
                      sz_ref, s0_ref, s1_ref, m_ref, acc_ref, qst_ref,
                      *, tq, tqp, tk, nblk, nq, hps, ahead, n_keys, q_pos0, lam_init):
    qi = pl.program_id(2)
    width = 2 * tqp
    heads = range(hps)
    lanes = lambda hd: slice(hd * HEAD_DIM, (hd + 1) * HEAD_DIM)
    pos_lo = q_pos0 + qi * tq
    k_lo = jnp.minimum((pos_lo // CHUNK + 1) * CHUNK, n_keys)
    k_hi = jnp.minimum(((pos_lo + tq - 1) // CHUNK + 1) * CHUNK, n_keys)
    n_full = k_lo // tk
    last = (k_hi + tk - 1) // tk - 1

    def scores(jb, s_ref):
        jl = jnp.minimum(jb, nblk - 1)
        ks = pl.ds(pl.multiple_of(jl * tk, tk), tk)
        for hd in heads:
            s_ref[hd, :, 0:width] = _dot(k_ref[0, ks, lanes(hd)], qst_ref[hd])

    def start(qx_ref):
        for hd in heads:
            q = qx_ref[0, :, lanes(hd)]
            if tqp > tq:
                q = jnp.concatenate([q, jnp.zeros((tqp - tq, HEAD_DIM), q.dtype)], axis=0)
            lane = lax.broadcasted_iota(jnp.int32, q.shape, 1)
            zero = jnp.zeros_like(q)
            qs = jnp.concatenate([jnp.where(lane < DQK, q, zero), jnp.where(lane >= DQK, q, zero)], axis=0)
            qst_ref[hd] = qs.T
        m_ref[...] = jnp.full(m_ref.shape, -jnp.inf, F32)
        acc_ref[...] = jnp.zeros(acc_ref.shape, F32)
        scores(0, sz_ref)

    def apply_mask(jb, s_ref):
        col = lax.broadcasted_iota(jnp.int32, (tk, width), 1)
        qpos = pos_lo + jnp.where(col >= tqp, col - tqp, col)
        kpos = jb * tk + lax.broadcasted_iota(jnp.int32, (tk, width), 0)
        vis = kpos < jnp.minimum((qpos // CHUNK + 1) * CHUNK, n_keys)
        for hd in heads:
            s_ref[hd, :, 0:width] = jnp.where(vis, s_ref[hd, :, 0:width], -jnp.inf)

    def mask(jb, s_ref):
        @pl.when(jb >= n_full)
        def _():
            apply_mask(jb, s_ref)

    def softmax_pv(jb, s_ref):
        ones = jnp.ones((SUM_ROWS, tk), BF16)
        for hd in heads:
            st = s_ref[hd, :, 0:width]
            m_old = m_ref[hd]
            m_new = jnp.maximum(m_old, jnp.max(st, axis=0, keepdims=True))
            alpha = jnp.exp2(m_old - m_new)
            p = jnp.exp2(st - m_new)
            vt1 = jnp.concatenate([vt_ref[jnp.minimum(jb, nblk - 1), lanes(hd), :], ones], axis=0)
            acc_ref[hd] = alpha * acc_ref[hd] + _dot(vt1, p.astype(BF16))
            m_ref[hd] = m_new

    def first_block():
        mask(0, sz_ref)
        scores(1, s1_ref)
        softmax_pv(0, sz_ref)

    @pl.when(qi == 0)
    def _():
        start(q_ref)

    @pl.when((last >= 1) & ((qi == 0) if ahead else True))
    def _():
        first_block()

    def full_pair(i, carry):
        j = 1 + 2 * i
        scores(j + 1, s0_ref)
        softmax_pv(j, s1_ref)
        scores(j + 2, s1_ref)
        softmax_pv(j + 1, s0_ref)
        return carry

    def full_quad(i, carry):
        full_pair(2 * i, carry)
        return full_pair(2 * i + 1, carry)

    def full_octo(i, carry):
        full_quad(2 * i, carry)
        return full_quad(2 * i + 1, carry)

    def edge_pair(i, carry):
        j = 1 + 2 * i
        mask(j, s1_ref)
        scores(j + 1, s0_ref)
        softmax_pv(j, s1_ref)

        @pl.when(j + 1 < last)
        def _():
            mask(j + 1, s0_ref)
            scores(j + 2, s1_ref)
            softmax_pv(j + 1, s0_ref)
        return carry

    n_free = jnp.maximum(jnp.minimum(n_full, last) - 1, 0)
    lax.fori_loop(0, n_free // 8, full_octo, 0)
    lax.fori_loop(2 * (n_free // 8), n_free // 4, full_quad, 0)
    lax.fori_loop(2 * (n_free // 4), n_free // 2, full_pair, 0)
    lax.fori_loop(n_free // 2, last // 2, edge_pair, 0)

    def finish(s_ref, more):
        apply_mask(last, s_ref)
        softmax_pv(last, s_ref)
        outs = [acc_ref[hd, 0:HEAD_DIM, :] * (1.0 / acc_ref[hd, HEAD_DIM:HEAD_DIM + 1, :]) for hd in heads]
        if more:
            start(qn_ref)
            if ahead:
                scores(1, s1_ref)
                softmax_pv(0, sz_ref)
        lf = lam_ref[...]
        lam = (jnp.exp(jnp.sum(lf[0:1] * lf[1:2], axis=-1, keepdims=True))
               - jnp.exp(jnp.sum(lf[2:3] * lf[3:4], axis=-1, keepdims=True)) + lam_init)
        for hd in heads:
            od = outs[hd][:, :tqp] - lam * outs[hd][:, tqp:]
            ms = jnp.mean(od * od, axis=0, keepdims=True)
            on = od * lax.rsqrt(ms + NORM_EPS) * ng_ref[...] * (1.0 - lam_init)
            o_ref[0, :, lanes(hd)] = on.T[:tq].astype(BF16)

    for more in ((True, False) if nq > 1 else (False,)):
        step_ok = (qi < nq - 1) if more else (qi == nq - 1)

        @pl.when(step_ok & (last == 0))
        def _():
            finish(sz_ref, more)

        @pl.when(step_ok & (last >= 1) & (last % 2 == 1))
        def _():
            finish(s1_ref, more)

        @pl.when(step_ok & (last >= 1) & (last % 2 == 0))
        def _():
            finish(s0_ref, more)


def _diff_attn(q, k, vt, w, n_keys, q_pos0, tq, tk):
    b, t, _ = q.shape
    tkp = k.shape[1]
    nblk = tkp // tk
    nq = t // tq
    tqp = max(tq, LANES)
    hps = HEADS if nq == 1 else 1
    wide = hps * HEAD_DIM
    lam_init = 0.8 - 0.6 * math.exp(-0.3 * 0)

    def bounds(qi):
        k_lo = min(((q_pos0 + qi * tq) // CHUNK + 1) * CHUNK, n_keys)
        k_hi = min(((q_pos0 + qi * tq + tq - 1) // CHUNK + 1) * CHUNK, n_keys)
        return k_lo // tk, -(-k_hi // tk) - 1
    ahead = nq > 1 and all(min(bounds(qi)) >= 1 for qi in range(1, nq))
    kern = functools.partial(_diff_attn_kernel, tq=tq, tqp=tqp, tk=tk, nblk=nblk, nq=nq, hps=hps, ahead=ahead,
                             n_keys=n_keys, q_pos0=q_pos0, lam_init=lam_init)
    return pl.pallas_call(
        kern,
        grid=(b, HEADS // hps, nq),
        in_specs=[
            pl.BlockSpec((1, tq, wide), lambda i, h, j: (i, j, h)),
            pl.BlockSpec((1, tq, wide), lambda i, h, j: (i, jnp.minimum(j + 1, nq - 1), h)),
            pl.BlockSpec((1, tkp, wide), lambda i, h, j: (i, 0, h)),
            pl.BlockSpec((nblk, wide, tk), lambda i, h, j: (i, h, 0)),
            pl.BlockSpec((4, DQK), lambda i, h, j: (0, 0)),
            pl.BlockSpec((HEAD_DIM, 1), lambda i, h, j: (0, 0)),
        ],
        out_specs=pl.BlockSpec((1, tq, wide), lambda i, h, j: (i, j, h)),
        out_shape=jax.ShapeDtypeStruct((b, t, KEY_DIM), BF16),
        scratch_shapes=[
            pltpu.VMEM((hps, tk, 2 * tqp + LANES), F32),
            pltpu.VMEM((hps, tk, 2 * tqp + LANES), F32),
            pltpu.VMEM((hps, tk, 2 * tqp + LANES), F32),
            pltpu.VMEM((hps, 1, 2 * tqp), F32),
            pltpu.VMEM((hps, HEAD_DIM + SUM_ROWS, 2 * tqp), F32),
            pltpu.VMEM((hps, HEAD_DIM, 2 * tqp), BF16),
        ],
        compiler_params=_params("arbitrary", "arbitrary", "arbitrary"),
        name="diff_attn",
    )(q, q, k, vt, w["diff_lambda"], w["diff_norm_g_col"])


def _kv_cache_kernel(ck_ref, cv_ref, kn_ref, vn_ref, k_ref, vt_ref, *, p_len, t_new, pad_rows):
    k_ref[0, 0:p_len, :] = ck_ref[0].reshape(p_len, KEY_DIM).astype(BF16)
    zeros = jnp.zeros((pad_rows - t_new, KEY_DIM), BF16)
    k_ref[0, p_len:p_len + pad_rows, :] = jnp.concatenate([kn_ref[0], zeros], axis=0)
    vt_ref[0, :, 0:p_len] = cv_ref[0].reshape(p_len, KEY_DIM).T.astype(BF16)
    v_new = jnp.concatenate([vn_ref[0], zeros], axis=0).astype(F32)
    vt_ref[0, :, p_len:p_len + pad_rows] = v_new.T.astype(BF16)


def _kv_cache(cache_k, cache_v, k_new, v_new):
    b, p_len = cache_k.shape[:2]
    t_new = k_new.shape[1]
    assert p_len % LANES == 0, p_len
    pad_rows = -(-t_new // LANES) * LANES
    rows = p_len + pad_rows
    kern = functools.partial(_kv_cache_kernel, p_len=p_len, t_new=t_new, pad_rows=pad_rows)
    return pl.pallas_call(
        kern,
        grid=(b,),
        in_specs=[
            pl.BlockSpec((1, p_len, HEADS, HEAD_DIM), lambda i: (i, 0, 0, 0)),
            pl.BlockSpec((1, p_len, HEADS, HEAD_DIM), lambda i: (i, 0, 0, 0)),
            pl.BlockSpec((1, t_new, KEY_DIM), lambda i: (i, 0, 0)),
            pl.BlockSpec((1, t_new, KEY_DIM), lambda i: (i, 0, 0)),
        ],
        out_specs=(pl.BlockSpec((1, rows, KEY_DIM), lambda i: (i, 0, 0)),
                   pl.BlockSpec((1, KEY_DIM, rows), lambda i: (i, 0, 0))),
        out_shape=(jax.ShapeDtypeStruct((b, rows, KEY_DIM), BF16),
                   jax.ShapeDtypeStruct((b, KEY_DIM, rows), BF16)),
        compiler_params=_params("parallel"),
        name="kv_cache",
    )(cache_k, cache_v, k_new, v_new)


def _post_mix_kernel(x_ref, og_ref, od_ref, mk_ref, mv_ref, wo_ref, g_ref, wq_ref, wmo_ref, o_ref):
    nb, tm, _ = x_ref.shape
    rows = lambda ref: ref[...].reshape(nb * tm, ref.shape[-1])
    x1 = rows(x_ref) + _dot(rows(og_ref), wo_ref[0:KEY_DIM, :]) + _dot(rows(od_ref), wo_ref[KEY_DIM:, :])
    h2 = _rms(x1, g_ref[...]).astype(BF16)
    qm = _dot(h2, wq_ref[...]).astype(BF16)
    oms = []
    for bi in range(nb):
        outs = []
        for hd in range(HEADS):
            sl = slice(hd * HEAD_DIM, (hd + 1) * HEAD_DIM)
            mk = mk_ref[bi, :, sl]
            mv = mv_ref[bi, :, sl]
            s = _dot_nt(qm[bi * tm:(bi + 1) * tm, sl], mk) * (HEAD_DIM ** -0.5)
            p = jnp.exp(s - jnp.max(s, axis=-1, keepdims=True))
            l = jnp.sum(p, axis=-1, keepdims=True)
            outs.append(_dot((p / l).astype(BF16), mv))
        oms.append(jnp.concatenate(outs, axis=1))
    om = jnp.concatenate(oms, axis=0).astype(BF16)
    o_ref[...] = (x1 + _dot(om, wmo_ref[...])).reshape(o_ref.shape)


def _post_mix(x, og, od, mk, mv, w, tm):
    b, t, _ = x.shape
    nb = _tile(b, max(1, ROW_TILE // tm)) if tm == t else 1
    blk = lambda i, j: (i, j, 0)
    fix = lambda i, j: (0, 0)
    mem = lambda i, j: (i, 0, 0)
    return pl.pallas_call(
        _post_mix_kernel,
        grid=(b // nb, t // tm),
        in_specs=[
            pl.BlockSpec((nb, tm, D_MODEL), blk),
            pl.BlockSpec((nb, tm, KEY_DIM), blk),
            pl.BlockSpec((nb, tm, KEY_DIM), blk),
            pl.BlockSpec((nb, N_MEM, KEY_DIM), mem),
            pl.BlockSpec((nb, N_MEM, KEY_DIM), mem),
            pl.BlockSpec((D_MODEL, D_MODEL), fix, pipeline_mode=pl.Buffered(1)),
            pl.BlockSpec((1, D_MODEL), fix),
            pl.BlockSpec((D_MODEL, KEY_DIM), fix, pipeline_mode=pl.Buffered(1)),
            pl.BlockSpec((KEY_DIM, D_MODEL), fix, pipeline_mode=pl.Buffered(1)),
        ],
        out_specs=pl.BlockSpec((nb, tm, D_MODEL), blk),
        out_shape=jax.ShapeDtypeStruct((b, t, D_MODEL), F32),
        compiler_params=_params("parallel", "parallel"),
        name="post_mix",
    )(x, og, od, mk, mv, w["w_out"], w["norm_mem_g"], w["w_mq"], w["w_mo"])


def _mlp_kernel(x_ref, g_ref, wu_ref, wd_ref, fg_ref, o_ref, *, ff_blk):
    x = x_ref[...]
    hf = _rms(x, g_ref[...]).astype(BF16)
    acc = x
    for c0 in range(0, D_FF, ff_blk):
        u = jnp.maximum(_dot(hf, wu_ref[:, c0:c0 + ff_blk]), 0.0)
        acc = acc + _dot((u * u).astype(BF16), wd_ref[c0:c0 + ff_blk, :])
    o_ref[...] = _rms(acc, fg_ref[...])


def _mlp(x2d, w, tm):
    n = x2d.shape[0]
    row = lambda i: (i, 0)
    fix = lambda i: (0, 0)
    return pl.pallas_call(
        functools.partial(_mlp_kernel, ff_blk=D_MODEL),
        grid=(n // tm,),
        in_specs=[
            pl.BlockSpec((tm, D_MODEL), row),
            pl.BlockSpec((1, D_MODEL), fix),
            pl.BlockSpec((D_MODEL, D_FF), fix, pipeline_mode=pl.Buffered(1)),
            pl.BlockSpec((D_FF, D_MODEL), fix, pipeline_mode=pl.Buffered(1)),
            pl.BlockSpec((1, D_MODEL), fix),
        ],
        out_specs=pl.BlockSpec((tm, D_MODEL), row),
        out_shape=jax.ShapeDtypeStruct((n, D_MODEL), F32),
        compiler_params=_params("parallel"),
        name="mlp",
    )(x2d, w["norm_ffn_g"], w["w_up"], w["w_down"], w["final_norm_g"])


def _mem_kv_kernel(m_ref, g_ref, w_ref, k_ref, v_ref, kb_ref, vb_ref):
    mh = _rms(m_ref[...], g_ref[...]).astype(BF16)
    kv = _dot(mh, w_ref[...])
    kb_ref[...] = kv[:, :KEY_DIM].astype(BF16)
    vb_ref[...] = kv[:, KEY_DIM:].astype(BF16)
    k_ref[...] = kv[:, :KEY_DIM].reshape(k_ref.shape)
    v_ref[...] = kv[:, KEY_DIM:].reshape(v_ref.shape)


def _mem_kv(mem2d, w, tm):
    n = mem2d.shape[0]
    row = lambda i: (i, 0)
    fix = lambda i: (0, 0)
    return pl.pallas_call(
        _mem_kv_kernel,
        grid=(n // tm,),
        in_specs=[
            pl.BlockSpec((tm, D_MODEL), row),
            pl.BlockSpec((1, D_MODEL), fix),
            pl.BlockSpec((D_MODEL, 2 * KEY_DIM), fix),
        ],
        out_specs=(pl.BlockSpec((tm, HEADS, HEAD_DIM), lambda i: (i, 0, 0)),) * 2
        + (pl.BlockSpec((tm, KEY_DIM), row),) * 2,
        out_shape=(jax.ShapeDtypeStruct((n, HEADS, HEAD_DIM), F32),) * 2
        + (jax.ShapeDtypeStruct((n, KEY_DIM), BF16),) * 2,
        compiler_params=_params("parallel"),
        name="mem_kv",
    )(mem2d, w["mem_norm_g"], w["w_mkv"])


def _tile(n, pref):
    t = min(n, pref)
    assert n % t == 0, (n, t)
    return t


def _layer(x, pos0, conv_prev, s0, kv_prev, mem_k, mem_v, w):
    b, t, _ = x.shape
    n = b * t
    c = min(t, CHUNK)
    x2d = x.reshape(n, D_MODEL)
    tm = _tile(n, ROW_TILE)
    tab_rows = t if t % tm == 0 else n
    pos_rows = pos0 + (np.arange(tab_rows) % t)
    qkv, z, gb, qd, kf, kb, vf, vb, vt = _proj_in(x2d, pos_rows, tab_rows, w, tm)
    r3 = lambda a: a.reshape(b, t, -1)
    qkv3 = r3(qkv)
    og, s_new = _gdn(qkv3, r3(z), r3(gb), conv_prev, s0, w, nchunks=_tile(t // c, GDN_CHUNKS))
    conv_new = qkv3[:, t - (CONV_W - 1):, :]
    kb3 = r3(kb)
    if kv_prev is None:
        assert t % tm == 0, (t, tm)
        n_keys, tk = t, tm
    else:
        n_keys = kv_prev[0].shape[1] + t
        kb3, vt = _kv_cache(kv_prev[0], kv_prev[1], kb3, r3(vb))
        tk = kb3.shape[1]
    od = _diff_attn(r3(qd), kb3, vt, w, n_keys=n_keys, q_pos0=pos0, tq=_tile(t, ROW_TILE), tk=tk)
    x2 = _post_mix(x, og, od, mem_k, mem_v, w, tm=_tile(t, 2 * ROW_TILE))
    y = _mlp(x2.reshape(n, D_MODEL), w, _tile(n, 2 * ROW_TILE)).reshape(b, t, D_MODEL)
    h4 = lambda a: a.reshape(b, t, HEADS, HEAD_DIM)
    return y, s_new, conv_new, h4(kf), h4(vf)


def kernel(x_prompt, x_sample, mem_prompt, cache_diff_k, cache_diff_v, cache_mem_k, cache_mem_v, state_gdn, state_gdn_conv, norm_mix_g, w_in, gdn_conv_w, gdn_a_log, gdn_dt_bias, gdn_norm_g, diff_lambda, diff_norm_g, w_out, norm_mem_g, mem_norm_g, w_mq, w_mkv, w_mo, norm_ffn_g, w_up, w_down, final_norm_g):
    bp, tp, _ = x_prompt.shape
    bs, ts, _ = x_sample.shape
    p_len = cache_diff_k.shape[2]
    depth = w_in.shape[0]
    assert depth == 1
    l = 0
    wi = w_in[l]
    sp = [CONV_DIM, CONV_DIM + KEY_DIM, CONV_DIM + KEY_DIM + HEADS, CONV_DIM + KEY_DIM + 2 * HEADS]
    w_ab = jnp.concatenate([wi[:, sp[1]:sp[3]], jnp.zeros((D_MODEL, LANES - 2 * HEADS), F32)], axis=1)
    lanes_pad = lambda a: jnp.concatenate([a, jnp.zeros((LANES - a.shape[0],), F32)])[None, :]
    row = lambda a: a.reshape(1, -1)
    w = {
        "norm_mix_g": row(norm_mix_g[l]),
        "w_main": jnp.concatenate([wi[:, :sp[1]], wi[:, sp[3]:]], axis=1).astype(BF16),
        "w_ab": w_ab.astype(BF16),
        "a_log": lanes_pad(gdn_a_log[l]),
        "dt_bias": lanes_pad(gdn_dt_bias[l]),
        "conv_w": gdn_conv_w[l],
        "gdn_norm_g": row(gdn_norm_g[l]),
        "diff_lambda": diff_lambda[l],
        "diff_norm_g_col": diff_norm_g[l].reshape(HEAD_DIM, 1),
        "w_out": w_out[l].astype(BF16),
        "norm_mem_g": row(norm_mem_g[l]),
        "mem_norm_g": row(mem_norm_g[l]),
        "w_mq": w_mq[l].astype(BF16),
        "w_mkv": w_mkv[l].astype(BF16),
        "w_mo": w_mo[l].astype(BF16),
        "norm_ffn_g": row(norm_ffn_g[l]),
        "w_up": w_up[l].astype(BF16),
        "w_down": w_down[l].astype(BF16),
        "final_norm_g": row(final_norm_g),
    }
    n_mem = mem_prompt.shape[1]
    mk, mv, mkb, mvb = _mem_kv(mem_prompt.reshape(bp * n_mem, D_MODEL), w, _tile(bp * n_mem, ROW_TILE))
    mk = mk.reshape(bp, n_mem, HEADS, HEAD_DIM)
    mv = mv.reshape(bp, n_mem, HEADS, HEAD_DIM)
    mkb = mkb.reshape(bp, n_mem, KEY_DIM)
    mvb = mvb.reshape(bp, n_mem, KEY_DIM)

    zeros_conv = jnp.zeros((bp, CONV_W - 1, CONV_DIM), F32)
    zeros_state = jnp.zeros((bp, HEADS, HEAD_DIM, HEAD_DIM), F32)
    yp, sp_, cp, kp, vp = _layer(x_prompt, 0, zeros_conv, zeros_state, None, mkb, mvb, w)
    ys, ss, cs, ks_, vs = _layer(x_sample, p_len, state_gdn_conv[l], state_gdn[l],
                                 (cache_diff_k[l], cache_diff_v[l]),
                                 cache_mem_k[l].reshape(bs, n_mem, KEY_DIM).astype(BF16),
                                 cache_mem_v[l].reshape(bs, n_mem, KEY_DIM).astype(BF16), w)
    return (yp, ys, sp_[None], cp[None], kp[None], vp[None], mk[None], mv[None],
            ss[None], cs[None], ks_[None], vs[None])
```

```python
import functools
import math

import jax
import jax.numpy as jnp
import numpy as np
from jax import lax
from jax.experimental import pallas as pl
from jax.experimental.pallas import tpu as pltpu

D_MODEL = 1024
CHUNK = 64
HEADS = 4
HEAD_DIM = 128
KEY_DIM = HEADS * HEAD_DIM
CONV_DIM = 3 * KEY_DIM
CONV_W = 4
DQK = 64
ROT_DIM = 16
ROPE_THETA = 500000.0
N_MEM = 256
D_FF = 4 * D_MODEL
NORM_EPS = 1e-6
MAIN_COLS = CONV_DIM + 4 * KEY_DIM
LANES = 128
CONV_PAD = 8
SUM_ROWS = 16
VMEM_LIMIT = 56 * 1024 * 1024
ROW_TILE = 512
GDN_CHUNKS = 16
GDN_GROUP = 8

F32 = jnp.float32
BF16 = jnp.bfloat16
HIGHEST = lax.Precision.HIGHEST


def _dot(a, b, precision=None):
    return jnp.dot(a, b, preferred_element_type=F32, precision=precision)


def _dot_nt(a, b):
    return lax.dot_general(a, b, (((1,), (1,)), ((), ())), preferred_element_type=F32)


def _dot_tn(a, b):
    return lax.dot_general(a, b, (((0,), (0,)), ((), ())), preferred_element_type=F32)


def _rms(x, g):
    return x * lax.rsqrt(jnp.mean(x * x, axis=-1, keepdims=True) + NORM_EPS) * g


def _sigmoid(x):
    return 1.0 / (1.0 + jnp.exp(-x))


def _params(*sem):
    return pltpu.CompilerParams(dimension_semantics=sem, vmem_limit_bytes=VMEM_LIMIT)


def _proj_in_kernel(x_ref, g_ref, wm_ref, wab_ref, alog_ref, dtb_ref, cos_ref, sna_ref, snb_ref,
                    qkv_ref, z_ref, gb_ref, q_ref, kf_ref, kb_ref, vf_ref, vb_ref, vt_ref):
    x = x_ref[...]
    h = _rms(x, g_ref[...]).astype(BF16)
    q0 = CONV_DIM + KEY_DIM
    att = _dot(h, wm_ref[:, q0:])
    main = _dot(h, wm_ref[:, :q0])
    ab = _dot(h, wab_ref[...])
    qkv_ref[...] = main[:, :CONV_DIM]
    z_ref[...] = main[:, CONV_DIM:CONV_DIM + KEY_DIM]
    xa = ab + dtb_ref[...]
    softplus = jnp.maximum(xa, 0.0) + jnp.log1p(jnp.exp(-jnp.abs(xa)))
    lane = lax.broadcasted_iota(jnp.int32, ab.shape, 1)
    gb_ref[...] = jnp.where(lane < HEADS, -jnp.exp(alog_ref[...]) * softplus, _sigmoid(ab))
    cos, sna, snb = cos_ref[...], sna_ref[...], snb_ref[...]
    krs, vhs = [], []
    for hd in range(HEADS):
        sl = slice(hd * HEAD_DIM, (hd + 1) * HEAD_DIM)
        qh = att[:, hd * HEAD_DIM:(hd + 1) * HEAD_DIM]
        kh = att[:, KEY_DIM + hd * HEAD_DIM:KEY_DIM + (hd + 1) * HEAD_DIM]
        vh = att[:, 2 * KEY_DIM + hd * HEAD_DIM:2 * KEY_DIM + (hd + 1) * HEAD_DIM]
        qr = qh * cos + pltpu.roll(qh, LANES - ROT_DIM // 2, 1) * sna + pltpu.roll(qh, ROT_DIM // 2, 1) * snb
        kr = kh * cos + pltpu.roll(kh, LANES - ROT_DIM // 2, 1) * sna + pltpu.roll(kh, ROT_DIM // 2, 1) * snb
        q_ref[:, sl] = (qr * (DQK ** -0.5 * math.log2(math.e))).astype(BF16)
        krs.append(kr)
        vhs.append(vh)
        kb_ref[:, sl] = kr.astype(BF16)
        vb_ref[:, sl] = vh.astype(BF16)
        vt_ref[0, sl, :] = vh.T.astype(BF16)
    kf_ref[...] = jnp.concatenate(krs, axis=1).reshape(kf_ref.shape)
    vf_ref[...] = jnp.concatenate(vhs, axis=1).reshape(vf_ref.shape)


def _rope_tables(pos):
    inv = ROPE_THETA ** (-np.arange(0, ROT_DIM, 2, dtype=np.float64) / ROT_DIM)
    ang = pos.astype(np.float64)[:, None] * inv[None, :]
    cos, sin = np.cos(ang), np.sin(ang)
    pad = np.zeros((pos.shape[0], DQK - ROT_DIM))
    cos64 = np.concatenate([cos, cos, pad + 1.0], axis=1)
    sna64 = np.concatenate([-sin, np.zeros_like(sin), pad], axis=1)
    snb64 = np.concatenate([np.zeros_like(sin), sin, pad], axis=1)
    return tuple(jnp.asarray(np.concatenate([a, a], axis=1), F32) for a in (cos64, sna64, snb64))


def _proj_in(x2d, pos_rows, tab_rows, w, tm):
    n = x2d.shape[0]
    nt = tab_rows // tm
    cos, sna, snb = _rope_tables(pos_rows)
    row = lambda i: (i, 0)
    fix = lambda i: (0, 0)
    tab = lambda i: (i % nt, 0)
    out_shape = (
        jax.ShapeDtypeStruct((n, CONV_DIM), F32),
        jax.ShapeDtypeStruct((n, KEY_DIM), F32),
        jax.ShapeDtypeStruct((n, LANES), F32),
        jax.ShapeDtypeStruct((n, KEY_DIM), BF16),
        jax.ShapeDtypeStruct((n, HEADS, HEAD_DIM), F32),
        jax.ShapeDtypeStruct((n, KEY_DIM), BF16),
        jax.ShapeDtypeStruct((n, HEADS, HEAD_DIM), F32),
        jax.ShapeDtypeStruct((n, KEY_DIM), BF16),
        jax.ShapeDtypeStruct((n // tm, KEY_DIM, tm), BF16),
    )

    def spec(sd):
        if len(sd.shape) == 2:
            return pl.BlockSpec((tm, sd.shape[1]), row)
        return pl.BlockSpec((sd.shape[0] * tm // n,) + sd.shape[1:], lambda i: (i, 0, 0))
    out_specs = tuple(spec(sd) for sd in out_shape)
    return pl.pallas_call(
        _proj_in_kernel,
        grid=(n // tm,),
        in_specs=[
            pl.BlockSpec((tm, D_MODEL), row),
            pl.BlockSpec((1, D_MODEL), fix),
            pl.BlockSpec((D_MODEL, MAIN_COLS), fix),
            pl.BlockSpec((D_MODEL, LANES), fix),
            pl.BlockSpec((1, LANES), fix),
            pl.BlockSpec((1, LANES), fix),
            pl.BlockSpec((tm, LANES), tab),
            pl.BlockSpec((tm, LANES), tab),
            pl.BlockSpec((tm, LANES), tab),
        ],
        out_specs=out_specs,
        out_shape=out_shape,
        compiler_params=_params("parallel"),
        name="proj_in",
    )(x2d, w["norm_mix_g"], w["w_main"], w["w_ab"], w["a_log"], w["dt_bias"], cos, sna, snb)


def _gdn_kernel(qkv_ref, z_ref, gb_ref, cprev_ref, s0_ref, cw_ref, ng_ref,
                o_ref, sout_ref, xp_ref, y_ref, s_ref, *, c, nchunks):
    j = pl.program_id(1)
    rows = c * nchunks
    hc = HEADS * c

    @pl.when(j == 0)
    def _():
        xp_ref[0:CONV_PAD, :] = cprev_ref[0]
        s_ref[...] = s0_ref[0]

    xp_ref[CONV_PAD:CONV_PAD + rows, :] = qkv_ref[0]
    cw = cw_ref[...]
    xp = xp_ref[...]
    y = xp[CONV_PAD:] * cw[CONV_W - 1:CONV_W, :]
    for i in range(CONV_W - 1):
        y = y + pltpu.roll(xp, CONV_W - 1 - i, 0)[CONV_PAD:] * cw[i:i + 1, :]
    xp_ref[0:CONV_PAD, :] = xp[rows:rows + CONV_PAD]
    y_ref[...] = y * _sigmoid(y)

    ri = lax.broadcasted_iota(jnp.int32, (hc, hc), 0)
    cj = lax.broadcasted_iota(jnp.int32, (hc, hc), 1)
    same = (ri // c) == (cj // c)
    tril_bd = same & (ri >= cj)
    strict_bd = same & (ri > cj)
    eye_bd = (ri == cj).astype(F32)
    r1 = lax.broadcasted_iota(jnp.int32, (c, c), 0)
    c1 = lax.broadcasted_iota(jnp.int32, (c, c), 1)
    tril_c = (r1 >= c1).astype(F32)
    triu_c = (r1 <= c1).astype(F32)
    ng = ng_ref[...]
    gb_all = gb_ref[0]
    if rows % LANES:
        gb_all = jnp.concatenate([gb_all, jnp.zeros((LANES - rows % LANES, LANES), F32)], axis=0)
    gbt_all = gb_all.T[0:2 * HEADS, :]

    def prep(ci):
        rs = slice(ci * c, (ci + 1) * c)
        gcol = gb_ref[0, rs, :]
        gc_col = _dot(tril_c, gcol, HIGHEST)
        gc_row = _dot(gbt_all[:, rs], triu_c, HIGHEST)
        ks, kbs, qs, vbs, kbgs, qgs, kds, dcol, grow, glast = [], [], [], [], [], [], [], [], [], []
        for hd in range(HEADS):
            qh = y_ref[rs, hd * HEAD_DIM:(hd + 1) * HEAD_DIM]
            kh = y_ref[rs, KEY_DIM + hd * HEAD_DIM:KEY_DIM + (hd + 1) * HEAD_DIM]
            vh = y_ref[rs, 2 * KEY_DIM + hd * HEAD_DIM:2 * KEY_DIM + (hd + 1) * HEAD_DIM]
            qh = qh * lax.rsqrt(jnp.sum(qh * qh, axis=-1, keepdims=True) + NORM_EPS) * (HEAD_DIM ** -0.5)
            kh = kh * lax.rsqrt(jnp.sum(kh * kh, axis=-1, keepdims=True) + NORM_EPS)
            g_h = gc_col[:, hd:hd + 1]
            beta_h = gcol[:, HEADS + hd:HEADS + hd + 1]
            g_last = gc_col[c - 1:c, hd:hd + 1]
            eg = jnp.exp(g_h)
            kb = kh * beta_h
            ks.append(kh)
            kbs.append(kb)
            qs.append(qh)
            vbs.append(vh * beta_h)
            kbgs.append(kb * eg)
            qgs.append((qh * eg).astype(BF16))
            kds.append((kh * jnp.exp(g_last - g_h)).astype(BF16))
            glast.append(jnp.exp(g_last))
            dcol.append(jnp.broadcast_to(g_h, (c, hc)))
            grow.append(gc_row[hd:hd + 1, :])
        k_s = jnp.concatenate(ks, axis=0).astype(BF16)
        q_s = jnp.concatenate(qs, axis=0).astype(BF16)
        dmat = jnp.concatenate(dcol, axis=0) - jnp.concatenate(grow, axis=1)
        decay = jnp.exp(jnp.where(tril_bd, dmat, -jnp.inf))
        a_kk = _dot_nt(jnp.concatenate(kbs, axis=0).astype(BF16), k_s)
        x = jnp.where(strict_bd, -(a_kk * decay), 0.0)
        rhs = jnp.concatenate([jnp.concatenate(vbs, axis=0), jnp.concatenate(kbgs, axis=0)], axis=1)
        a_qk = (_dot_nt(q_s, k_s) * decay).astype(BF16)
        return dict(x=x, rhs=rhs.astype(BF16), a_qk=a_qk, qg=qgs, kd=kds, glast=glast)

    state = [s_ref[hd] for hd in range(HEADS)]
    for c0 in range(0, nchunks, GDN_GROUP):
        group = range(c0, min(c0 + GDN_GROUP, nchunks))
        pre = {ci: prep(ci) for ci in group}
        ps = [eye_bd + pre[ci]["x"] for ci in group]
        xps = [pre[ci]["x"] for ci in group]
        for _ in range(int(math.log2(c)) - 1):
            xbs = [xp.astype(BF16) for xp in xps]
            xps = [_dot(xb, xb) for xb in xbs]
            ps = [p + _dot(p.astype(BF16), xp.astype(BF16)) for p, xp in zip(ps, xps)]
        uws = {ci: _dot(p.astype(BF16), pre[ci]["rhs"]) for p, ci in zip(ps, group)}
        for ci in group:
            rs = slice(ci * c, (ci + 1) * c)
            d, uw = pre[ci], uws[ci]
            vnews, qss = [], []
            for hd in range(HEADS):
                hs = slice(hd * c, (hd + 1) * c)
                wq = jnp.concatenate([uw[hs, HEAD_DIM:].astype(BF16), d["qg"][hd]], axis=0)
                ws = _dot(wq, state[hd].astype(BF16))
                v_new = uw[hs, :HEAD_DIM] - ws[:c]
                vnews.append(v_new)
                qss.append(ws[c:])
                state[hd] = state[hd] * d["glast"][hd] + _dot_tn(d["kd"][hd], v_new.astype(BF16))
            o_s = jnp.concatenate(qss, axis=0) + _dot(d["a_qk"], jnp.concatenate(vnews, axis=0).astype(BF16))
            for hd in range(HEADS):
                sl = slice(hd * HEAD_DIM, (hd + 1) * HEAD_DIM)
                zh = z_ref[0, rs, sl]
                o_h = _rms(o_s[hd * c:(hd + 1) * c], ng) * (zh * _sigmoid(zh))
                o_ref[0, rs, sl] = o_h.astype(BF16)
    for hd in range(HEADS):
        s_ref[hd] = state[hd]

    @pl.when(j == pl.num_programs(1) - 1)
    def _():
        sout_ref[0] = s_ref[...]


def _gdn(qkv, z, gb, conv_prev, s0, w, nchunks):
    b, t, _ = qkv.shape
    c = min(t, CHUNK)
    rows = c * nchunks
    cprev = jnp.concatenate([jnp.zeros((b, CONV_PAD - (CONV_W - 1), CONV_DIM), F32), conv_prev], axis=1)
    blk = lambda i, j: (i, j, 0)
    kern = functools.partial(_gdn_kernel, c=c, nchunks=nchunks)
    return pl.pallas_call(
        kern,
        grid=(b, t // rows),
        in_specs=[
            pl.BlockSpec((1, rows, CONV_DIM), blk),
            pl.BlockSpec((1, rows, KEY_DIM), blk),
            pl.BlockSpec((1, rows, LANES), blk),
            pl.BlockSpec((1, CONV_PAD, CONV_DIM), lambda i, j: (i, 0, 0)),
            pl.BlockSpec((1, HEADS, HEAD_DIM, HEAD_DIM), lambda i, j: (i, 0, 0, 0)),
            pl.BlockSpec((CONV_W, CONV_DIM), lambda i, j: (0, 0)),
            pl.BlockSpec((1, HEAD_DIM), lambda i, j: (0, 0)),
        ],
        out_specs=(
            pl.BlockSpec((1, rows, KEY_DIM), blk),
            pl.BlockSpec((1, HEADS, HEAD_DIM, HEAD_DIM), lambda i, j: (i, 0, 0, 0)),
        ),
        out_shape=(
            jax.ShapeDtypeStruct((b, t, KEY_DIM), BF16),
            jax.ShapeDtypeStruct((b, HEADS, HEAD_DIM, HEAD_DIM), F32),
        ),
        scratch_shapes=[
            pltpu.VMEM((rows + CONV_PAD, CONV_DIM), F32),
            pltpu.VMEM((rows, CONV_DIM), F32),
            pltpu.VMEM((HEADS, HEAD_DIM, HEAD_DIM), F32),
        ],
        compiler_params=_params("parallel", "arbitrary"),
        name="gdn",
    )(qkv, z, gb, cprev, s0, w["conv_w"], w["gdn_norm_g"])


def _diff_attn_kernel(q_ref, qn_ref, k_ref, vt_ref, lam_ref, ng_ref, o_ref,
                      sz_ref, s0_ref, s1_ref, m_ref, acc_ref, qst_ref,
                      *, tq, tqp, tk, nblk, nq, hps, ahead, n_keys, q_pos0, lam_init):
    qi = pl.program_id(2)
    width = 2 * tqp
    heads = range(hps)
    lanes = lambda hd: slice(hd * HEAD_DIM, (hd + 1) * HEAD_DIM)
    pos_lo = q_pos0 + qi * tq
    k_lo = jnp.minimum((pos_lo // CHUNK + 1) * CHUNK, n_keys)
    k_hi = jnp.minimum(((pos_lo + tq - 1) // CHUNK + 1) * CHUNK, n_keys)
    n_full = k_lo // tk
    last = (k_hi + tk - 1) // tk - 1

    def scores(jb, s_ref):
        jl = jnp.minimum(jb, nblk - 1)
        ks = pl.ds(pl.multiple_of(jl * tk, tk), tk)
        for hd in heads:
            s_ref[hd, :, 0:width] = _dot(k_ref[0, ks, lanes(hd)], qst_ref[hd])

    def start(qx_ref):
        for hd in heads:
            q = qx_ref[0, :, lanes(hd)]
            if tqp > tq:
                q = jnp.concatenate([q, jnp.zeros((tqp - tq, HEAD_DIM), q.dtype)], axis=0)
            lane = lax.broadcasted_iota(jnp.int32, q.shape, 1)
            zero = jnp.zeros_like(q)
            qs = jnp.concatenate([jnp.where(lane < DQK, q, zero), jnp.where(lane >= DQK, q, zero)], axis=0)
            qst_ref[hd] = qs.T
        m_ref[...] = jnp.full(m_ref.shape, -jnp.inf, F32)
        acc_ref[...] = jnp.zeros(acc_ref.shape, F32)
        scores(0, sz_ref)

    def apply_mask(jb, s_ref):
        col = lax.broadcasted_iota(jnp.int32, (tk, width), 1)
        qpos = pos_lo + jnp.where(col >= tqp, col - tqp, col)
        kpos = jb * tk + lax.broadcasted_iota(jnp.int32, (tk, width), 0)
        vis = kpos < jnp.minimum((qpos // CHUNK + 1) * CHUNK, n_keys)
        for hd in heads:
            s_ref[hd, :, 0:width] = jnp.where(vis, s_ref[hd, :, 0:width], -jnp.inf)

    def mask(jb, s_ref):
        @pl.when(jb >= n_full)
        def _():
            apply_mask(jb, s_ref)

    def softmax_pv(jb, s_ref):
        ones = jnp.ones((SUM_ROWS, tk), BF16)
        for hd in heads:
            st = s_ref[hd, :, 0:width]
            m_old = m_ref[hd]
            m_new = jnp.maximum(m_old, jnp.max(st, axis=0, keepdims=True))
            alpha = jnp.exp2(m_old - m_new)
            p = jnp.exp2(st - m_new)
            vt1 = jnp.concatenate([vt_ref[jnp.minimum(jb, nblk - 1), lanes(hd), :], ones], axis=0)
            acc_ref[hd] = alpha * acc_ref[hd] + _dot(vt1, p.astype(BF16))
            m_ref[hd] = m_new

    def first_block():
        mask(0, sz_ref)
        scores(1, s1_ref)
        softmax_pv(0, sz_ref)

    @pl.when(qi == 0)
    def _():
        start(q_ref)

    @pl.when((last >= 1) & ((qi == 0) if ahead else True))
    def _():
        first_block()

    def full_pair(i, carry):
        j = 1 + 2 * i
        scores(j + 1, s0_ref)
        softmax_pv(j, s1_ref)
        scores(j + 2, s1_ref)
        softmax_pv(j + 1, s0_ref)
        return carry

    def full_quad(i, carry):
        full_pair(2 * i, carry)
        return full_pair(2 * i + 1, carry)

    def full_octo(i, carry):
        full_quad(2 * i, carry)
        return full_quad(2 * i + 1, carry)

    def edge_pair(i, carry):
        j = 1 + 2 * i
        mask(j, s1_ref)
        scores(j + 1, s0_ref)
        softmax_pv(j, s1_ref)

        @pl.when(j + 1 < last)
        def _():
            mask(j + 1, s0_ref)
            scores(j + 2, s1_ref)
            softmax_pv(j + 1, s0_ref)
        return carry

    n_free = jnp.maximum(jnp.minimum(n_full, last) - 1, 0)
    lax.fori_loop(0, n_free // 8, full_octo, 0)
    lax.fori_loop(2 * (n_free // 8), n_free // 4, full_quad, 0)
    lax.fori_loop(2 * (n_free // 4), n_free // 2, full_pair, 0)
    lax.fori_loop(n_free // 2, last // 2, edge_pair, 0)

    def finish(s_ref, more):
        apply_mask(last, s_ref)
        softmax_pv(last, s_ref)
        outs = [acc_ref[hd, 0:HEAD_DIM, :] * (1.0 / acc_ref[hd, HEAD_DIM:HEAD_DIM + 1, :]) for hd in heads]
        if more:
            start(qn_ref)
            if ahead:
                scores(1, s1_ref)
                softmax_pv(0, sz_ref)
        lf = lam_ref[...]
        lam = (jnp.exp(jnp.sum(lf[0:1] * lf[1:2], axis=-1, keepdims=True))
               - jnp.exp(jnp.sum(lf[2:3] * lf[3:4], axis=-1, keepdims=True)) + lam_init)
        for hd in heads:
            od = outs[hd][:, :tqp] - lam * outs[hd][:, tqp:]
            ms = jnp.mean(od * od, axis=0, keepdims=True)
            on = od * lax.rsqrt(ms + NORM_EPS) * ng_ref[...] * (1.0 - lam_init)
            o_ref[0, :, lanes(hd)] = on.T[:tq].astype(BF16)

    for more in ((True, False) if nq > 1 else (False,)):
        step_ok = (qi < nq - 1) if more else (qi == nq - 1)

        @pl.when(step_ok & (last == 0))
        def _():
            finish(sz_ref, more)

        @pl.when(step_ok & (last >= 1) & (last % 2 == 1))
        def _():
            finish(s1_ref, more)

        @pl.when(step_ok & (last >= 1) & (last % 2 == 0))
        def _():
            finish(s0_ref, more)


def _diff_attn(q, k, vt, w, n_keys, q_pos0, tq, tk):
    b, t, _ = q.shape
    tkp = k.shape[1]
    nblk = tkp // tk
    nq = t // tq
    tqp = max(tq, LANES)
    hps = HEADS if nq == 1 else 1
    wide = hps * HEAD_DIM
    lam_init = 0.8 - 0.6 * math.exp(-0.3 * 0)

    def bounds(qi):
        k_lo = min(((q_pos0 + qi * tq) // CHUNK + 1) * CHUNK, n_keys)
        k_hi = min(((q_pos0 + qi * tq + tq - 1) // CHUNK + 1) * CHUNK, n_keys)
        return k_lo // tk, -(-k_hi // tk) - 1
    ahead = nq > 1 and all(min(bounds(qi)) >= 1 for qi in range(1, nq))
    kern = functools.partial(_diff_attn_kernel, tq=tq, tqp=tqp, tk=tk, nblk=nblk, nq=nq, hps=hps, ahead=ahead,
                             n_keys=n_keys, q_pos0=q_pos0, lam_init=lam_init)
    return pl.pallas_call(
        kern,
        grid=(b, HEADS // hps, nq),
        in_specs=[
            pl.BlockSpec((1, tq, wide), lambda i, h, j: (i, j, h)),
            pl.BlockSpec((1, tq, wide), lambda i, h, j: (i, jnp.minimum(j + 1, nq - 1), h)),
            pl.BlockSpec((1, tkp, wide), lambda i, h, j: (i, 0, h)),
            pl.BlockSpec((nblk, wide, tk), lambda i, h, j: (i, h, 0)),
            pl.BlockSpec((4, DQK), lambda i, h, j: (0, 0)),
            pl.BlockSpec((HEAD_DIM, 1), lambda i, h, j: (0, 0)),
        ],
        out_specs=pl.BlockSpec((1, tq, wide), lambda i, h, j: (i, j, h)),
        out_shape=jax.ShapeDtypeStruct((b, t, KEY_DIM), BF16),
        scratch_shapes=[
            pltpu.VMEM((hps, tk, 2 * tqp + LANES), F32),
            pltpu.VMEM((hps, tk, 2 * tqp + LANES), F32),
            pltpu.VMEM((hps, tk, 2 * tqp + LANES), F32),
            pltpu.VMEM((hps, 1, 2 * tqp), F32),
            pltpu.VMEM((hps, HEAD_DIM + SUM_ROWS, 2 * tqp), F32),
            pltpu.VMEM((hps, HEAD_DIM, 2 * tqp), BF16),
        ],
        compiler_params=_params("arbitrary", "arbitrary", "arbitrary"),
        name="diff_attn",
    )(q, q, k, vt, w["diff_lambda"], w["diff_norm_g_col"])


def _kv_cache_kernel(ck_ref, cv_ref, kn_ref, vn_ref, k_ref, vt_ref, *, p_len, t_new, pad_rows):
    k_ref[0, 0:p_len, :] = ck_ref[0].reshape(p_len, KEY_DIM).astype(BF16)
    zeros = jnp.zeros((pad_rows - t_new, KEY_DIM), BF16)
    k_ref[0, p_len:p_len + pad_rows, :] = jnp.concatenate([kn_ref[0], zeros], axis=0)
    vt_ref[0, :, 0:p_len] = cv_ref[0].reshape(p_len, KEY_DIM).T.astype(BF16)
    v_new = jnp.concatenate([vn_ref[0], zeros], axis=0).astype(F32)
    vt_ref[0, :, p_len:p_len + pad_rows] = v_new.T.astype(BF16)


def _kv_cache(cache_k, cache_v, k_new, v_new):
    b, p_len = cache_k.shape[:2]
    t_new = k_new.shape[1]
    assert p_len % LANES == 0, p_len
    pad_rows = -(-t_new // LANES) * LANES
    rows = p_len + pad_rows
    kern = functools.partial(_kv_cache_kernel, p_len=p_len, t_new=t_new, pad_rows=pad_rows)
    return pl.pallas_call(
        kern,
        grid=(b,),
        in_specs=[
            pl.BlockSpec((1, p_len, HEADS, HEAD_DIM), lambda i: (i, 0, 0, 0)),
            pl.BlockSpec((1, p_len, HEADS, HEAD_DIM), lambda i: (i, 0, 0, 0)),
            pl.BlockSpec((1, t_new, KEY_DIM), lambda i: (i, 0, 0)),
            pl.BlockSpec((1, t_new, KEY_DIM), lambda i: (i, 0, 0)),
        ],
        out_specs=(pl.BlockSpec((1, rows, KEY_DIM), lambda i: (i, 0, 0)),
                   pl.BlockSpec((1, KEY_DIM, rows), lambda i: (i, 0, 0))),
        out_shape=(jax.ShapeDtypeStruct((b, rows, KEY_DIM), BF16),
                   jax.ShapeDtypeStruct((b, KEY_DIM, rows), BF16)),
        compiler_params=_params("parallel"),
        name="kv_cache",
    )(cache_k, cache_v, k_new, v_new)


def _post_mix_kernel(x_ref, og_ref, od_ref, mk_ref, mv_ref, wo_ref, g_ref, wq_ref, wmo_ref, o_ref):
    nb, tm, _ = x_ref.shape
    rows = lambda ref: ref[...].reshape(nb * tm, ref.shape[-1])
    x1 = rows(x_ref) + _dot(rows(og_ref), wo_ref[0:KEY_DIM, :]) + _dot(rows(od_ref), wo_ref[KEY_DIM:, :])
    h2 = _rms(x1, g_ref[...]).astype(BF16)
    qm = _dot(h2, wq_ref[...]).astype(BF16)
    oms = []
    for bi in range(nb):
        outs = []
        for hd in range(HEADS):
            sl = slice(hd * HEAD_DIM, (hd + 1) * HEAD_DIM)
            mk = mk_ref[bi, :, sl]
            mv = mv_ref[bi, :, sl]
            s = _dot_nt(qm[bi * tm:(bi + 1) * tm, sl], mk) * (HEAD_DIM ** -0.5)
            p = jnp.exp(s - jnp.max(s, axis=-1, keepdims=True))
            l = jnp.sum(p, axis=-1, keepdims=True)
            outs.append(_dot((p / l).astype(BF16), mv))
        oms.append(jnp.concatenate(outs, axis=1))
    om = jnp.concatenate(oms, axis=0).astype(BF16)
    o_ref[...] = (x1 + _dot(om, wmo_ref[...])).reshape(o_ref.shape)


def _post_mix(x, og, od, mk, mv, w, tm):
    b, t, _ = x.shape
    nb = _tile(b, max(1, ROW_TILE // tm)) if tm == t else 1
    blk = lambda i, j: (i, j, 0)
    fix = lambda i, j: (0, 0)
    mem = lambda i, j: (i, 0, 0)
    return pl.pallas_call(
        _post_mix_kernel,
        grid=(b // nb, t // tm),
        in_specs=[
            pl.BlockSpec((nb, tm, D_MODEL), blk),
            pl.BlockSpec((nb, tm, KEY_DIM), blk),
            pl.BlockSpec((nb, tm, KEY_DIM), blk),
            pl.BlockSpec((nb, N_MEM, KEY_DIM), mem),
            pl.BlockSpec((nb, N_MEM, KEY_DIM), mem),
            pl.BlockSpec((D_MODEL, D_MODEL), fix, pipeline_mode=pl.Buffered(1)),
            pl.BlockSpec((1, D_MODEL), fix),
            pl.BlockSpec((D_MODEL, KEY_DIM), fix, pipeline_mode=pl.Buffered(1)),
            pl.BlockSpec((KEY_DIM, D_MODEL), fix, pipeline_mode=pl.Buffered(1)),
        ],
        out_specs=pl.BlockSpec((nb, tm, D_MODEL), blk),
        out_shape=jax.ShapeDtypeStruct((b, t, D_MODEL), F32),
        compiler_params=_params("parallel", "parallel"),
        name="post_mix",
    )(x, og, od, mk, mv, w["w_out"], w["norm_mem_g"], w["w_mq"], w["w_mo"])


def _mlp_kernel(x_ref, g_ref, wu_ref, wd_ref, fg_ref, o_ref, *, ff_blk):
    x = x_ref[...]
    hf = _rms(x, g_ref[...]).astype(BF16)
    acc = x
    for c0 in range(0, D_FF, ff_blk):
        u = jnp.maximum(_dot(hf, wu_ref[:, c0:c0 + ff_blk]), 0.0)
        acc = acc + _dot((u * u).astype(BF16), wd_ref[c0:c0 + ff_blk, :])
    o_ref[...] = _rms(acc, fg_ref[...])


def _mlp(x2d, w, tm):
    n = x2d.shape[0]
    row = lambda i: (i, 0)
    fix = lambda i: (0, 0)
    return pl.pallas_call(
        functools.partial(_mlp_kernel, ff_blk=D_MODEL),
        grid=(n // tm,),
        in_specs=[
            pl.BlockSpec((tm, D_MODEL), row),
            pl.BlockSpec((1, D_MODEL), fix),
            pl.BlockSpec((D_MODEL, D_FF), fix, pipeline_mode=pl.Buffered(1)),
            pl.BlockSpec((D_FF, D_MODEL), fix, pipeline_mode=pl.Buffered(1)),
            pl.BlockSpec((1, D_MODEL), fix),
        ],
        out_specs=pl.BlockSpec((tm, D_MODEL), row),
        out_shape=jax.ShapeDtypeStruct((n, D_MODEL), F32),
        compiler_params=_params("parallel"),
        name="mlp",
    )(x2d, w["norm_ffn_g"], w["w_up"], w["w_down"], w["final_norm_g"])


def _mem_kv_kernel(m_ref, g_ref, w_ref, k_ref, v_ref, kb_ref, vb_ref):
    mh = _rms(m_ref[...], g_ref[...]).astype(BF16)
    kv = _dot(mh, w_ref[...])
    kb_ref[...] = kv[:, :KEY_DIM].astype(BF16)
    vb_ref[...] = kv[:, KEY_DIM:].astype(BF16)
    k_ref[...] = kv[:, :KEY_DIM].reshape(k_ref.shape)
    v_ref[...] = kv[:, KEY_DIM:].reshape(v_ref.shape)


def _mem_kv(mem2d, w, tm):
    n = mem2d.shape[0]
    row = lambda i: (i, 0)
    fix = lambda i: (0, 0)
    return pl.pallas_call(
        _mem_kv_kernel,
        grid=(n // tm,),
        in_specs=[
            pl.BlockSpec((tm, D_MODEL), row),
            pl.BlockSpec((1, D_MODEL), fix),
            pl.BlockSpec((D_MODEL, 2 * KEY_DIM), fix),
        ],
        out_specs=(pl.BlockSpec((tm, HEADS, HEAD_DIM), lambda i: (i, 0, 0)),) * 2
        + (pl.BlockSpec((tm, KEY_DIM), row),) * 2,
        out_shape=(jax.ShapeDtypeStruct((n, HEADS, HEAD_DIM), F32),) * 2
        + (jax.ShapeDtypeStruct((n, KEY_DIM), BF16),) * 2,
        compiler_params=_params("parallel"),
        name="mem_kv",
    )(mem2d, w["mem_norm_g"], w["w_mkv"])


def _tile(n, pref):
    t = min(n, pref)
    assert n % t == 0, (n, t)
    return t


def _layer(x, pos0, conv_prev, s0, kv_prev, mem_k, mem_v, w):
    b, t, _ = x.shape
    n = b * t
    c = min(t, CHUNK)
    x2d = x.reshape(n, D_MODEL)
    tm = _tile(n, ROW_TILE)
    tab_rows = t if t % tm == 0 else n
    pos_rows = pos0 + (np.arange(tab_rows) % t)
    qkv, z, gb, qd, kf, kb, vf, vb, vt = _proj_in(x2d, pos_rows, tab_rows, w, tm)
    r3 = lambda a: a.reshape(b, t, -1)
    qkv3 = r3(qkv)
    og, s_new = _gdn(qkv3, r3(z), r3(gb), conv_prev, s0, w, nchunks=_tile(t // c, GDN_CHUNKS))
    conv_new = qkv3[:, t - (CONV_W - 1):, :]
    kb3 = r3(kb)
    if kv_prev is None:
        assert t % tm == 0, (t, tm)
        n_keys, tk = t, tm
    else:
        n_keys = kv_prev[0].shape[1] + t
        kb3, vt = _kv_cache(kv_prev[0], kv_prev[1], kb3, r3(vb))
        tk = kb3.shape[1]
    od = _diff_attn(r3(qd), kb3, vt, w, n_keys=n_keys, q_pos0=pos0, tq=_tile(t, ROW_TILE), tk=tk)
    x2 = _post_mix(x, og, od, mem_k, mem_v, w, tm=_tile(t, 2 * ROW_TILE))
    y = _mlp(x2.reshape(n, D_MODEL), w, _tile(n, 2 * ROW_TILE)).reshape(b, t, D_MODEL)
    h4 = lambda a: a.reshape(b, t, HEADS, HEAD_DIM)
    return y, s_new, conv_new, h4(kf), h4(vf)


def kernel(x_prompt, x_sample, mem_prompt, cache_diff_k, cache_diff_v, cache_mem_k, cache_mem_v, state_gdn, state_gdn_conv, norm_mix_g, w_in, gdn_conv_w, gdn_a_log, gdn_dt_bias, gdn_norm_g, diff_lambda, diff_norm_g, w_out, norm_mem_g, mem_norm_g, w_mq, w_mkv, w_mo, norm_ffn_g, w_up, w_down, final_norm_g):
    bp, tp, _ = x_prompt.shape
    bs, ts, _ = x_sample.shape
    p_len = cache_diff_k.shape[2]
    depth = w_in.shape[0]
    assert depth == 1
    l = 0
    wi = w_in[l]
    sp = [CONV_DIM, CONV_DIM + KEY_DIM, CONV_DIM + KEY_DIM + HEADS, CONV_DIM + KEY_DIM + 2 * HEADS]
    w_ab = jnp.concatenate([wi[:, sp[1]:sp[3]], jnp.zeros((D_MODEL, LANES - 2 * HEADS), F32)], axis=1)
    lanes_pad = lambda a: jnp.concatenate([a, jnp.zeros((LANES - a.shape[0],), F32)])[None, :]
    row = lambda a: a.reshape(1, -1)
    w = {
        "norm_mix_g": row(norm_mix_g[l]),
        "w_main": jnp.concatenate([wi[:, :sp[1]], wi[:, sp[3]:]], axis=1).astype(BF16),
        "w_ab": w_ab.astype(BF16),
        "a_log": lanes_pad(gdn_a_log[l]),
        "dt_bias": lanes_pad(gdn_dt_bias[l]),
        "conv_w": gdn_conv_w[l],
        "gdn_norm_g": row(gdn_norm_g[l]),
        "diff_lambda": diff_lambda[l],
        "diff_norm_g_col": diff_norm_g[l].reshape(HEAD_DIM, 1),
        "w_out": w_out[l].astype(BF16),
        "norm_mem_g": row(norm_mem_g[l]),
        "mem_norm_g": row(mem_norm_g[l]),
        "w_mq": w_mq[l].astype(BF16),
        "w_mkv": w_mkv[l].astype(BF16),
        "w_mo": w_mo[l].astype(BF16),
        "norm_ffn_g": row(norm_ffn_g[l]),
        "w_up": w_up[l].astype(BF16),
        "w_down": w_down[l].astype(BF16),
        "final_norm_g": row(final_norm_g),
    }
    n_mem = mem_prompt.shape[1]
    mk, mv, mkb, mvb = _mem_kv(mem_prompt.reshape(bp * n_mem, D_MODEL), w, _tile(bp * n_mem, ROW_TILE))
    mk = mk.reshape(bp, n_mem, HEADS, HEAD_DIM)
    mv = mv.reshape(bp, n_mem, HEADS, HEAD_DIM)
    mkb = mkb.reshape(bp, n_mem, KEY_DIM)
    mvb = mvb.reshape(bp, n_mem, KEY_DIM)

    zeros_conv = jnp.zeros((bp, CONV_W - 1, CONV_DIM), F32)
    zeros_state = jnp.zeros((bp, HEADS, HEAD_DIM, HEAD_DIM), F32)
    yp, sp_, cp, kp, vp = _layer(x_prompt, 0, zeros_conv, zeros_state, None, mkb, mvb, w)
    ys, ss, cs, ks_, vs = _layer(x_sample, p_len, state_gdn_conv[l], state_gdn[l],
                                 (cache_diff_k[l], cache_diff_v[l]),
                                 cache_mem_k[l].reshape(bs, n_mem, KEY_DIM).astype(BF16),
                                 cache_mem_v[l].reshape(bs, n_mem, KEY_DIM).astype(BF16), w)
    return (yp, ys, sp_[None], cp[None], kp[None], vp[None], mk[None], mv[None],
            ss[None], cs[None], ks_[None], vs[None])
```

```python
import functools
import math

import jax
import jax.numpy as jnp
import numpy as np
from jax import lax
from jax.experimental import pallas as pl
from jax.experimental.pallas import tpu as pltpu

D_MODEL = 1024
CHUNK = 64
HEADS = 4
HEAD_DIM = 128
KEY_DIM = HEADS * HEAD_DIM
CONV_DIM = 3 * KEY_DIM
CONV_W = 4
DQK = 64
ROT_DIM = 16
ROPE_THETA = 500000.0
N_MEM = 256
D_FF = 4 * D_MODEL
NORM_EPS = 1e-6
MAIN_COLS = CONV_DIM + 4 * KEY_DIM
LANES = 128
CONV_PAD = 8
SUM_ROWS = 16
VMEM_LIMIT = 56 * 1024 * 1024
ROW_TILE = 512
GDN_CHUNKS = 16
GDN_GROUP = 4

F32 = jnp.float32
BF16 = jnp.bfloat16
HIGHEST = lax.Precision.HIGHEST


def _dot(a, b, precision=None):
    return jnp.dot(a, b, preferred_element_type=F32, precision=precision)


def _dot_nt(a, b):
    return lax.dot_general(a, b, (((1,), (1,)), ((), ())), preferred_element_type=F32)


def _dot_tn(a, b):
    return lax.dot_general(a, b, (((0,), (0,)), ((), ())), preferred_element_type=F32)


def _rms(x, g):
    return x * lax.rsqrt(jnp.mean(x * x, axis=-1, keepdims=True) + NORM_EPS) * g


def _sigmoid(x):
    return 1.0 / (1.0 + jnp.exp(-x))


def _params(*sem):
    return pltpu.CompilerParams(dimension_semantics=sem, vmem_limit_bytes=VMEM_LIMIT)


def _proj_in_kernel(x_ref, g_ref, wm_ref, wab_ref, alog_ref, dtb_ref, cos_ref, sna_ref, snb_ref,
                    qkv_ref, z_ref, gb_ref, q_ref, kf_ref, kb_ref, vf_ref, vb_ref, vt_ref):
    x = x_ref[...]
    h = _rms(x, g_ref[...]).astype(BF16)
    q0 = CONV_DIM + KEY_DIM
    att = _dot(h, wm_ref[:, q0:])
    main = _dot(h, wm_ref[:, :q0])
    ab = _dot(h, wab_ref[...])
    qkv_ref[...] = main[:, :CONV_DIM]
    z_ref[...] = main[:, CONV_DIM:CONV_DIM + KEY_DIM]
    xa = ab + dtb_ref[...]
    softplus = jnp.maximum(xa, 0.0) + jnp.log1p(jnp.exp(-jnp.abs(xa)))
    lane = lax.broadcasted_iota(jnp.int32, ab.shape, 1)
    gb_ref[...] = jnp.where(lane < HEADS, -jnp.exp(alog_ref[...]) * softplus, _sigmoid(ab))
    cos, sna, snb = cos_ref[...], sna_ref[...], snb_ref[...]
    krs, vhs = [], []
    for hd in range(HEADS):
        sl = slice(hd * HEAD_DIM, (hd + 1) * HEAD_DIM)
        qh = att[:, hd * HEAD_DIM:(hd + 1) * HEAD_DIM]
        kh = att[:, KEY_DIM + hd * HEAD_DIM:KEY_DIM + (hd + 1) * HEAD_DIM]
        vh = att[:, 2 * KEY_DIM + hd * HEAD_DIM:2 * KEY_DIM + (hd + 1) * HEAD_DIM]
        qr = qh * cos + pltpu.roll(qh, LANES - ROT_DIM // 2, 1) * sna + pltpu.roll(qh, ROT_DIM // 2, 1) * snb
        kr = kh * cos + pltpu.roll(kh, LANES - ROT_DIM // 2, 1) * sna + pltpu.roll(kh, ROT_DIM // 2, 1) * snb
        q_ref[:, sl] = (qr * (DQK ** -0.5 * math.log2(math.e))).astype(BF16)
        krs.append(kr)
        vhs.append(vh)
        kb_ref[:, sl] = kr.astype(BF16)
        vb_ref[:, sl] = vh.astype(BF16)
        vt_ref[0, sl, :] = vh.T.astype(BF16)
    kf_ref[...] = jnp.concatenate(krs, axis=1).reshape(kf_ref.shape)
    vf_ref[...] = jnp.concatenate(vhs, axis=1).reshape(vf_ref.shape)


def _rope_tables(pos):
    inv = ROPE_THETA ** (-np.arange(0, ROT_DIM, 2, dtype=np.float64) / ROT_DIM)
    ang = pos.astype(np.float64)[:, None] * inv[None, :]
    cos, sin = np.cos(ang), np.sin(ang)
    pad = np.zeros((pos.shape[0], DQK - ROT_DIM))
    cos64 = np.concatenate([cos, cos, pad + 1.0], axis=1)
    sna64 = np.concatenate([-sin, np.zeros_like(sin), pad], axis=1)
    snb64 = np.concatenate([np.zeros_like(sin), sin, pad], axis=1)
    return tuple(jnp.asarray(np.concatenate([a, a], axis=1), F32) for a in (cos64, sna64, snb64))


def _proj_in(x2d, pos_rows, tab_rows, w, tm):
    n = x2d.shape[0]
    nt = tab_rows // tm
    cos, sna, snb = _rope_tables(pos_rows)
    row = lambda i: (i, 0)
    fix = lambda i: (0, 0)
    tab = lambda i: (i % nt, 0)
    out_shape = (
        jax.ShapeDtypeStruct((n, CONV_DIM), F32),
        jax.ShapeDtypeStruct((n, KEY_DIM), F32),
        jax.ShapeDtypeStruct((n, LANES), F32),
        jax.ShapeDtypeStruct((n, KEY_DIM), BF16),
        jax.ShapeDtypeStruct((n, HEADS, HEAD_DIM), F32),
        jax.ShapeDtypeStruct((n, KEY_DIM), BF16),
        jax.ShapeDtypeStruct((n, HEADS, HEAD_DIM), F32),
        jax.ShapeDtypeStruct((n, KEY_DIM), BF16),
        jax.ShapeDtypeStruct((n // tm, KEY_DIM, tm), BF16),
    )

    def spec(sd):
        if len(sd.shape) == 2:
            return pl.BlockSpec((tm, sd.shape[1]), row)
        return pl.BlockSpec((sd.shape[0] * tm // n,) + sd.shape[1:], lambda i: (i, 0, 0))
    out_specs = tuple(spec(sd) for sd in out_shape)
    return pl.pallas_call(
        _proj_in_kernel,
        grid=(n // tm,),
        in_specs=[
            pl.BlockSpec((tm, D_MODEL), row),
            pl.BlockSpec((1, D_MODEL), fix),
            pl.BlockSpec((D_MODEL, MAIN_COLS), fix),
            pl.BlockSpec((D_MODEL, LANES), fix),
            pl.BlockSpec((1, LANES), fix),
            pl.BlockSpec((1, LANES), fix),
            pl.BlockSpec((tm, LANES), tab),
            pl.BlockSpec((tm, LANES), tab),
            pl.BlockSpec((tm, LANES), tab),
        ],
        out_specs=out_specs,
        out_shape=out_shape,
        compiler_params=_params("parallel"),
        name="proj_in",
    )(x2d, w["norm_mix_g"], w["w_main"], w["w_ab"], w["a_log"], w["dt_bias"], cos, sna, snb)


def _gdn_kernel(qkv_ref, z_ref, gb_ref, cprev_ref, s0_ref, cw_ref, ng_ref,
                o_ref, sout_ref, xp_ref, y_ref, s_ref, *, c, nchunks):
    j = pl.program_id(1)
    rows = c * nchunks
    hc = HEADS * c

    @pl.when(j == 0)
    def _():
        xp_ref[0:CONV_PAD, :] = cprev_ref[0]
        s_ref[...] = s0_ref[0]

    xp_ref[CONV_PAD:CONV_PAD + rows, :] = qkv_ref[0]
    cw = cw_ref[...]
    xp = xp_ref[...]
    y = xp[CONV_PAD:] * cw[CONV_W - 1:CONV_W, :]
    for i in range(CONV_W - 1):
        y = y + pltpu.roll(xp, CONV_W - 1 - i, 0)[CONV_PAD:] * cw[i:i + 1, :]
    xp_ref[0:CONV_PAD, :] = xp[rows:rows + CONV_PAD]
    y_ref[...] = y * _sigmoid(y)

    ri = lax.broadcasted_iota(jnp.int32, (hc, hc), 0)
    cj = lax.broadcasted_iota(jnp.int32, (hc, hc), 1)
    same = (ri // c) == (cj // c)
    tril_bd = same & (ri >= cj)
    strict_bd = same & (ri > cj)
    eye_bd = (ri == cj).astype(F32)
    r1 = lax.broadcasted_iota(jnp.int32, (c, c), 0)
    c1 = lax.broadcasted_iota(jnp.int32, (c, c), 1)
    tril_c = (r1 >= c1).astype(F32)
    triu_c = (r1 <= c1).astype(F32)
    ng = ng_ref[...]
    gb_all = gb_ref[0]
    if rows % LANES:
        gb_all = jnp.concatenate([gb_all, jnp.zeros((LANES - rows % LANES, LANES), F32)], axis=0)
    gbt_all = gb_all.T[0:2 * HEADS, :]

    def prep(ci):
        rs = slice(ci * c, (ci + 1) * c)
        gcol = gb_ref[0, rs, :]
        gc_col = _dot(tril_c, gcol, HIGHEST)
        gc_row = _dot(gbt_all[:, rs], triu_c, HIGHEST)
        ks, kbs, qs, vbs, kbgs, qgs, kds, dcol, grow, glast = [], [], [], [], [], [], [], [], [], []
        for hd in range(HEADS):
            qh = y_ref[rs, hd * HEAD_DIM:(hd + 1) * HEAD_DIM]
            kh = y_ref[rs, KEY_DIM + hd * HEAD_DIM:KEY_DIM + (hd + 1) * HEAD_DIM]
            vh = y_ref[rs, 2 * KEY_DIM + hd * HEAD_DIM:2 * KEY_DIM + (hd + 1) * HEAD_DIM]
            qh = qh * lax.rsqrt(jnp.sum(qh * qh, axis=-1, keepdims=True) + NORM_EPS) * (HEAD_DIM ** -0.5)
            kh = kh * lax.rsqrt(jnp.sum(kh * kh, axis=-1, keepdims=True) + NORM_EPS)
            g_h = gc_col[:, hd:hd + 1]
            beta_h = gcol[:, HEADS + hd:HEADS + hd + 1]
            g_last = gc_col[c - 1:c, hd:hd + 1]
            eg = jnp.exp(g_h)
            kb = kh * beta_h
            ks.append(kh)
            kbs.append(kb)
            qs.append(qh)
            vbs.append(vh * beta_h)
            kbgs.append(kb * eg)
            qgs.append((qh * eg).astype(BF16))
            kds.append((kh * jnp.exp(g_last - g_h)).astype(BF16))
            glast.append(jnp.exp(g_last))
            dcol.append(jnp.broadcast_to(g_h, (c, hc)))
            grow.append(gc_row[hd:hd + 1, :])
        k_s = jnp.concatenate(ks, axis=0).astype(BF16)
        q_s = jnp.concatenate(qs, axis=0).astype(BF16)
        dmat = jnp.concatenate(dcol, axis=0) - jnp.concatenate(grow, axis=1)
        decay = jnp.exp(jnp.where(tril_bd, dmat, -jnp.inf))
        a_kk = _dot_nt(jnp.concatenate(kbs, axis=0).astype(BF16), k_s)
        x = jnp.where(strict_bd, -(a_kk * decay), 0.0)
        rhs = jnp.concatenate([jnp.concatenate(vbs, axis=0), jnp.concatenate(kbgs, axis=0)], axis=1)
        a_qk = (_dot_nt(q_s, k_s) * decay).astype(BF16)
        return dict(x=x, rhs=rhs.astype(BF16), a_qk=a_qk, qg=qgs, kd=kds, glast=glast)

    state = [s_ref[hd] for hd in range(HEADS)]
    for c0 in range(0, nchunks, GDN_GROUP):
        group = range(c0, min(c0 + GDN_GROUP, nchunks))
        pre = {ci: prep(ci) for ci in group}
        ps = [eye_bd + pre[ci]["x"] for ci in group]
        xps = [pre[ci]["x"] for ci in group]
        for _ in range(int(math.log2(c)) - 1):
            xbs = [xp.astype(BF16) for xp in xps]
            xps = [_dot(xb, xb) for xb in xbs]
            ps = [p + _dot(p.astype(BF16), xp.astype(BF16)) for p, xp in zip(ps, xps)]
        uws = {ci: _dot(p.astype(BF16), pre[ci]["rhs"]) for p, ci in zip(ps, group)}
        for ci in group:
            rs = slice(ci * c, (ci + 1) * c)
            d, uw = pre[ci], uws[ci]
            vnews, qss = [], []
            for hd in range(HEADS):
                hs = slice(hd * c, (hd + 1) * c)
                wq = jnp.concatenate([uw[hs, HEAD_DIM:].astype(BF16), d["qg"][hd]], axis=0)
                ws = _dot(wq, state[hd].astype(BF16))
                v_new = uw[hs, :HEAD_DIM] - ws[:c]
                vnews.append(v_new)
                qss.append(ws[c:])
                state[hd] = state[hd] * d["glast"][hd] + _dot_tn(d["kd"][hd], v_new.astype(BF16))
            o_s = jnp.concatenate(qss, axis=0) + _dot(d["a_qk"], jnp.concatenate(vnews, axis=0).astype(BF16))
            for hd in range(HEADS):
                sl = slice(hd * HEAD_DIM, (hd + 1) * HEAD_DIM)
                zh = z_ref[0, rs, sl]
                o_h = _rms(o_s[hd * c:(hd + 1) * c], ng) * (zh * _sigmoid(zh))
                o_ref[0, rs, sl] = o_h.astype(BF16)
    for hd in range(HEADS):
        s_ref[hd] = state[hd]

    @pl.when(j == pl.num_programs(1) - 1)
    def _():
        sout_ref[0] = s_ref[...]


def _gdn(qkv, z, gb, conv_prev, s0, w, nchunks):
    b, t, _ = qkv.shape
    c = min(t, CHUNK)
    rows = c * nchunks
    cprev = jnp.concatenate([jnp.zeros((b, CONV_PAD - (CONV_W - 1), CONV_DIM), F32), conv_prev], axis=1)
    blk = lambda i, j: (i, j, 0)
    kern = functools.partial(_gdn_kernel, c=c, nchunks=nchunks)
    return pl.pallas_call(
        kern,
        grid=(b, t // rows),
        in_specs=[
            pl.BlockSpec((1, rows, CONV_DIM), blk),
            pl.BlockSpec((1, rows, KEY_DIM), blk),
            pl.BlockSpec((1, rows, LANES), blk),
            pl.BlockSpec((1, CONV_PAD, CONV_DIM), lambda i, j: (i, 0, 0)),
            pl.BlockSpec((1, HEADS, HEAD_DIM, HEAD_DIM), lambda i, j: (i, 0, 0, 0)),
            pl.BlockSpec((CONV_W, CONV_DIM), lambda i, j: (0, 0)),
            pl.BlockSpec((1, HEAD_DIM), lambda i, j: (0, 0)),
        ],
        out_specs=(
            pl.BlockSpec((1, rows, KEY_DIM), blk),
            pl.BlockSpec((1, HEADS, HEAD_DIM, HEAD_DIM), lambda i, j: (i, 0, 0, 0)),
        ),
        out_shape=(
            jax.ShapeDtypeStruct((b, t, KEY_DIM), BF16),
            jax.ShapeDtypeStruct((b, HEADS, HEAD_DIM, HEAD_DIM), F32),
        ),
        scratch_shapes=[
            pltpu.VMEM((rows + CONV_PAD, CONV_DIM), F32),
            pltpu.VMEM((rows, CONV_DIM), F32),
            pltpu.VMEM((HEADS, HEAD_DIM, HEAD_DIM), F32),
        ],
        compiler_params=_params("parallel", "arbitrary"),
        name="gdn",
    )(qkv, z, gb, cprev, s0, w["conv_w"], w["gdn_norm_g"])


def _diff_attn_kernel(q_ref, qn_ref, k_ref, vt_ref, lam_ref, ng_ref, o_ref,
                      sz_ref, s0_ref, s1_ref, m_ref, acc_ref, qst_ref,
                      *, tq, tqp, tk, nblk, nq, hps, ahead, n_keys, q_pos0, lam_init):
    qi = pl.program_id(2)
    width = 2 * tqp
    heads = range(hps)
    lanes = lambda hd: slice(hd * HEAD_DIM, (hd + 1) * HEAD_DIM)
    pos_lo = q_pos0 + qi * tq
    k_lo = jnp.minimum((pos_lo // CHUNK + 1) * CHUNK, n_keys)
    k_hi = jnp.minimum(((pos_lo + tq - 1) // CHUNK + 1) * CHUNK, n_keys)
    n_full = k_lo // tk
    last = (k_hi + tk - 1) // tk - 1

    def scores(jb, s_ref):
        jl = jnp.minimum(jb, nblk - 1)
        ks = pl.ds(pl.multiple_of(jl * tk, tk), tk)
        for hd in heads:
            s_ref[hd, :, 0:width] = _dot(k_ref[0, ks, lanes(hd)], qst_ref[hd])

    def start(qx_ref):
        for hd in heads:
            q = qx_ref[0, :, lanes(hd)]
            if tqp > tq:
                q = jnp.concatenate([q, jnp.zeros((tqp - tq, HEAD_DIM), q.dtype)], axis=0)
            lane = lax.broadcasted_iota(jnp.int32, q.shape, 1)
            zero = jnp.zeros_like(q)
            qs = jnp.concatenate([jnp.where(lane < DQK, q, zero), jnp.where(lane >= DQK, q, zero)], axis=0)
            qst_ref[hd] = qs.T
        m_ref[...] = jnp.full(m_ref.shape, -jnp.inf, F32)
        acc_ref[...] = jnp.zeros(acc_ref.shape, F32)
        scores(0, sz_ref)

    def apply_mask(jb, s_ref):
        col = lax.broadcasted_iota(jnp.int32, (tk, width), 1)
        qpos = pos_lo + jnp.where(col >= tqp, col - tqp, col)
        kpos = jb * tk + lax.broadcasted_iota(jnp.int32, (tk, width), 0)
        vis = kpos < jnp.minimum((qpos // CHUNK + 1) * CHUNK, n_keys)
        for hd in heads:
            s_ref[hd, :, 0:width] = jnp.where(vis, s_ref[hd, :, 0:width], -jnp.inf)

    def mask(jb, s_ref):
        @pl.when(jb >= n_full)
        def _():
            apply_mask(jb, s_ref)

    def softmax_pv(jb, s_ref):
        ones = jnp.ones((SUM_ROWS, tk), BF16)
        for hd in heads:
            st = s_ref[hd, :, 0:width]
            m_old = m_ref[hd]
            m_new = jnp.maximum(m_old, jnp.max(st, axis=0, keepdims=True))
            alpha = jnp.exp2(m_old - m_new)
            p = jnp.exp2(st - m_new)
            vt1 = jnp.concatenate([vt_ref[jnp.minimum(jb, nblk - 1), lanes(hd), :], ones], axis=0)
            acc_ref[hd] = alpha * acc_ref[hd] + _dot(vt1, p.astype(BF16))
            m_ref[hd] = m_new

    def first_block():
        mask(0, sz_ref)
        scores(1, s1_ref)
        softmax_pv(0, sz_ref)

    @pl.when(qi == 0)
    def _():
        start(q_ref)

    @pl.when((last >= 1) & ((qi == 0) if ahead else True))
    def _():
        first_block()

    def full_pair(i, carry):
        j = 1 + 2 * i
        scores(j + 1, s0_ref)
        softmax_pv(j, s1_ref)
        scores(j + 2, s1_ref)
        softmax_pv(j + 1, s0_ref)
        return carry

    def full_quad(i, carry):
        full_pair(2 * i, carry)
        return full_pair(2 * i + 1, carry)

    def full_octo(i, carry):
        full_quad(2 * i, carry)
        return full_quad(2 * i + 1, carry)

    def edge_pair(i, carry):
        j = 1 + 2 * i
        mask(j, s1_ref)
        scores(j + 1, s0_ref)
        softmax_pv(j, s1_ref)

        @pl.when(j + 1 < last)
        def _():
            mask(j + 1, s0_ref)
            scores(j + 2, s1_ref)
            softmax_pv(j + 1, s0_ref)
        return carry

    n_free = jnp.maximum(jnp.minimum(n_full, last) - 1, 0)
    lax.fori_loop(0, n_free // 8, full_octo, 0)
    lax.fori_loop(2 * (n_free // 8), n_free // 4, full_quad, 0)
    lax.fori_loop(2 * (n_free // 4), n_free // 2, full_pair, 0)
    lax.fori_loop(n_free // 2, last // 2, edge_pair, 0)

    def finish(s_ref, more):
        apply_mask(last, s_ref)
        softmax_pv(last, s_ref)
        outs = [acc_ref[hd, 0:HEAD_DIM, :] * (1.0 / acc_ref[hd, HEAD_DIM:HEAD_DIM + 1, :]) for hd in heads]
        if more:
            start(qn_ref)
            if ahead:
                scores(1, s1_ref)
                softmax_pv(0, sz_ref)
        lf = lam_ref[...]
        lam = (jnp.exp(jnp.sum(lf[0:1] * lf[1:2], axis=-1, keepdims=True))
               - jnp.exp(jnp.sum(lf[2:3] * lf[3:4], axis=-1, keepdims=True)) + lam_init)
        for hd in heads:
            od = outs[hd][:, :tqp] - lam * outs[hd][:, tqp:]
            ms = jnp.mean(od * od, axis=0, keepdims=True)
            on = od * lax.rsqrt(ms + NORM_EPS) * ng_ref[...] * (1.0 - lam_init)
            o_ref[0, :, lanes(hd)] = on.T[:tq].astype(BF16)

    for more in ((True, False) if nq > 1 else (False,)):
        step_ok = (qi < nq - 1) if more else (qi == nq - 1)

        @pl.when(step_ok & (last == 0))
        def _():
            finish(sz_ref, more)

        @pl.when(step_ok & (last >= 1) & (last % 2 == 1))
        def _():
            finish(s1_ref, more)

        @pl.when(step_ok & (last >= 1) & (last % 2 == 0))
        def _():
            finish(s0_ref, more)


def _diff_attn(q, k, vt, w, n_keys, q_pos0, tq, tk):
    b, t, _ = q.shape
    tkp = k.shape[1]
    nblk = tkp // tk
    nq = t // tq
    tqp = max(tq, LANES)
    hps = HEADS if nq == 1 else 1
    wide = hps * HEAD_DIM
    lam_init = 0.8 - 0.6 * math.exp(-0.3 * 0)

    def bounds(qi):
        k_lo = min(((q_pos0 + qi * tq) // CHUNK + 1) * CHUNK, n_keys)
        k_hi = min(((q_pos0 + qi * tq + tq - 1) // CHUNK + 1) * CHUNK, n_keys)
        return k_lo // tk, -(-k_hi // tk) - 1
    ahead = nq > 1 and all(min(bounds(qi)) >= 1 for qi in range(1, nq))
    kern = functools.partial(_diff_attn_kernel, tq=tq, tqp=tqp, tk=tk, nblk=nblk, nq=nq, hps=hps, ahead=ahead,
                             n_keys=n_keys, q_pos0=q_pos0, lam_init=lam_init)
    return pl.pallas_call(
        kern,
        grid=(b, HEADS // hps, nq),
        in_specs=[
            pl.BlockSpec((1, tq, wide), lambda i, h, j: (i, j, h)),
            pl.BlockSpec((1, tq, wide), lambda i, h, j: (i, jnp.minimum(j + 1, nq - 1), h)),
            pl.BlockSpec((1, tkp, wide), lambda i, h, j: (i, 0, h)),
            pl.BlockSpec((nblk, wide, tk), lambda i, h, j: (i, h, 0)),
            pl.BlockSpec((4, DQK), lambda i, h, j: (0, 0)),
            pl.BlockSpec((HEAD_DIM, 1), lambda i, h, j: (0, 0)),
        ],
        out_specs=pl.BlockSpec((1, tq, wide), lambda i, h, j: (i, j, h)),
        out_shape=jax.ShapeDtypeStruct((b, t, KEY_DIM), BF16),
        scratch_shapes=[
            pltpu.VMEM((hps, tk, 2 * tqp + LANES), F32),
            pltpu.VMEM((hps, tk, 2 * tqp + LANES), F32),
            pltpu.VMEM((hps, tk, 2 * tqp + LANES), F32),
            pltpu.VMEM((hps, 1, 2 * tqp), F32),
            pltpu.VMEM((hps, HEAD_DIM + SUM_ROWS, 2 * tqp), F32),
            pltpu.VMEM((hps, HEAD_DIM, 2 * tqp), BF16),
        ],
        compiler_params=_params("arbitrary", "arbitrary", "arbitrary"),
        name="diff_attn",
    )(q, q, k, vt, w["diff_lambda"], w["diff_norm_g_col"])


def _kv_cache_kernel(ck_ref, cv_ref, kn_ref, vn_ref, k_ref, vt_ref, *, p_len, t_new, pad_rows):
    k_ref[0, 0:p_len, :] = ck_ref[0].reshape(p_len, KEY_DIM).astype(BF16)
    zeros = jnp.zeros((pad_rows - t_new, KEY_DIM), BF16)
    k_ref[0, p_len:p_len + pad_rows, :] = jnp.concatenate([kn_ref[0], zeros], axis=0)
    vt_ref[0, :, 0:p_len] = cv_ref[0].reshape(p_len, KEY_DIM).T.astype(BF16)
    v_new = jnp.concatenate([vn_ref[0], zeros], axis=0).astype(F32)
    vt_ref[0, :, p_len:p_len + pad_rows] = v_new.T.astype(BF16)


def _kv_cache(cache_k, cache_v, k_new, v_new):
    b, p_len = cache_k.shape[:2]
    t_new = k_new.shape[1]
    assert p_len % LANES == 0, p_len
    pad_rows = -(-t_new // LANES) * LANES
    rows = p_len + pad_rows
    kern = functools.partial(_kv_cache_kernel, p_len=p_len, t_new=t_new, pad_rows=pad_rows)
    return pl.pallas_call(
        kern,
        grid=(b,),
        in_specs=[
            pl.BlockSpec((1, p_len, HEADS, HEAD_DIM), lambda i: (i, 0, 0, 0)),
            pl.BlockSpec((1, p_len, HEADS, HEAD_DIM), lambda i: (i, 0, 0, 0)),
            pl.BlockSpec((1, t_new, KEY_DIM), lambda i: (i, 0, 0)),
            pl.BlockSpec((1, t_new, KEY_DIM), lambda i: (i, 0, 0)),
        ],
        out_specs=(pl.BlockSpec((1, rows, KEY_DIM), lambda i: (i, 0, 0)),
                   pl.BlockSpec((1, KEY_DIM, rows), lambda i: (i, 0, 0))),
        out_shape=(jax.ShapeDtypeStruct((b, rows, KEY_DIM), BF16),
                   jax.ShapeDtypeStruct((b, KEY_DIM, rows), BF16)),
        compiler_params=_params("parallel"),
        name="kv_cache",
    )(cache_k, cache_v, k_new, v_new)


def _post_mix_kernel(x_ref, og_ref, od_ref, mk_ref, mv_ref, wo_ref, g_ref, wq_ref, wmo_ref, o_ref):
    nb, tm, _ = x_ref.shape
    rows = lambda ref: ref[...].reshape(nb * tm, ref.shape[-1])
    x1 = rows(x_ref) + _dot(rows(og_ref), wo_ref[0:KEY_DIM, :]) + _dot(rows(od_ref), wo_ref[KEY_DIM:, :])
    h2 = _rms(x1, g_ref[...]).astype(BF16)
    qm = _dot(h2, wq_ref[...]).astype(BF16)
    oms = []
    for bi in range(nb):
        outs = []
        for hd in range(HEADS):
            sl = slice(hd * HEAD_DIM, (hd + 1) * HEAD_DIM)
            mk = mk_ref[bi, :, sl]
            mv = mv_ref[bi, :, sl]
            s = _dot_nt(qm[bi * tm:(bi + 1) * tm, sl], mk) * (HEAD_DIM ** -0.5)
            p = jnp.exp(s - jnp.max(s, axis=-1, keepdims=True))
            l = jnp.sum(p, axis=-1, keepdims=True)
            outs.append(_dot((p / l).astype(BF16), mv))
        oms.append(jnp.concatenate(outs, axis=1))
    om = jnp.concatenate(oms, axis=0).astype(BF16)
    o_ref[...] = (x1 + _dot(om, wmo_ref[...])).reshape(o_ref.shape)


def _post_mix(x, og, od, mk, mv, w, tm):
    b, t, _ = x.shape
    nb = _tile(b, max(1, ROW_TILE // tm)) if tm == t else 1
    blk = lambda i, j: (i, j, 0)
    fix = lambda i, j: (0, 0)
    mem = lambda i, j: (i, 0, 0)
    return pl.pallas_call(
        _post_mix_kernel,
        grid=(b // nb, t // tm),
        in_specs=[
            pl.BlockSpec((nb, tm, D_MODEL), blk),
            pl.BlockSpec((nb, tm, KEY_DIM), blk),
            pl.BlockSpec((nb, tm, KEY_DIM), blk),
            pl.BlockSpec((nb, N_MEM, KEY_DIM), mem),
            pl.BlockSpec((nb, N_MEM, KEY_DIM), mem),
            pl.BlockSpec((D_MODEL, D_MODEL), fix, pipeline_mode=pl.Buffered(1)),
            pl.BlockSpec((1, D_MODEL), fix),
            pl.BlockSpec((D_MODEL, KEY_DIM), fix, pipeline_mode=pl.Buffered(1)),
            pl.BlockSpec((KEY_DIM, D_MODEL), fix, pipeline_mode=pl.Buffered(1)),
        ],
        out_specs=pl.BlockSpec((nb, tm, D_MODEL), blk),
        out_shape=jax.ShapeDtypeStruct((b, t, D_MODEL), F32),
        compiler_params=_params("parallel", "parallel"),
        name="post_mix",
    )(x, og, od, mk, mv, w["w_out"], w["norm_mem_g"], w["w_mq"], w["w_mo"])


def _mlp_kernel(x_ref, g_ref, wu_ref, wd_ref, fg_ref, o_ref, *, ff_blk):
    x = x_ref[...]
    hf = _rms(x, g_ref[...]).astype(BF16)
    acc = x
    for c0 in range(0, D_FF, ff_blk):
        u = jnp.maximum(_dot(hf, wu_ref[:, c0:c0 + ff_blk]), 0.0)
        acc = acc + _dot((u * u).astype(BF16), wd_ref[c0:c0 + ff_blk, :])
    o_ref[...] = _rms(acc, fg_ref[...])


def _mlp(x2d, w, tm):
    n = x2d.shape[0]
    row = lambda i: (i, 0)
    fix = lambda i: (0, 0)
    return pl.pallas_call(
        functools.partial(_mlp_kernel, ff_blk=D_MODEL),
        grid=(n // tm,),
        in_specs=[
            pl.BlockSpec((tm, D_MODEL), row),
            pl.BlockSpec((1, D_MODEL), fix),
            pl.BlockSpec((D_MODEL, D_FF), fix, pipeline_mode=pl.Buffered(1)),
            pl.BlockSpec((D_FF, D_MODEL), fix, pipeline_mode=pl.Buffered(1)),
            pl.BlockSpec((1, D_MODEL), fix),
        ],
        out_specs=pl.BlockSpec((tm, D_MODEL), row),
        out_shape=jax.ShapeDtypeStruct((n, D_MODEL), F32),
        compiler_params=_params("parallel"),
        name="mlp",
    )(x2d, w["norm_ffn_g"], w["w_up"], w["w_down"], w["final_norm_g"])


def _mem_kv_kernel(m_ref, g_ref, w_ref, k_ref, v_ref, kb_ref, vb_ref):
    mh = _rms(m_ref[...], g_ref[...]).astype(BF16)
    kv = _dot(mh, w_ref[...])
    kb_ref[...] = kv[:, :KEY_DIM].astype(BF16)
    vb_ref[...] = kv[:, KEY_DIM:].astype(BF16)
    k_ref[...] = kv[:, :KEY_DIM].reshape(k_ref.shape)
    v_ref[...] = kv[:, KEY_DIM:].reshape(v_ref.shape)


def _mem_kv(mem2d, w, tm):
    n = mem2d.shape[0]
    row = lambda i: (i, 0)
    fix = lambda i: (0, 0)
    return pl.pallas_call(
        _mem_kv_kernel,
        grid=(n // tm,),
        in_specs=[
            pl.BlockSpec((tm, D_MODEL), row),
            pl.BlockSpec((1, D_MODEL), fix),
            pl.BlockSpec((D_MODEL, 2 * KEY_DIM), fix),
        ],
        out_specs=(pl.BlockSpec((tm, HEADS, HEAD_DIM), lambda i: (i, 0, 0)),) * 2
        + (pl.BlockSpec((tm, KEY_DIM), row),) * 2,
        out_shape=(jax.ShapeDtypeStruct((n, HEADS, HEAD_DIM), F32),) * 2
        + (jax.ShapeDtypeStruct((n, KEY_DIM), BF16),) * 2,
        compiler_params=_params("parallel"),
        name="mem_kv",
    )(mem2d, w["mem_norm_g"], w["w_mkv"])


def _tile(n, pref):
    t = min(n, pref)
    assert n % t == 0, (n, t)
    return t


def _layer(x, pos0, conv_prev, s0, kv_prev, mem_k, mem_v, w):
    b, t, _ = x.shape
    n = b * t
    c = min(t, CHUNK)
    x2d = x.reshape(n, D_MODEL)
    tm = _tile(n, ROW_TILE)
    tab_rows = t if t % tm == 0 else n
    pos_rows = pos0 + (np.arange(tab_rows) % t)
    qkv, z, gb, qd, kf, kb, vf, vb, vt = _proj_in(x2d, pos_rows, tab_rows, w, tm)
    r3 = lambda a: a.reshape(b, t, -1)
    qkv3 = r3(qkv)
    og, s_new = _gdn(qkv3, r3(z), r3(gb), conv_prev, s0, w, nchunks=_tile(t // c, GDN_CHUNKS))
    conv_new = qkv3[:, t - (CONV_W - 1):, :]
    kb3 = r3(kb)
    if kv_prev is None:
        assert t % tm == 0, (t, tm)
        n_keys, tk = t, tm
    else:
        n_keys = kv_prev[0].shape[1] + t
        kb3, vt = _kv_cache(kv_prev[0], kv_prev[1], kb3, r3(vb))
        tk = kb3.shape[1]
    od = _diff_attn(r3(qd), kb3, vt, w, n_keys=n_keys, q_pos0=pos0, tq=_tile(t, ROW_TILE), tk=tk)
    x2 = _post_mix(x, og, od, mem_k, mem_v, w, tm=_tile(t, 2 * ROW_TILE))
    y = _mlp(x2.reshape(n, D_MODEL), w, _tile(n, 2 * ROW_TILE)).reshape(b, t, D_MODEL)
    h4 = lambda a: a.reshape(b, t, HEADS, HEAD_DIM)
    return y, s_new, conv_new, h4(kf), h4(vf)


def kernel(x_prompt, x_sample, mem_prompt, cache_diff_k, cache_diff_v, cache_mem_k, cache_mem_v, state_gdn, state_gdn_conv, norm_mix_g, w_in, gdn_conv_w, gdn_a_log, gdn_dt_bias, gdn_norm_g, diff_lambda, diff_norm_g, w_out, norm_mem_g, mem_norm_g, w_mq, w_mkv, w_mo, norm_ffn_g, w_up, w_down, final_norm_g):
    bp, tp, _ = x_prompt.shape
    bs, ts, _ = x_sample.shape
    p_len = cache_diff_k.shape[2]
    depth = w_in.shape[0]
    assert depth == 1
    l = 0
    wi = w_in[l]
    sp = [CONV_DIM, CONV_DIM + KEY_DIM, CONV_DIM + KEY_DIM + HEADS, CONV_DIM + KEY_DIM + 2 * HEADS]
    w_ab = jnp.concatenate([wi[:, sp[1]:sp[3]], jnp.zeros((D_MODEL, LANES - 2 * HEADS), F32)], axis=1)
    lanes_pad = lambda a: jnp.concatenate([a, jnp.zeros((LANES - a.shape[0],), F32)])[None, :]
    row = lambda a: a.reshape(1, -1)
    w = {
        "norm_mix_g": row(norm_mix_g[l]),
        "w_main": jnp.concatenate([wi[:, :sp[1]], wi[:, sp[3]:]], axis=1).astype(BF16),
        "w_ab": w_ab.astype(BF16),
        "a_log": lanes_pad(gdn_a_log[l]),
        "dt_bias": lanes_pad(gdn_dt_bias[l]),
        "conv_w": gdn_conv_w[l],
        "gdn_norm_g": row(gdn_norm_g[l]),
        "diff_lambda": diff_lambda[l],
        "diff_norm_g_col": diff_norm_g[l].reshape(HEAD_DIM, 1),
        "w_out": w_out[l].astype(BF16),
        "norm_mem_g": row(norm_mem_g[l]),
        "mem_norm_g": row(mem_norm_g[l]),
        "w_mq": w_mq[l].astype(BF16),
        "w_mkv": w_mkv[l].astype(BF16),
        "w_mo": w_mo[l].astype(BF16),
        "norm_ffn_g": row(norm_ffn_g[l]),
        "w_up": w_up[l].astype(BF16),
        "w_down": w_down[l].astype(BF16),
        "final_norm_g": row(final_norm_g),
    }
    n_mem = mem_prompt.shape[1]
    mk, mv, mkb, mvb = _mem_kv(mem_prompt.reshape(bp * n_mem, D_MODEL), w, _tile(bp * n_mem, ROW_TILE))
    mk = mk.reshape(bp, n_mem, HEADS, HEAD_DIM)
    mv = mv.reshape(bp, n_mem, HEADS, HEAD_DIM)
    mkb = mkb.reshape(bp, n_mem, KEY_DIM)
    mvb = mvb.reshape(bp, n_mem, KEY_DIM)

    zeros_conv = jnp.zeros((bp, CONV_W - 1, CONV_DIM), F32)
    zeros_state = jnp.zeros((bp, HEADS, HEAD_DIM, HEAD_DIM), F32)
    yp, sp_, cp, kp, vp = _layer(x_prompt, 0, zeros_conv, zeros_state, None, mkb, mvb, w)
    ys, ss, cs, ks_, vs = _layer(x_sample, p_len, state_gdn_conv[l], state_gdn[l],
                                 (cache_diff_k[l], cache_diff_v[l]),
                                 cache_mem_k[l].reshape(bs, n_mem, KEY_DIM).astype(BF16),
                                 cache_mem_v[l].reshape(bs, n_mem, KEY_DIM).astype(BF16), w)
    return (yp, ys, sp_[None], cp[None], kp[None], vp[None], mk[None], mv[None],
            ss[None], cs[None], ks_[None], vs[None])
```
